```python
import functools
import jax, jax.numpy as jnp
from jax import lax
import numpy as np

D_MODEL = 1024
BATCH = 16
SEQ = 2048
DEPTH = 1
DEC_BATCH = 32
DEC_SEQ = 4
PAST_LEN = 16384
PAGE_SIZE = 128

D_MIX = D_MODEL
D_CONV = D_MIX // 2
CONV_W = 3
N_HEADS = 8
HEAD_DIM = (D_MIX - D_CONV) // N_HEADS
N_KV = 2
N_REP = N_HEADS // N_KV
ROT_DIM = HEAD_DIM // 4
ROPE_THETA = 500000.0
N_IDX_HEADS = 8
D_IDX = 64
IDX_ROT = D_IDX // 4
K_TOP = 256
Q_BLOCK = 128
D_FF = 128 * ((8 * D_MODEL // 3 + 127) // 128)
EPS = 1e-6
IN_WIDTHS = (D_CONV, D_CONV, D_CONV, N_HEADS * HEAD_DIM, N_KV * HEAD_DIM, N_KV * HEAD_DIM,
             N_IDX_HEADS * D_IDX, D_IDX, N_IDX_HEADS)
N_IN = sum(IN_WIDTHS)

kernel_name = "hymba_conv_dsa_macaron_step"


def rmsnorm(x, g):
    xf = x.astype(jnp.float32)
    y = xf * lax.rsqrt(jnp.mean(xf * xf, axis=-1, keepdims=True) + EPS)
    return (y * g.astype(jnp.float32)).astype(x.dtype)


def swiglu(x, w_gate, w_up, w_down):
    return (jax.nn.silu(x @ w_gate) * (x @ w_up)) @ w_down


def partial_rope(x, pos, rot_dim):
    half = rot_dim // 2
    inv = jnp.power(jnp.float32(ROPE_THETA), -jnp.arange(half, dtype=jnp.float32) * (2.0 / rot_dim))
    ang = pos.astype(jnp.float32)[:, None] * inv[None, :]
    cos = jnp.cos(ang)[:, None, :]
    sin = jnp.sin(ang)[:, None, :]
    x1 = x[..., :half].astype(jnp.float32)
    x2 = x[..., half:rot_dim].astype(jnp.float32)
    rot = jnp.concatenate([x1 * cos - x2 * sin, x2 * cos + x1 * sin], axis=-1).astype(x.dtype)
    return jnp.concatenate([rot, x[..., rot_dim:]], axis=-1)


def short_conv(u_ext, w, b):
    T = u_ext.shape[1] - (CONV_W - 1)
    out = b
    for j in range(CONV_W):
        out = out + w[j] * u_ext[:, j:j + T]
    return out


def take_rows(a, idx):
    return jax.vmap(lambda ab, ib: ab[ib])(a, idx)


def indexer_scores(qi, wi, ki):
    s = jax.nn.relu(jnp.einsum('bqhd,bsd->bqhs', qi, ki).astype(jnp.float32))
    return jnp.einsum('bqhs,bqh->bqs', s, wi.astype(jnp.float32)) * (D_IDX ** -0.5 * N_IDX_HEADS ** -0.5)


def sparse_attention(q, scores, q_pos, gather_kv, k_sel):
    n_keys = scores.shape[-1]
    key_pos = jnp.arange(n_keys, dtype=jnp.int32)
    scores = jnp.where(key_pos[None, None, :] <= q_pos[None, :, None], scores, -jnp.inf)
    _, idx = lax.top_k(scores, k_sel)
    valid = idx <= q_pos[None, :, None]
    kg, vg = gather_kv(idx)
    B, Q = q.shape[:2]
    qg = q.reshape(B, Q, N_KV, N_REP, HEAD_DIM)
    logits = jnp.einsum('bqgrd,bqkgd->bqgrk', qg, kg).astype(jnp.float32) * (HEAD_DIM ** -0.5)
    logits = jnp.where(valid[:, :, None, None, :], logits, -jnp.inf)
    p = jax.nn.softmax(logits, axis=-1)
    out = jnp.einsum('bqgrk,bqkgd->bqgrd', p.astype(vg.dtype), vg)
    return out.reshape(B, Q, N_HEADS, HEAD_DIM)


def attend_prompt(q, k, v, qi, ki, wi, k_sel):
    B, T = q.shape[:2]
    n_blk = T // Q_BLOCK
    gather_kv = lambda idx: (take_rows(k, idx), take_rows(v, idx))

    def block(j):
        t0 = j * Q_BLOCK
        sl = lambda a: lax.dynamic_slice_in_dim(a, t0, Q_BLOCK, axis=1)
        q_pos = t0 + jnp.arange(Q_BLOCK, dtype=jnp.int32)
        sc = indexer_scores(sl(qi), sl(wi), ki)
        return sparse_attention(sl(q), sc, q_pos, gather_kv, k_sel)

    out = lax.map(block, jnp.arange(n_blk, dtype=jnp.int32))
    return jnp.moveaxis(out, 0, 1).reshape(B, T, N_HEADS, HEAD_DIM)


def attend_paged(q, k, v, qi, ki, wi, pool_k, pool_v, pool_idx_k, page_table, k_sel):
    Bd, T = q.shape[:2]
    past_len = page_table.shape[1] * PAGE_SIZE
    past_ki = pool_idx_k[page_table].reshape(Bd, past_len, D_IDX).astype(ki.dtype)
    ki_all = jnp.concatenate([past_ki, ki], axis=1)
    q_pos = past_len + jnp.arange(T, dtype=jnp.int32)
    sc = indexer_scores(qi, wi, ki_all)

    def gather_kv(idx):
        is_past = (idx < past_len)[..., None, None]
        ic = jnp.minimum(idx, past_len - 1)
        page = jax.vmap(lambda pt, i: pt[i])(page_table, ic // PAGE_SIZE)
        off = ic % PAGE_SIZE
        inew = jnp.clip(idx - past_len, 0, T - 1)
        kg = jnp.where(is_past, pool_k[page, off].astype(k.dtype), take_rows(k, inew))
        vg = jnp.where(is_past, pool_v[page, off].astype(v.dtype), take_rows(v, inew))
        return kg, vg

    return sparse_attention(q, sc, q_pos, gather_kv, k_sel)


def decoder_layer(x, pos, conv_prev, attend, ffn1_norm, ffn1_w_gate, ffn1_w_up, ffn1_w_down,
                  mix_norm, w_in, conv_w, conv_b, q_norm, k_norm, w_out,
                  ffn2_norm, ffn2_w_gate, ffn2_w_up, ffn2_w_down):
    B, T, _ = x.shape
    x = x + 0.5 * swiglu(rmsnorm(x, ffn1_norm), ffn1_w_gate, ffn1_w_up, ffn1_w_down)
    h = rmsnorm(x, mix_norm)
    split_at = [int(s) for s in np.cumsum(IN_WIDTHS)[:-1]]
    gb, gc, hc, q, k, v, qi, ki, wi = jnp.split(h @ w_in, split_at, axis=-1)
    u = gc * hc
    u_ext = jnp.concatenate([conv_prev.astype(u.dtype), u], axis=1)
    y_conv = gb * short_conv(u_ext, conv_w, conv_b)
    conv_state = u_ext[:, -(CONV_W - 1):]
    q = partial_rope(rmsnorm(q.reshape(B, T, N_HEADS, HEAD_DIM), q_norm), pos, ROT_DIM)
    k = partial_rope(rmsnorm(k.reshape(B, T, N_KV, HEAD_DIM), k_norm), pos, ROT_DIM)
    v = v.reshape(B, T, N_KV, HEAD_DIM)
    qi = partial_rope(qi.reshape(B, T, N_IDX_HEADS, D_IDX), pos, IDX_ROT)
    ki = partial_rope(ki[:, :, None, :], pos, IDX_ROT)[:, :, 0]
    attn = attend(q, k, v, qi, ki, wi)
    x = x + jnp.concatenate([y_conv, attn.reshape(B, T, N_HEADS * HEAD_DIM)], axis=-1) @ w_out
    x = x + 0.5 * swiglu(rmsnorm(x, ffn2_norm), ffn2_w_gate, ffn2_w_up, ffn2_w_down)
    return x, k, v, ki, conv_state


def setup_inputs(seed: int = 0) -> dict:
    key = jax.random.key(seed)
    ks = jax.random.split(key, 24)
    n_pages = PAST_LEN // PAGE_SIZE
    n_pool = (DEC_BATCH * n_pages * 5) // 4
    nrm = lambda k, shape, s=1.0: jax.random.normal(k, shape, jnp.float32) * s
    gain = lambda k, shape: 1.0 + 0.01 * jax.random.normal(k, shape, jnp.float32)
    page_table = jax.random.permutation(ks[6], n_pool)[:DEC_BATCH * n_pages].reshape(DEC_BATCH, n_pages).astype(jnp.int32)
    return {
        "x_prompt": nrm(ks[0], (BATCH, SEQ, D_MODEL)),
        "x_sample": nrm(ks[1], (DEC_BATCH, DEC_SEQ, D_MODEL)),
        "cache_k": nrm(ks[2], (DEPTH, n_pool, PAGE_SIZE, N_KV, HEAD_DIM)),
        "cache_v": nrm(ks[3], (DEPTH, n_pool, PAGE_SIZE, N_KV, HEAD_DIM)),
        "cache_idx_k": nrm(ks[4], (DEPTH, n_pool, PAGE_SIZE, D_IDX)),
        "cache_conv": nrm(ks[5], (DEPTH, DEC_BATCH, CONV_W - 1, D_CONV)),
        "page_table": page_table,
        "ffn1_norm": gain(ks[7], (DEPTH, D_MODEL)),
        "ffn1_w_gate": nrm(ks[8], (DEPTH, D_MODEL, D_FF), D_MODEL ** -0.5),
        "ffn1_w_up": nrm(ks[9], (DEPTH, D_MODEL, D_FF), D_MODEL ** -0.5),
        "ffn1_w_down": nrm(ks[10], (DEPTH, D_FF, D_MODEL), D_FF ** -0.5),
        "mix_norm": gain(ks[11], (DEPTH, D_MODEL)),
        "w_in": nrm(ks[12], (DEPTH, D_MODEL, N_IN), D_MODEL ** -0.5),
        "conv_w": nrm(ks[13], (DEPTH, CONV_W, D_CONV), CONV_W ** -0.5),
        "conv_b": nrm(ks[14], (DEPTH, D_CONV), 0.01),
        "q_norm": gain(ks[15], (DEPTH, HEAD_DIM)),
        "k_norm": gain(ks[16], (DEPTH, HEAD_DIM)),
        "w_out": nrm(ks[17], (DEPTH, D_MIX, D_MODEL), D_MIX ** -0.5),
        "ffn2_norm": gain(ks[18], (DEPTH, D_MODEL)),
        "ffn2_w_gate": nrm(ks[19], (DEPTH, D_MODEL, D_FF), D_MODEL ** -0.5),
        "ffn2_w_up": nrm(ks[20], (DEPTH, D_MODEL, D_FF), D_MODEL ** -0.5),
        "ffn2_w_down": nrm(ks[21], (DEPTH, D_FF, D_MODEL), D_FF ** -0.5),
    }


def reference(x_prompt, x_sample, cache_k, cache_v, cache_idx_k, cache_conv, page_table,
              ffn1_norm, ffn1_w_gate, ffn1_w_up, ffn1_w_down, mix_norm, w_in, conv_w, conv_b,
              q_norm, k_norm, w_out, ffn2_norm, ffn2_w_gate, ffn2_w_up, ffn2_w_down):
    past_len = page_table.shape[1] * PAGE_SIZE
    pos_p = jnp.arange(SEQ, dtype=jnp.int32)
    pos_s = past_len + jnp.arange(DEC_SEQ, dtype=jnp.int32)
    k_sel_p = min(K_TOP, SEQ // 4)
    k_sel_s = min(K_TOP, (past_len + DEC_SEQ) // 4)
    yp, ys = x_prompt, x_sample
    kp_l, vp_l, ip_l, cp_l, ks_l, vs_l, is_l, cs_l = [], [], [], [], [], [], [], []
    for l in range(DEPTH):
        w = (ffn1_norm[l], ffn1_w_gate[l], ffn1_w_up[l], ffn1_w_down[l], mix_norm[l], w_in[l],
             conv_w[l], conv_b[l], q_norm[l], k_norm[l], w_out[l],
             ffn2_norm[l], ffn2_w_gate[l], ffn2_w_up[l], ffn2_w_down[l])
        conv_zero = jnp.zeros((yp.shape[0], CONV_W - 1, D_CONV), yp.dtype)
        att_p = functools.partial(attend_prompt, k_sel=k_sel_p)
        att_s = functools.partial(attend_paged, pool_k=cache_k[l], pool_v=cache_v[l],
                                  pool_idx_k=cache_idx_k[l], page_table=page_table, k_sel=k_sel_s)
        yp, kp, vp, ip, cp = decoder_layer(yp, pos_p, conv_zero, att_p, *w)
        ys, kss, vss, iss, css = decoder_layer(ys, pos_s, cache_conv[l], att_s, *w)
        kp_l.append(kp); vp_l.append(vp); ip_l.append(ip); cp_l.append(cp)
        ks_l.append(kss); vs_l.append(vss); is_l.append(iss); cs_l.append(css)
    k_prompt = jnp.stack(kp_l); v_prompt = jnp.stack(vp_l)
    idxk_prompt = jnp.stack(ip_l); conv_prompt = jnp.stack(cp_l)
    k_sample = jnp.stack(ks_l); v_sample = jnp.stack(vs_l)
    idxk_sample = jnp.stack(is_l); conv_sample = jnp.stack(cs_l)
    return (yp, ys, k_prompt, v_prompt, idxk_prompt, conv_prompt, k_sample, v_sample, idxk_sample, conv_sample)
```

```python
import functools

import jax
import jax.numpy as jnp
from jax import lax
from jax.experimental import pallas as pl
from jax.experimental.pallas import tpu as pltpu

F32 = jnp.float32
BF16 = jnp.bfloat16
I32 = jnp.int32

N_HEADS = 8
HEAD_DIM = 64
N_KV = 2
ROT_DIM = HEAD_DIM // 4
ROPE_THETA = 500000.0
N_IDX_HEADS = 8
D_IDX = 64
IDX_ROT = D_IDX // 4
K_TOP = 256
CONV_W = 3
EPS = 1e-6
IDX_SCALE = D_IDX ** -0.5 * N_IDX_HEADS ** -0.5
ATTN_SCALE = HEAD_DIM ** -0.5

LANES = 128
SUBLANES = 8
Q_BLOCK = 128
VMEM_LIMIT_BYTES = 56 * 1024 * 1024

INT_MIN = -2 ** 31
NEG_INF = float("-inf")
_NT = (((1,), (1,)), ((), ()))


def _rmsnorm(x, g):
    y = x * lax.rsqrt(jnp.mean(x * x, axis=-1, keepdims=True) + EPS)
    return y * g


def _swiglu_residual(x, g_ref, wg_ref, wu_ref, wd_ref):
    h = _rmsnorm(x, g_ref[...]).astype(BF16)
    gate = jnp.dot(h, wg_ref[...], preferred_element_type=F32)
    up = jnp.dot(h, wu_ref[...], preferred_element_type=F32)
    act = (gate * jax.nn.sigmoid(gate) * up).astype(BF16)
    return x + 0.5 * jnp.dot(act, wd_ref[...], preferred_element_type=F32)


def _ffn_kernel(x_ref, g_ref, wg_ref, wu_ref, wd_ref, o_ref):
    o_ref[...] = _swiglu_residual(x_ref[...], g_ref, wg_ref, wu_ref, wd_ref)


def _out_ffn_kernel(x_ref, yc_ref, at_ref, wo_ref, g_ref, wg_ref, wu_ref, wd_ref, o_ref):
    mixed = jnp.concatenate([yc_ref[...], at_ref[...]], axis=-1)
    x = x_ref[...] + jnp.dot(mixed, wo_ref[...], preferred_element_type=F32)
    o_ref[...] = _swiglu_residual(x, g_ref, wg_ref, wu_ref, wd_ref)


def _const_spec(shape):
    return pl.BlockSpec(shape, lambda i: (0,) * len(shape))


def _ffn_call(x, g, wg, wu, wd, tm, mix=None):
    n_tok, d = x.shape
    d_ff = wg.shape[1]
    tok = lambda w: pl.BlockSpec((tm, w), lambda i: (i, 0))
    w_specs = [_const_spec((1, d)), _const_spec((d, d_ff)), _const_spec((d, d_ff)), _const_spec((d_ff, d))]
    if mix is None:
        body, args, specs = _ffn_kernel, (x, g, wg, wu, wd), [tok(d)] + w_specs
    else:
        yc, at, wo = mix
        body, args = _out_ffn_kernel, (x, yc, at, wo, g, wg, wu, wd)
        specs = [tok(d), tok(yc.shape[1]), tok(at.shape[1]), _const_spec(wo.shape)] + w_specs
    return pl.pallas_call(
        body,
        out_shape=jax.ShapeDtypeStruct((n_tok, d), F32),
        grid=(n_tok // tm,),
        in_specs=specs,
        out_specs=tok(d),
        compiler_params=pltpu.CompilerParams(
            dimension_semantics=("arbitrary",), vmem_limit_bytes=VMEM_LIMIT_BYTES),
        name="ffn" if mix is None else "out_ffn",
    )(*args)


def _seg_mean(x2, seg):
    hi = x2.astype(BF16)
    lo = (x2 - hi.astype(F32)).astype(BF16)
    return (jnp.dot(hi, seg, preferred_element_type=F32)
            + jnp.dot(lo, seg, preferred_element_type=F32))


def _rope(x, c, sa, sb):
    half = ROT_DIM // 2
    n = x.shape[-1]
    return x * c + pltpu.roll(x, n - half, 1) * sa + pltpu.roll(x, half, 1) * sb


def _dup_halves(x, lane):
    r = pltpu.roll(x, LANES // 2, 1)
    low = lane < LANES // 2
    return jnp.where(low, x, r), jnp.where(low, r, x)


def _mix_kernel(*refs, tm, seq_len, has_prev, d_conv):
    it = iter(refs)
    x_ref, g_ref, win_ref, cw_ref, cb_ref, qn_ref, kn_ref, seg_ref, cos_ref, sa_ref, sb_ref = (
        next(it) for _ in range(11))
    if has_prev:
        e1_ref, e2_ref = next(it), next(it)
    (yc_ref, qb_ref, k_ref, v_ref, kiw_ref, kd_ref, vt_ref, kid_ref, qi_ref, wt_ref, u_ref,
     uext_ref) = (next(it) for _ in range(12))

    i = pl.program_id(0)
    h = _rmsnorm(x_ref[...], g_ref[...]).astype(BF16)
    proj = jnp.dot(h, win_ref[...], preferred_element_type=F32)
    o_q = 3 * d_conv
    o_k = o_q + N_HEADS * HEAD_DIM
    o_v = o_k + N_KV * HEAD_DIM
    o_qi = o_v + N_KV * HEAD_DIM
    o_ki = o_qi + N_IDX_HEADS * D_IDX

    gb = proj[:, 0:d_conv]
    u = proj[:, d_conv:2 * d_conv] * proj[:, 2 * d_conv:3 * d_conv]

    @pl.when(i == 0)
    def _():
        uext_ref[0:SUBLANES, :] = jnp.zeros((SUBLANES, d_conv), F32)

    uext_ref[SUBLANES:SUBLANES + tm, :] = u
    u1 = uext_ref[SUBLANES - 1:SUBLANES - 1 + tm, :]
    u2 = uext_ref[SUBLANES - 2:SUBLANES - 2 + tm, :]
    t = (i * tm + lax.broadcasted_iota(I32, (tm, d_conv), 0)) & (seq_len - 1)
    p1 = e1_ref[...] if has_prev else 0.0
    p2 = e2_ref[...] if has_prev else 0.0
    u1 = jnp.where(t >= 1, u1, p1)
    u2 = jnp.where(t >= 2, u2, p2)
    cw = cw_ref[...]
    conv = cb_ref[...] + cw[0:1, :] * u2
    conv = conv + cw[1:2, :] * u1
    conv = conv + cw[2:3, :] * u
    yc_ref[...] = (gb * conv).astype(BF16)
    tail = u[tm - SUBLANES:tm, :]
    uext_ref[0:SUBLANES, :] = tail
    u_ref[...] = u if has_prev else tail

    c, sa, sb = cos_ref[...], sa_ref[...], sb_ref[...]
    seg = seg_ref[...]
    lane = lax.broadcasted_iota(I32, (tm, LANES), 1)

    def qk_norm_rope(xt, gain):
        ms = _seg_mean(xt * xt, seg)
        return _rope(xt * lax.rsqrt(ms + EPS) * gain, c, sa, sb)

    for j in range(N_HEADS * HEAD_DIM // LANES):
        sl = slice(j * LANES, (j + 1) * LANES)
        qt = qk_norm_rope(proj[:, o_q + j * LANES:o_q + (j + 1) * LANES], qn_ref[...])
        qb_ref[:, sl] = (qt * ATTN_SCALE).astype(BF16)
        qit = _rope(proj[:, o_qi + j * LANES:o_qi + (j + 1) * LANES], c, sa, sb)
        qi_ref[:, sl] = qit.astype(BF16)

    kt = qk_norm_rope(proj[:, o_k:o_v], kn_ref[...])
    k_ref[...] = kt
    k0, k1 = _dup_halves(kt, lane)
    kd_ref[:, 0:LANES] = k0.astype(BF16)
    kd_ref[:, LANES:2 * LANES] = k1.astype(BF16)

    vt = proj[:, o_v:o_qi]
    v_ref[...] = vt
    vt_ref[...] = vt.T.astype(BF16)

    low = lane < D_IDX
    kiw = _rope(proj[:, o_ki:o_ki + LANES], jnp.where(low, c, 1.0), jnp.where(low, sa, 0.0),
                jnp.where(low, sb, 0.0))
    kiw_ref[...] = kiw
    kid_ref[...] = _dup_halves(kiw, lane)[0].astype(BF16)
    wt_ref[...] = kiw.T[D_IDX:D_IDX + N_IDX_HEADS, :] * IDX_SCALE


def _mix_call(x, g, win, cw, cb, qn, kn, seg, tables, tm, seq_len, prev=None):
    n_tok, d = x.shape
    d_conv = cw.shape[1]
    n_proj = win.shape[1]
    tps = max(seq_len // tm, 1)
    has_prev = prev is not None
    tok = lambda w: pl.BlockSpec((tm, w), lambda i: (i, 0))
    tab = pl.BlockSpec((tm, LANES), lambda i: (i % tps, 0))
    in_specs = [tok(d), _const_spec((1, d)), _const_spec((d, n_proj)), _const_spec(cw.shape),
                _const_spec((1, d_conv)), _const_spec((1, LANES)), _const_spec((1, LANES)),
                _const_spec((LANES, LANES)), tab, tab, tab]
    args = [x, g, win, cw, cb, qn, kn, seg, *tables]
    if has_prev:
        in_specs += [tok(d_conv), tok(d_conv)]
        args += list(prev)
    qw = N_HEADS * HEAD_DIM
    u_rows = tm if has_prev else SUBLANES
    n_tiles = n_tok // tm
    out_shape = [
        jax.ShapeDtypeStruct((n_tok, d_conv), BF16),
        jax.ShapeDtypeStruct((n_tok, qw), BF16),
        jax.ShapeDtypeStruct((n_tok, LANES), F32),
        jax.ShapeDtypeStruct((n_tok, LANES), F32),
        jax.ShapeDtypeStruct((n_tok, LANES), F32),
        jax.ShapeDtypeStruct((n_tok, 2 * LANES), BF16),
        jax.ShapeDtypeStruct((LANES, n_tok), BF16),
        jax.ShapeDtypeStruct((n_tok, LANES), BF16),
        jax.ShapeDtypeStruct((n_tok, N_IDX_HEADS * D_IDX), BF16),
        jax.ShapeDtypeStruct((N_IDX_HEADS, n_tok), F32),
        jax.ShapeDtypeStruct((n_tiles * u_rows, d_conv), F32),
    ]
    out_specs = [tok(d_conv), tok(qw), tok(LANES), tok(LANES), tok(LANES), tok(2 * LANES),
                 pl.BlockSpec((LANES, tm), lambda i: (0, i)), tok(LANES), tok(N_IDX_HEADS * D_IDX),
                 pl.BlockSpec((N_IDX_HEADS, tm), lambda i: (0, i)),
                 pl.BlockSpec((u_rows, d_conv), lambda i: (i, 0))]
    return pl.pallas_call(
        functools.partial(_mix_kernel, tm=tm, seq_len=seq_len, has_prev=has_prev, d_conv=d_conv),
        out_shape=out_shape,
        grid=(n_tiles,),
        in_specs=in_specs,
        out_specs=out_specs,
        scratch_shapes=[pltpu.VMEM((tm + SUBLANES, d_conv), F32)],
        compiler_params=pltpu.CompilerParams(
            dimension_semantics=("arbitrary",), vmem_limit_bytes=VMEM_LIMIT_BYTES),
        name="mix_sample" if has_prev else "mix_prompt",
    )(*args)


def _sortable_key(score):
    bits = pltpu.bitcast(jnp.where(score == 0.0, 0.0, score), I32)
    return jnp.where(bits < 0, bits ^ 0x7FFFFFFF, bits)


def _radix_select(count_ge, k_sel, shape):
    def body(i, t):
        cand = t + lax.shift_left(jnp.int32(1), 31 - i)
        return jnp.where(count_ge(cand) >= k_sel, cand, t)
    return lax.fori_loop(0, 32, body, jnp.full(shape, INT_MIN, I32))


def _tie_cut(count_eq_below, need, n_bits, shape):
    def body(i, j):
        cand = j + lax.shift_left(jnp.int32(1), n_bits - 1 - i)
        return jnp.where(count_eq_below(cand) < need, cand, j)
    return lax.fori_loop(0, n_bits, body, jnp.zeros(shape, I32))


def _split_heads_rhs(q_tile, lane):
    zero = jnp.zeros_like(q_tile)
    low = lane < LANES // 2
    return jnp.concatenate([jnp.where(low, q_tile, zero), jnp.where(low, zero, q_tile)], axis=0)


def _col_sum(x):
    s = x.shape[0]
    if s % 32 == 0 and s > 32:
        x = x.reshape(s // 32, 32, x.shape[1]).sum(axis=0)
    return x.sum(axis=0, keepdims=True)


def _attn_prompt_kernel(qi_ref, wt_ref, qb_ref, kid_ref, kd_ref, vt_ref, o_ref, key_ref, bias_ref,
                        ot_ref, *, s_variants, k_sel, n_bits):
    j = pl.program_id(1)
    t0 = j * Q_BLOCK
    blocks_per_variant = s_variants[0] // Q_BLOCK
    lane_q = lax.broadcasted_iota(I32, (Q_BLOCK, LANES), 1)

    def process(s):
        kid = kid_ref[0:s, :]
        score = jnp.zeros((s, LANES), F32)
        for p in range(N_IDX_HEADS // 2):
            rhs = _split_heads_rhs(qi_ref[:, p * LANES:(p + 1) * LANES], lane_q)
            d = lax.dot_general(kid, rhs, _NT, preferred_element_type=F32)
            score = score + jnp.maximum(d[:, 0:LANES], 0.0) * wt_ref[2 * p:2 * p + 1, :]
            score = score + jnp.maximum(d[:, LANES:], 0.0) * wt_ref[2 * p + 1:2 * p + 2, :]
        row = lax.broadcasted_iota(I32, (s, LANES), 0)
        tq = t0 + lax.broadcasted_iota(I32, (s, LANES), 1)
        key_ref[0:s, :] = jnp.where(row <= tq, _sortable_key(score), INT_MIN)

        def count_ge(cand):
            return _col_sum((key_ref[0:s, :] >= cand).astype(I32))

        thr = jnp.maximum(_radix_select(count_ge, k_sel, (1, LANES)), INT_MIN + 1)
        key = key_ref[0:s, :]
        ge = key >= thr
        bias_ref[0:s, :] = jnp.where(ge, 0.0, NEG_INF)
        n_ge = _col_sum(ge.astype(I32))

        @pl.when(jnp.max(n_ge) > k_sel)
        def _():
            key = key_ref[0:s, :]
            row = lax.broadcasted_iota(I32, (s, LANES), 0)
            gt = key > thr
            need = k_sel - _col_sum(gt.astype(I32))

            def count_eq_below(cand):
                k2 = key_ref[0:s, :]
                r2 = lax.broadcasted_iota(I32, (s, LANES), 0)
                return _col_sum(jnp.where(k2 == thr, (r2 < cand).astype(I32), 0))

            cut = _tie_cut(count_eq_below, need, n_bits, (1, LANES))
            bias_ref[0:s, :] = jnp.where(key == thr, jnp.where(row <= cut, 0.0, NEG_INF),
                                         jnp.where(gt, 0.0, NEG_INF))

        bias = bias_ref[0:s, :]
        bias2 = jnp.concatenate([bias, bias], axis=1)
        pairs_per_kv = (N_HEADS // N_KV) // 2
        for p in range(N_HEADS // 2):
            g = p // pairs_per_kv
            rhs = _split_heads_rhs(qb_ref[:, p * LANES:(p + 1) * LANES], lane_q)
            logit = lax.dot_general(kd_ref[0:s, g * LANES:(g + 1) * LANES], rhs, _NT,
                                    preferred_element_type=F32) + bias2
            m = jnp.max(logit, axis=0, keepdims=True)
            e = jnp.exp(logit - m)
            den = jnp.sum(e, axis=0, keepdims=True)
            pv = jnp.dot(vt_ref[g * HEAD_DIM:(g + 1) * HEAD_DIM, 0:s], e.astype(BF16),
                         preferred_element_type=F32)
            pv = pv / den
            ot_ref[(2 * p) * HEAD_DIM:(2 * p + 1) * HEAD_DIM, :] = pv[:, 0:LANES]
            ot_ref[(2 * p + 1) * HEAD_DIM:(2 * p + 2) * HEAD_DIM, :] = pv[:, LANES:]
        o_ref[...] = ot_ref[...].T.astype(BF16)

    for vi, s in enumerate(s_variants):
        if len(s_variants) == 1:
            process(s)
        else:
            pl.when(j // blocks_per_variant == vi)(functools.partial(process, s))


def _attn_prompt_call(qi, wt, qb, kid, kd, vt, batch, seq, k_sel):
    n_tok = batch * seq
    nq = seq // Q_BLOCK
    n_var = 4 if nq % 4 == 0 else 1
    s_variants = tuple(seq // n_var * (v + 1) for v in range(n_var))
    qw = qb.shape[1]
    qblk = lambda w: pl.BlockSpec((Q_BLOCK, w), lambda b, j: (b * nq + j, 0))
    per_b = lambda w: pl.BlockSpec((seq, w), lambda b, j: (b, 0))
    return pl.pallas_call(
        functools.partial(_attn_prompt_kernel, s_variants=s_variants, k_sel=k_sel,
                          n_bits=(seq - 1).bit_length()),
        out_shape=jax.ShapeDtypeStruct((n_tok, qw), BF16),
        grid=(batch, nq),
        in_specs=[qblk(qi.shape[1]),
                  pl.BlockSpec((N_IDX_HEADS, Q_BLOCK), lambda b, j: (0, b * nq + j)),
                  qblk(qw), per_b(LANES), per_b(2 * LANES),
                  pl.BlockSpec((LANES, seq), lambda b, j: (0, b))],
        out_specs=qblk(qw),
        scratch_shapes=[pltpu.VMEM((seq, LANES), I32), pltpu.VMEM((seq, LANES), F32),
                        pltpu.VMEM((qw, Q_BLOCK), F32)],
        compiler_params=pltpu.CompilerParams(
            dimension_semantics=("arbitrary", "arbitrary"), vmem_limit_bytes=VMEM_LIMIT_BYTES),
        name="attn_prompt",
    )(qi, wt, qb, kid, kd, vt)


def _attn_sample_kernel(pt_ref, qi_ref, wc_ref, qp_ref, kin_ref, kn_ref, vn_ref, cidx_ref, ck_ref,
                        cv_ref, o_ref, idx_buf, k_buf, v_buf, key_ref, bias_ref, lg_ref, sem, *, n_pages,
                        page,
                        dec_seq, k_sel, chunk):
    b = pl.program_id(0)
    nb = pl.num_programs(0)
    slot = b % 2
    past = n_pages * page
    n_keys = past + LANES
    rows = N_IDX_HEADS * SUBLANES

    def idx_copy(bb, sl, pg):
        return pltpu.make_async_copy(cidx_ref.at[pt_ref[bb, pg]],
                                     idx_buf.at[sl, pl.ds(pl.multiple_of(pg * page, page), page)],
                                     sem.at[sl])

    def kv_copy(src, dst, s_id, pg):
        return pltpu.make_async_copy(src.at[pt_ref[b, pg]],
                                     dst.at[pl.ds(pl.multiple_of(pg * page, page), page)],
                                     sem.at[s_id])

    def for_pages(fn):
        def body(pg, carry):
            fn(pg)
            return carry
        lax.fori_loop(0, n_pages, body, 0)

    @pl.when(b == 0)
    def _():
        for_pages(lambda pg: idx_copy(0, 0, pg).start())

    for_pages(lambda pg: kv_copy(ck_ref, k_buf, 2, pg).start())
    for_pages(lambda pg: kv_copy(cv_ref, v_buf, 3, pg).start())

    @pl.when(b + 1 < nb)
    def _():
        for_pages(lambda pg: idx_copy(b + 1, 1 - slot, pg).start())

    idx_buf[slot, past:n_keys, :] = jnp.zeros((LANES, D_IDX), F32)
    idx_buf[slot, past:past + dec_seq, :] = kin_ref[...]
    k_buf[past:n_keys, :] = jnp.zeros((LANES, LANES), F32)
    k_buf[past:past + dec_seq, :] = kn_ref[...]
    v_buf[past:n_keys, :] = jnp.zeros((LANES, LANES), F32)
    v_buf[past:past + dec_seq, :] = vn_ref[...]

    for_pages(lambda pg: idx_copy(b, slot, pg).wait())

    chunks = [(c0, min(chunk, past - c0)) for c0 in range(0, past, chunk)] + [(past, LANES)]

    qi = qi_ref[...]
    wc = wc_ref[...]
    for c0, n in chunks:
        keys_c = idx_buf[slot, c0:c0 + n, :].astype(BF16)
        d = lax.dot_general(qi, keys_c, _NT, preferred_element_type=F32)
        sh = jnp.maximum(d, 0.0) * wc
        score = sh[0:SUBLANES, :]
        for hh in range(1, N_IDX_HEADS):
            score = score + sh[hh * SUBLANES:(hh + 1) * SUBLANES, :]
        tslot = lax.broadcasted_iota(I32, (SUBLANES, n), 0)
        col = c0 + lax.broadcasted_iota(I32, (SUBLANES, n), 1)
        last = jnp.where(tslot < dec_seq, past + tslot, 0)
        key_ref[:, c0:c0 + n] = jnp.where(col <= last, _sortable_key(score), INT_MIN)

    def count_ge(cand):
        return jnp.sum((key_ref[...] >= cand).astype(I32), axis=1, keepdims=True)

    thr = jnp.maximum(_radix_select(count_ge, k_sel, (SUBLANES, 1)), INT_MIN + 1)
    ge = key_ref[...] >= thr
    n_ge = jnp.sum(ge.astype(I32), axis=1, keepdims=True)
    bias_ref[...] = jnp.where(ge, 0.0, NEG_INF)

    @pl.when(jnp.max(n_ge) > k_sel)
    def _():
        need = k_sel - jnp.sum((key_ref[...] > thr).astype(I32), axis=1, keepdims=True)

        def count_eq_below(cand):
            col = lax.broadcasted_iota(I32, (SUBLANES, n_keys), 1)
            return jnp.sum(jnp.where(key_ref[...] == thr, (col < cand).astype(I32), 0), axis=1,
                           keepdims=True)

        cut = _tie_cut(count_eq_below, need, (n_keys - 1).bit_length(), (SUBLANES, 1))
        key = key_ref[...]
        col = lax.broadcasted_iota(I32, (SUBLANES, n_keys), 1)
        bias_ref[...] = jnp.where(key == thr, jnp.where(col <= cut, 0.0, NEG_INF),
                                  jnp.where(key > thr, 0.0, NEG_INF))

    for_pages(lambda pg: kv_copy(ck_ref, k_buf, 2, pg).wait())
    for_pages(lambda pg: kv_copy(cv_ref, v_buf, 3, pg).wait())

    qp = qp_ref[...]
    for c0, n in chunks:
        bias = jnp.concatenate([bias_ref[:, c0:c0 + n]] * N_HEADS, axis=0)
        lg_ref[:, c0:c0 + n] = lax.dot_general(qp, k_buf[c0:c0 + n, :].astype(BF16), _NT,
                                               preferred_element_type=F32) + bias
    m = jnp.max(lg_ref[...], axis=1, keepdims=True)
    den = jnp.zeros((rows, 1), F32)
    acc = jnp.zeros((rows, LANES), F32)
    for c0, n in chunks:
        e = jnp.exp(lg_ref[:, c0:c0 + n] - m)
        den = den + jnp.sum(e, axis=1, keepdims=True)
        acc = acc + jnp.dot(e.astype(BF16), v_buf[c0:c0 + n, :].astype(BF16),
                            preferred_element_type=F32)
    o_ref[...] = acc / den


def _attn_sample_call(page_table, qi_rows, wcol, qp_rows, ki_new, k_new, v_new, cache_idx, cache_k,
                      cache_v, k_sel):
    dec_batch, n_pages = page_table.shape
    page = cache_idx.shape[1]
    dec_seq = ki_new.shape[1]
    past = n_pages * page
    n_keys = past + LANES
    rows = N_IDX_HEADS * SUBLANES
    per_b = lambda r, w: pl.BlockSpec((None, r, w), lambda b, pt: (b, 0, 0))
    any_spec = pl.BlockSpec(memory_space=pl.ANY)
    grid_spec = pltpu.PrefetchScalarGridSpec(
        num_scalar_prefetch=1,
        grid=(dec_batch,),
        in_specs=[per_b(rows, D_IDX), per_b(rows, 1), per_b(rows, LANES), per_b(dec_seq, D_IDX),
                  per_b(dec_seq, LANES), per_b(dec_seq, LANES), any_spec, any_spec, any_spec],
        out_specs=per_b(rows, LANES),
        scratch_shapes=[pltpu.VMEM((2, n_keys, D_IDX), F32), pltpu.VMEM((n_keys, LANES), F32),
                        pltpu.VMEM((n_keys, LANES), F32), pltpu.VMEM((SUBLANES, n_keys), I32),
                        pltpu.VMEM((SUBLANES, n_keys), F32),
                        pltpu.VMEM((rows, n_keys), F32), pltpu.SemaphoreType.DMA((4,))],
    )
    return pl.pallas_call(
        functools.partial(_attn_sample_kernel, n_pages=n_pages, page=page, dec_seq=dec_seq,
                          k_sel=k_sel, chunk=min(4096, past)),
        out_shape=jax.ShapeDtypeStruct((dec_batch, rows, LANES), F32),
        grid_spec=grid_spec,
        compiler_params=pltpu.CompilerParams(
            dimension_semantics=("arbitrary",), vmem_limit_bytes=VMEM_LIMIT_BYTES),
        name="attn_sample",
    )(page_table, qi_rows, wcol, qp_rows, ki_new, k_new, v_new, cache_idx, cache_k, cache_v)


def _rope_tables(pos):
    half = ROT_DIM // 2
    inv = jnp.power(jnp.float32(ROPE_THETA), -jnp.arange(half, dtype=F32) * (2.0 / ROT_DIM))
    ang = pos.astype(F32)[:, None] * inv[None, :]
    cos, sin = jnp.cos(ang), jnp.sin(ang)
    n = pos.shape[0]
    rest = HEAD_DIM - ROT_DIM
    zh = jnp.zeros((n, half), F32)
    c = jnp.concatenate([cos, cos, jnp.ones((n, rest), F32)], axis=-1)
    sa = jnp.concatenate([-sin, zh, jnp.zeros((n, rest), F32)], axis=-1)
    sb = jnp.concatenate([zh, sin, jnp.zeros((n, rest), F32)], axis=-1)
    two = lambda a: jnp.concatenate([a, a], axis=-1)
    return two(c), two(sa), two(sb)


def kernel(x_prompt, x_sample, cache_k, cache_v, cache_idx_k, cache_conv, page_table, ffn1_norm,
           ffn1_w_gate, ffn1_w_up, ffn1_w_down, mix_norm, w_in, conv_w, conv_b, q_norm, k_norm, w_out,
           ffn2_norm, ffn2_w_gate, ffn2_w_up, ffn2_w_down):
    assert IDX_ROT == ROT_DIM and D_IDX == HEAD_DIM and 2 * HEAD_DIM == LANES
    batch, seq, d = x_prompt.shape
    dec_batch, dec_seq, _ = x_sample.shape
    depth, n_pool, page = cache_k.shape[:3]
    n_pages = page_table.shape[1]
    past = n_pages * page
    d_conv = conv_w.shape[-1]
    assert seq & (seq - 1) == 0 and dec_seq & (dec_seq - 1) == 0 and seq % Q_BLOCK == 0
    assert dec_seq <= SUBLANES and dec_seq >= CONV_W - 1
    n_s = dec_batch * dec_seq
    tm_p = min(512, seq)

    tabs_p = _rope_tables(jnp.arange(seq, dtype=I32))
    tabs_s = _rope_tables(past + (jnp.arange(n_s, dtype=I32) % dec_seq))
    seg = jnp.kron(jnp.eye(LANES // HEAD_DIM, dtype=F32),
                   jnp.full((HEAD_DIM, HEAD_DIM), 1.0 / HEAD_DIM, F32)).astype(BF16)
    two = lambda a: jnp.concatenate([a, a], axis=-1)[None, :]

    xp = x_prompt.reshape(batch * seq, d)
    xs = x_sample.reshape(n_s, d)
    outs_p, outs_s = [], []
    for l in range(depth):
        wg1, wu1, wd1 = (w[l].astype(BF16) for w in (ffn1_w_gate, ffn1_w_up, ffn1_w_down))
        wg2, wu2, wd2 = (w[l].astype(BF16) for w in (ffn2_w_gate, ffn2_w_up, ffn2_w_down))
        n_in = w_in.shape[-1]
        n_proj = -(-(n_in - N_IDX_HEADS - D_IDX + LANES) // LANES) * LANES
        win = jnp.pad(w_in[l], ((0, 0), (0, n_proj - n_in))).astype(BF16)
        wo = w_out[l].astype(BF16)
        g1, gm, g2 = ffn1_norm[l][None, :], mix_norm[l][None, :], ffn2_norm[l][None, :]
        qn, kn = two(q_norm[l]), two(k_norm[l])
        cw, cb = conv_w[l], conv_b[l][None, :]

        x1 = _ffn_call(xp, g1, wg1, wu1, wd1, tm_p)
        (yc, qb, k, v, kiw, kd, vt, kid, qi, wt, u_tail) = _mix_call(
            x1, gm, win, cw, cb, qn, kn, seg, tabs_p, tm_p, seq)
        attn = _attn_prompt_call(qi, wt, qb, kid, kd, vt, batch, seq, min(K_TOP, seq // 4))
        xp = _ffn_call(x1, g2, wg2, wu2, wd2, tm_p, mix=(yc, attn, wo))
        tiles_per_seq = seq // tm_p
        conv_p = u_tail.reshape(batch, tiles_per_seq, SUBLANES, d_conv)[:, -1, SUBLANES - (CONV_W - 1):]
        outs_p.append((k.reshape(batch, seq, N_KV, HEAD_DIM), v.reshape(batch, seq, N_KV, HEAD_DIM),
                       kiw[:, :D_IDX].reshape(batch, seq, D_IDX), conv_p))

        x1s = _ffn_call(xs, g1, wg1, wu1, wd1, n_s)
        prev = cache_conv[l]
        zrow = jnp.zeros((dec_batch, 1, d_conv), F32)
        pad = jnp.zeros((dec_batch, dec_seq - 2, d_conv), F32)
        e1 = jnp.concatenate([prev[:, 1:2], zrow, pad], axis=1).reshape(n_s, d_conv)
        e2 = jnp.concatenate([prev[:, 0:1], prev[:, 1:2], pad], axis=1).reshape(n_s, d_conv)
        (ycs, qbs, ks, vs, kiws, _, _, _, qis, wts, us) = _mix_call(
            x1s, gm, win, cw, cb, qn, kn, seg, tabs_s, n_s, dec_seq, prev=(e1, e2))

        def head_rows(a, width):
            a = a.reshape(dec_batch, dec_seq, N_HEADS, width).transpose(0, 2, 1, 3)
            a = jnp.pad(a, ((0, 0), (0, 0), (0, SUBLANES - dec_seq), (0, 0)))
            return a.reshape(dec_batch, N_HEADS * SUBLANES, width)

        qi_rows = head_rows(qis, D_IDX)
        q_rows = head_rows(qbs, HEAD_DIM)
        rep = N_HEADS // N_KV
        grp = (jnp.arange(N_HEADS * SUBLANES) // SUBLANES // rep)[None, :, None]
        zq = jnp.zeros_like(q_rows)
        qp_rows = jnp.concatenate([jnp.where(grp == 0, q_rows, zq), jnp.where(grp == 1, q_rows, zq)],
                                  axis=-1)
        wcol = head_rows(wts.T.reshape(n_s, N_IDX_HEADS), 1)
        o_s = _attn_sample_call(
            page_table, qi_rows, wcol, qp_rows, kiws[:, :D_IDX].reshape(dec_batch, dec_seq, D_IDX),
            ks.reshape(dec_batch, dec_seq, LANES), vs.reshape(dec_batch, dec_seq, LANES),
            cache_idx_k[l], cache_k[l].reshape(n_pool, page, LANES),
            cache_v[l].reshape(n_pool, page, LANES), min(K_TOP, (past + dec_seq) // 4))
        o_s = o_s.reshape(dec_batch, N_HEADS, SUBLANES, N_KV, HEAD_DIM)[:, :, :dec_seq]
        hsel = (jnp.arange(N_HEADS) // rep)[None, :, None, None]
        attn_s = jnp.where(hsel == 0, o_s[:, :, :, 0], o_s[:, :, :, 1])
        attn_s = attn_s.transpose(0, 2, 1, 3).reshape(n_s, N_HEADS * HEAD_DIM).astype(BF16)
        xs = _ffn_call(x1s, g2, wg2, wu2, wd2, n_s, mix=(ycs, attn_s, wo))
        conv_s = us.reshape(dec_batch, dec_seq, d_conv)[:, dec_seq - (CONV_W - 1):]
        outs_s.append((ks.reshape(dec_batch, dec_seq, N_KV, HEAD_DIM),
                       vs.reshape(dec_batch, dec_seq, N_KV, HEAD_DIM),
                       kiws[:, :D_IDX].reshape(dec_batch, dec_seq, D_IDX), conv_s))

    stack = lambda outs, n: jnp.stack([o[n] for o in outs])
    return (xp.reshape(batch, seq, d), xs.reshape(dec_batch, dec_seq, d),
            stack(outs_p, 0), stack(outs_p, 1), stack(outs_p, 2), stack(outs_p, 3),
            stack(outs_s, 0), stack(outs_s, 1), stack(outs_s, 2), stack(outs_s, 3))
```

```python
import functools

import jax
import jax.numpy as jnp
from jax import lax
from jax.experimental import pallas as pl
from jax.experimental.pallas import tpu as pltpu

F32 = jnp.float32
BF16 = jnp.bfloat16
I32 = jnp.int32

N_HEADS = 8
HEAD_DIM = 64
N_KV = 2
ROT_DIM = HEAD_DIM // 4
ROPE_THETA = 500000.0
N_IDX_HEADS = 8
D_IDX = 64
IDX_ROT = D_IDX // 4
K_TOP = 256
CONV_W = 3
EPS = 1e-6
IDX_SCALE = D_IDX ** -0.5 * N_IDX_HEADS ** -0.5
ATTN_SCALE = HEAD_DIM ** -0.5

LANES = 128
SUBLANES = 8
Q_BLOCK = 128
SCORE_ROWS = 128
COUNT_ROWS = 64
VMEM_LIMIT_BYTES = 56 * 1024 * 1024

INT_MIN = -2 ** 31
NEG_INF = float("-inf")
_NT = (((1,), (1,)), ((), ()))


def _rmsnorm(x, g):
    y = x * lax.rsqrt(jnp.mean(x * x, axis=-1, keepdims=True) + EPS)
    return y * g


def _swiglu_residual(x, g_ref, wg_ref, wu_ref, wd_ref):
    h = _rmsnorm(x, g_ref[...]).astype(BF16)
    gate = jnp.dot(h, wg_ref[...], preferred_element_type=F32)
    up = jnp.dot(h, wu_ref[...], preferred_element_type=F32)
    act = (gate * jax.nn.sigmoid(gate) * up).astype(BF16)
    return x + 0.5 * jnp.dot(act, wd_ref[...], preferred_element_type=F32)


def _ffn_kernel(x_ref, g_ref, wg_ref, wu_ref, wd_ref, o_ref):
    o_ref[...] = _swiglu_residual(x_ref[...], g_ref, wg_ref, wu_ref, wd_ref)


def _out_ffn_kernel(x_ref, yc_ref, at_ref, wo_ref, g_ref, wg_ref, wu_ref, wd_ref, o_ref):
    mixed = jnp.concatenate([yc_ref[...], at_ref[...]], axis=-1)
    x = x_ref[...] + jnp.dot(mixed, wo_ref[...], preferred_element_type=F32)
    o_ref[...] = _swiglu_residual(x, g_ref, wg_ref, wu_ref, wd_ref)


def _const_spec(shape):
    return pl.BlockSpec(shape, lambda i: (0,) * len(shape))


def _ffn_call(x, g, wg, wu, wd, tm, mix=None):
    n_tok, d = x.shape
    d_ff = wg.shape[1]
    tok = lambda w: pl.BlockSpec((tm, w), lambda i: (i, 0))
    w_specs = [_const_spec((1, d)), _const_spec((d, d_ff)), _const_spec((d, d_ff)), _const_spec((d_ff, d))]
    if mix is None:
        body, args, specs = _ffn_kernel, (x, g, wg, wu, wd), [tok(d)] + w_specs
    else:
        yc, at, wo = mix
        body, args = _out_ffn_kernel, (x, yc, at, wo, g, wg, wu, wd)
        specs = [tok(d), tok(yc.shape[1]), tok(at.shape[1]), _const_spec(wo.shape)] + w_specs
    return pl.pallas_call(
        body,
        out_shape=jax.ShapeDtypeStruct((n_tok, d), F32),
        grid=(n_tok // tm,),
        in_specs=specs,
        out_specs=tok(d),
        compiler_params=pltpu.CompilerParams(
            dimension_semantics=("arbitrary",), vmem_limit_bytes=VMEM_LIMIT_BYTES),
        name="ffn" if mix is None else "out_ffn",
    )(*args)


def _seg_mean(x2, seg):
    hi = x2.astype(BF16)
    lo = (x2 - hi.astype(F32)).astype(BF16)
    return (jnp.dot(hi, seg, preferred_element_type=F32)
            + jnp.dot(lo, seg, preferred_element_type=F32))


def _rope(x, c, sa, sb):
    half = ROT_DIM // 2
    n = x.shape[-1]
    return x * c + pltpu.roll(x, n - half, 1) * sa + pltpu.roll(x, half, 1) * sb


def _dup_halves(x, lane):
    r = pltpu.roll(x, LANES // 2, 1)
    low = lane < LANES // 2
    return jnp.where(low, x, r), jnp.where(low, r, x)


def _mix_kernel(*refs, tm, seq_len, has_prev, d_conv):
    it = iter(refs)
    x_ref, g_ref, win_ref, cw_ref, cb_ref, qn_ref, kn_ref, seg_ref, cos_ref, sa_ref, sb_ref = (
        next(it) for _ in range(11))
    if has_prev:
        e1_ref, e2_ref = next(it), next(it)
    (yc_ref, qb_ref, kd_ref, vtb_ref, kid_ref, qi_ref, wt_ref, u_ref, kt_ref, vt_ref, kit_ref,
     uext_ref) = (next(it) for _ in range(12))

    i = pl.program_id(0)
    h = _rmsnorm(x_ref[...], g_ref[...]).astype(BF16)
    proj = jnp.dot(h, win_ref[...], preferred_element_type=F32)
    o_q = 3 * d_conv
    o_k = o_q + N_HEADS * HEAD_DIM
    o_v = o_k + N_KV * HEAD_DIM
    o_qi = o_v + N_KV * HEAD_DIM
    o_ki = o_qi + N_IDX_HEADS * D_IDX

    gb = proj[:, 0:d_conv]
    u = proj[:, d_conv:2 * d_conv] * proj[:, 2 * d_conv:3 * d_conv]

    @pl.when(i == 0)
    def _():
        uext_ref[0:SUBLANES, :] = jnp.zeros((SUBLANES, d_conv), F32)

    uext_ref[SUBLANES:SUBLANES + tm, :] = u
    u1 = uext_ref[SUBLANES - 1:SUBLANES - 1 + tm, :]
    u2 = uext_ref[SUBLANES - 2:SUBLANES - 2 + tm, :]
    t = (i * tm + lax.broadcasted_iota(I32, (tm, d_conv), 0)) & (seq_len - 1)
    p1 = e1_ref[...] if has_prev else 0.0
    p2 = e2_ref[...] if has_prev else 0.0
    u1 = jnp.where(t >= 1, u1, p1)
    u2 = jnp.where(t >= 2, u2, p2)
    cw = cw_ref[...]
    conv = cb_ref[...] + cw[0:1, :] * u2
    conv = conv + cw[1:2, :] * u1
    conv = conv + cw[2:3, :] * u
    yc_ref[...] = (gb * conv).astype(BF16)
    tail = u[tm - SUBLANES:tm, :]
    uext_ref[0:SUBLANES, :] = tail
    u_ref[...] = u if has_prev else tail

    c, sa, sb = cos_ref[...], sa_ref[...], sb_ref[...]
    seg = seg_ref[...]
    lane = lax.broadcasted_iota(I32, (tm, LANES), 1)

    def qk_norm_rope(xt, gain):
        ms = _seg_mean(xt * xt, seg)
        return _rope(xt * lax.rsqrt(ms + EPS) * gain, c, sa, sb)

    for j in range(N_HEADS * HEAD_DIM // LANES):
        sl = slice(j * LANES, (j + 1) * LANES)
        qt = qk_norm_rope(proj[:, o_q + j * LANES:o_q + (j + 1) * LANES], qn_ref[...])
        qb_ref[:, sl] = (qt * ATTN_SCALE).astype(BF16)
        qit = _rope(proj[:, o_qi + j * LANES:o_qi + (j + 1) * LANES], c, sa, sb)
        qi_ref[:, sl] = qit.astype(BF16)

    kt = qk_norm_rope(proj[:, o_k:o_v], kn_ref[...])
    kt_ref[...] = kt.T
    k0, k1 = _dup_halves(kt, lane)
    kd_ref[:, 0:LANES] = k0.astype(BF16)
    kd_ref[:, LANES:2 * LANES] = k1.astype(BF16)

    v_t = proj[:, o_v:o_qi].T
    vt_ref[...] = v_t
    vtb_ref[...] = v_t.astype(BF16)

    low = lane < D_IDX
    kiw = _rope(proj[:, o_ki:o_ki + LANES], jnp.where(low, c, 1.0), jnp.where(low, sa, 0.0),
                jnp.where(low, sb, 0.0))
    kid_ref[...] = _dup_halves(kiw, lane)[0].astype(BF16)
    kiw_t = kiw.T
    kit_ref[...] = kiw_t[0:D_IDX, :]
    wt_ref[...] = kiw_t[D_IDX:D_IDX + N_IDX_HEADS, :] * IDX_SCALE


def _mix_call(x, g, win, cw, cb, qn, kn, seg, tables, tm, seq_len, prev=None):
    n_tok, d = x.shape
    d_conv = cw.shape[1]
    n_proj = win.shape[1]
    tps = max(seq_len // tm, 1)
    span = tps * tm
    has_prev = prev is not None
    tok = lambda w: pl.BlockSpec((tm, w), lambda i: (i, 0))
    tab = pl.BlockSpec((tm, LANES), lambda i: (i % tps, 0))
    fmaj = lambda r: pl.BlockSpec((None, r, tm), lambda i: (i // tps, 0, i % tps))
    in_specs = [tok(d), _const_spec((1, d)), _const_spec((d, n_proj)), _const_spec(cw.shape),
                _const_spec((1, d_conv)), _const_spec((1, LANES)), _const_spec((1, LANES)),
                _const_spec((LANES, LANES)), tab, tab, tab]
    args = [x, g, win, cw, cb, qn, kn, seg, *tables]
    if has_prev:
        in_specs += [tok(d_conv), tok(d_conv)]
        args += list(prev)
    qw = N_HEADS * HEAD_DIM
    u_rows = tm if has_prev else SUBLANES
    n_tiles = n_tok // tm
    out_shape = [
        jax.ShapeDtypeStruct((n_tok, d_conv), BF16),
        jax.ShapeDtypeStruct((n_tok, qw), BF16),
        jax.ShapeDtypeStruct((n_tok, 2 * LANES), BF16),
        jax.ShapeDtypeStruct((LANES, n_tok), BF16),
        jax.ShapeDtypeStruct((n_tok, LANES), BF16),
        jax.ShapeDtypeStruct((n_tok, N_IDX_HEADS * D_IDX), BF16),
        jax.ShapeDtypeStruct((N_IDX_HEADS, n_tok), F32),
        jax.ShapeDtypeStruct((n_tiles * u_rows, d_conv), F32),
        jax.ShapeDtypeStruct((n_tok // span, LANES, span), F32),
        jax.ShapeDtypeStruct((n_tok // span, LANES, span), F32),
        jax.ShapeDtypeStruct((n_tok // span, D_IDX, span), F32),
    ]
    out_specs = [tok(d_conv), tok(qw), tok(2 * LANES),
                 pl.BlockSpec((LANES, tm), lambda i: (0, i)), tok(LANES), tok(N_IDX_HEADS * D_IDX),
                 pl.BlockSpec((N_IDX_HEADS, tm), lambda i: (0, i)),
                 pl.BlockSpec((u_rows, d_conv), lambda i: (i, 0)),
                 fmaj(LANES), fmaj(LANES), fmaj(D_IDX)]
    return pl.pallas_call(
        functools.partial(_mix_kernel, tm=tm, seq_len=seq_len, has_prev=has_prev, d_conv=d_conv),
        out_shape=out_shape,
        grid=(n_tiles,),
        in_specs=in_specs,
        out_specs=out_specs,
        scratch_shapes=[pltpu.VMEM((tm + SUBLANES, d_conv), F32)],
        compiler_params=pltpu.CompilerParams(
            dimension_semantics=("arbitrary",), vmem_limit_bytes=VMEM_LIMIT_BYTES),
        name="mix_sample" if has_prev else "mix_prompt",
    )(*args)


def _sortable_key(score):
    bits = pltpu.bitcast(jnp.where(score == 0.0, 0.0, score), I32)
    return jnp.where(bits < 0, bits ^ 0x7FFFFFFF, bits)


def _radix_select(count_ge, k_sel, shape):
    def body(i, t):
        cand = t + lax.shift_left(jnp.int32(1), 31 - i)
        return jnp.where(count_ge(cand) >= k_sel, cand, t)
    return lax.fori_loop(0, 32, body, jnp.full(shape, INT_MIN, I32))


def _tie_cut(count_eq_below, need, n_bits, shape):
    def body(i, j):
        cand = j + lax.shift_left(jnp.int32(1), n_bits - 1 - i)
        return jnp.where(count_eq_below(cand) < need, cand, j)
    return lax.fori_loop(0, n_bits, body, jnp.zeros(shape, I32))


def _split_heads_rhs(q_tile, lane):
    zero = jnp.zeros_like(q_tile)
    low = lane < LANES // 2
    return jnp.concatenate([jnp.where(low, q_tile, zero), jnp.where(low, zero, q_tile)], axis=0)


def _count_rows(key_ref, s, pred):
    acc = jnp.zeros((COUNT_ROWS, LANES), I32)
    for c in range(s // COUNT_ROWS):
        r0 = c * COUNT_ROWS
        acc = acc + jnp.where(pred(key_ref[r0:r0 + COUNT_ROWS, :], r0), 1, 0)
    return acc.sum(axis=0, keepdims=True)


def _attn_prompt_kernel(qi_ref, wt_ref, qb_ref, kid_ref, kd_ref, vt_ref, o_ref, key_ref, bias_ref,
                        ot_ref, rhsi_ref, rhsq_ref, *, s_variants, k_sel, n_bits):
    j = pl.program_id(1)
    t0 = j * Q_BLOCK
    gran = s_variants[0]
    lane_q = lax.broadcasted_iota(I32, (Q_BLOCK, LANES), 1)
    n_pairs = N_HEADS // 2
    for p in range(n_pairs):
        rhsi_ref[p] = _split_heads_rhs(qi_ref[:, p * LANES:(p + 1) * LANES], lane_q)
        rhsq_ref[p] = _split_heads_rhs(qb_ref[:, p * LANES:(p + 1) * LANES], lane_q)

    def process(s):
        w = wt_ref[...]
        for c in range(s // SCORE_ROWS):
            r0 = c * SCORE_ROWS
            kid_c = kid_ref[r0:r0 + SCORE_ROWS, :]
            score = None
            for p in range(n_pairs):
                d = lax.dot_general(kid_c, rhsi_ref[p], _NT, preferred_element_type=F32)
                term = (jnp.maximum(d[:, 0:LANES], 0.0) * w[2 * p:2 * p + 1, :]
                        + jnp.maximum(d[:, LANES:], 0.0) * w[2 * p + 1:2 * p + 2, :])
                score = term if score is None else score + term
            key = _sortable_key(score)
            if r0 + SCORE_ROWS > s - gran:
                row = r0 + lax.broadcasted_iota(I32, (SCORE_ROWS, LANES), 0)
                tq = t0 + lax.broadcasted_iota(I32, (SCORE_ROWS, LANES), 1)
                key = jnp.where(row <= tq, key, INT_MIN)
            key_ref[r0:r0 + SCORE_ROWS, :] = key

        def count_ge(cand):
            return _count_rows(key_ref, s, lambda k, r0: k >= cand)

        thr = jnp.maximum(_radix_select(count_ge, k_sel, (1, LANES)), INT_MIN + 1)
        bias_ref[0:s, :] = jnp.where(key_ref[0:s, :] >= thr, 0.0, NEG_INF)
        n_ge = count_ge(thr)

        @pl.when(jnp.max(n_ge) > k_sel)
        def _():
            need = k_sel - _count_rows(key_ref, s, lambda k, r0: k > thr)

            def count_eq_below(cand):
                def pred(k, r0):
                    row = r0 + lax.broadcasted_iota(I32, (COUNT_ROWS, LANES), 0)
                    return jnp.where(k == thr, row, cand) < cand
                return _count_rows(key_ref, s, pred)

            cut = _tie_cut(count_eq_below, need, n_bits, (1, LANES))
            key = key_ref[0:s, :]
            row = lax.broadcasted_iota(I32, (s, LANES), 0)
            bias_ref[0:s, :] = jnp.where(key == thr, jnp.where(row <= cut, 0.0, NEG_INF),
                                         jnp.where(key > thr, 0.0, NEG_INF))

        bias = bias_ref[0:s, :]
        bias2 = jnp.concatenate([bias, bias], axis=1)
        pairs_per_kv = (N_HEADS // N_KV) // 2
        for p in range(n_pairs):
            g = p // pairs_per_kv
            logit = lax.dot_general(kd_ref[0:s, g * LANES:(g + 1) * LANES], rhsq_ref[p], _NT,
                                    preferred_element_type=F32) + bias2
            m = jnp.max(logit, axis=0, keepdims=True)
            e = jnp.exp(logit - m)
            den = jnp.sum(e, axis=0, keepdims=True)
            pv = jnp.dot(vt_ref[g * HEAD_DIM:(g + 1) * HEAD_DIM, 0:s], e.astype(BF16),
                         preferred_element_type=F32)
            pv = pv / den
            ot_ref[(2 * p) * HEAD_DIM:(2 * p + 1) * HEAD_DIM, :] = pv[:, 0:LANES]
            ot_ref[(2 * p + 1) * HEAD_DIM:(2 * p + 2) * HEAD_DIM, :] = pv[:, LANES:]
        o_ref[...] = ot_ref[...].T.astype(BF16)

    blocks_per_variant = gran // Q_BLOCK
    for vi, s in enumerate(s_variants):
        if len(s_variants) == 1:
            process(s)
        else:
            pl.when(j // blocks_per_variant == vi)(functools.partial(process, s))


def _attn_prompt_call(qi, wt, qb, kid, kd, vt, batch, seq, k_sel):
    n_tok = batch * seq
    nq = seq // Q_BLOCK
    n_var = max(v for v in (1, 2, 4, 8) if nq % v == 0)
    s_variants = tuple(seq // n_var * (v + 1) for v in range(n_var))
    qw = qb.shape[1]
    qblk = lambda w: pl.BlockSpec((Q_BLOCK, w), lambda b, j: (b * nq + j, 0))
    per_b = lambda w: pl.BlockSpec((seq, w), lambda b, j: (b, 0))
    return pl.pallas_call(
        functools.partial(_attn_prompt_kernel, s_variants=s_variants, k_sel=k_sel,
                          n_bits=(seq - 1).bit_length()),
        out_shape=jax.ShapeDtypeStruct((n_tok, qw), BF16),
        grid=(batch, nq),
        in_specs=[qblk(qi.shape[1]),
                  pl.BlockSpec((N_IDX_HEADS, Q_BLOCK), lambda b, j: (0, b * nq + j)),
                  qblk(qw), per_b(LANES), per_b(2 * LANES),
                  pl.BlockSpec((LANES, seq), lambda b, j: (0, b))],
        out_specs=qblk(qw),
        scratch_shapes=[pltpu.VMEM((seq, LANES), I32), pltpu.VMEM((seq, LANES), F32),
                        pltpu.VMEM((qw, Q_BLOCK), F32),
                        pltpu.VMEM((N_HEADS // 2, 2 * Q_BLOCK, LANES), BF16),
                        pltpu.VMEM((N_HEADS // 2, 2 * Q_BLOCK, LANES), BF16)],
        compiler_params=pltpu.CompilerParams(
            dimension_semantics=("arbitrary", "arbitrary"), vmem_limit_bytes=VMEM_LIMIT_BYTES),
        name="attn_prompt",
    )(qi, wt, qb, kid, kd, vt)


def _attn_sample_kernel(pt_ref, qi_ref, wc_ref, qp_ref, kin_ref, kn_ref, vn_ref, cidx_ref, ck_ref,
                        cv_ref, o_ref, idx_buf, k_buf, v_buf, key_ref, bias_ref, lg_ref, sem, *, n_pages,
                        page, dec_seq, k_sel, chunk):
    b = pl.program_id(0)
    nb = pl.num_programs(0)
    slot = b % 2
    past = n_pages * page
    n_keys = past + LANES
    rows = N_IDX_HEADS * SUBLANES

    def page_cols(pg):
        return pl.ds(pl.multiple_of(pg * page, page), page)

    def idx_copy(bb, sl, pg):
        return pltpu.make_async_copy(cidx_ref.at[pt_ref[bb, pg]], idx_buf.at[sl, :, page_cols(pg)],
                                     sem.at[sl])

    def kv_copy(src, dst, s_id, pg):
        return pltpu.make_async_copy(src.at[pt_ref[b, pg]], dst.at[:, page_cols(pg)], sem.at[s_id])

    def for_pages(fn):
        def body(pg, carry):
            fn(pg)
            return carry
        lax.fori_loop(0, n_pages, body, 0)

    @pl.when(b == 0)
    def _():
        for_pages(lambda pg: idx_copy(0, 0, pg).start())

    for_pages(lambda pg: kv_copy(ck_ref, k_buf, 2, pg).start())
    for_pages(lambda pg: kv_copy(cv_ref, v_buf, 3, pg).start())

    @pl.when(b + 1 < nb)
    def _():
        for_pages(lambda pg: idx_copy(b + 1, 1 - slot, pg).start())

    idx_buf[slot, :, past:n_keys] = kin_ref[...]
    k_buf[:, past:n_keys] = kn_ref[...]
    v_buf[:, past:n_keys] = vn_ref[...]

    for_pages(lambda pg: idx_copy(b, slot, pg).wait())

    chunks = [(c0, min(chunk, past - c0)) for c0 in range(0, past, chunk)] + [(past, LANES)]

    qi = qi_ref[...]
    wc = wc_ref[...]
    for c0, n in chunks:
        d = jnp.dot(qi, idx_buf[slot, :, c0:c0 + n].astype(BF16), preferred_element_type=F32)
        sh = jnp.maximum(d, 0.0) * wc
        score = sh[0:SUBLANES, :]
        for hh in range(1, N_IDX_HEADS):
            score = score + sh[hh * SUBLANES:(hh + 1) * SUBLANES, :]
        tslot = lax.broadcasted_iota(I32, (SUBLANES, n), 0)
        col = c0 + lax.broadcasted_iota(I32, (SUBLANES, n), 1)
        last = jnp.where(tslot < dec_seq, past + tslot, 0)
        key_ref[:, c0:c0 + n] = jnp.where(col <= last, _sortable_key(score), INT_MIN)

    def count_ge(cand):
        return jnp.sum((key_ref[...] >= cand).astype(I32), axis=1, keepdims=True)

    thr = jnp.maximum(_radix_select(count_ge, k_sel, (SUBLANES, 1)), INT_MIN + 1)
    ge = key_ref[...] >= thr
    n_ge = jnp.sum(ge.astype(I32), axis=1, keepdims=True)
    bias_ref[...] = jnp.where(ge, 0.0, NEG_INF)

    @pl.when(jnp.max(n_ge) > k_sel)
    def _():
        need = k_sel - jnp.sum((key_ref[...] > thr).astype(I32), axis=1, keepdims=True)

        def count_eq_below(cand):
            col = lax.broadcasted_iota(I32, (SUBLANES, n_keys), 1)
            return jnp.sum(jnp.where(key_ref[...] == thr, (col < cand).astype(I32), 0), axis=1,
                           keepdims=True)

        cut = _tie_cut(count_eq_below, need, (n_keys - 1).bit_length(), (SUBLANES, 1))
        key = key_ref[...]
        col = lax.broadcasted_iota(I32, (SUBLANES, n_keys), 1)
        bias_ref[...] = jnp.where(key == thr, jnp.where(col <= cut, 0.0, NEG_INF),
                                  jnp.where(key > thr, 0.0, NEG_INF))

    for_pages(lambda pg: kv_copy(ck_ref, k_buf, 2, pg).wait())
    for_pages(lambda pg: kv_copy(cv_ref, v_buf, 3, pg).wait())

    qp = qp_ref[...]
    for c0, n in chunks:
        bias = jnp.concatenate([bias_ref[:, c0:c0 + n]] * N_HEADS, axis=0)
        lg_ref[:, c0:c0 + n] = jnp.dot(qp, k_buf[:, c0:c0 + n].astype(BF16),
                                       preferred_element_type=F32) + bias
    m = jnp.max(lg_ref[...], axis=1, keepdims=True)
    den = jnp.zeros((rows, 1), F32)
    acc = jnp.zeros((rows, LANES), F32)
    for c0, n in chunks:
        e = jnp.exp(lg_ref[:, c0:c0 + n] - m)
        den = den + jnp.sum(e, axis=1, keepdims=True)
        acc = acc + lax.dot_general(e.astype(BF16), v_buf[:, c0:c0 + n].astype(BF16), _NT,
                                    preferred_element_type=F32)
    o_ref[...] = acc / den


def _attn_sample_call(page_table, qi_rows, wcol, qp_rows, ki_new, k_new, v_new, cache_idx, cache_k,
                      cache_v, dec_seq, k_sel):
    dec_batch, n_pages = page_table.shape
    page = cache_idx.shape[2]
    past = n_pages * page
    n_keys = past + LANES
    rows = N_IDX_HEADS * SUBLANES
    per_b = lambda r, w: pl.BlockSpec((None, r, w), lambda b, pt: (b, 0, 0))
    any_spec = pl.BlockSpec(memory_space=pl.ANY)
    grid_spec = pltpu.PrefetchScalarGridSpec(
        num_scalar_prefetch=1,
        grid=(dec_batch,),
        in_specs=[per_b(rows, D_IDX), per_b(rows, 1), per_b(rows, LANES), per_b(D_IDX, LANES),
                  per_b(LANES, LANES), per_b(LANES, LANES), any_spec, any_spec, any_spec],
        out_specs=per_b(rows, LANES),
        scratch_shapes=[pltpu.VMEM((2, D_IDX, n_keys), F32), pltpu.VMEM((LANES, n_keys), F32),
                        pltpu.VMEM((LANES, n_keys), F32), pltpu.VMEM((SUBLANES, n_keys), I32),
                        pltpu.VMEM((SUBLANES, n_keys), F32), pltpu.VMEM((rows, n_keys), F32),
                        pltpu.SemaphoreType.DMA((4,))],
    )
    return pl.pallas_call(
        functools.partial(_attn_sample_kernel, n_pages=n_pages, page=page, dec_seq=dec_seq,
                          k_sel=k_sel, chunk=min(4096, past)),
        out_shape=jax.ShapeDtypeStruct((dec_batch, rows, LANES), F32),
        grid_spec=grid_spec,
        compiler_params=pltpu.CompilerParams(
            dimension_semantics=("arbitrary",), vmem_limit_bytes=VMEM_LIMIT_BYTES),
        name="attn_sample",
    )(page_table, qi_rows, wcol, qp_rows, ki_new, k_new, v_new, cache_idx, cache_k, cache_v)


def _rope_tables(pos):
    half = ROT_DIM // 2
    inv = jnp.power(jnp.float32(ROPE_THETA), -jnp.arange(half, dtype=F32) * (2.0 / ROT_DIM))
    ang = pos.astype(F32)[:, None] * inv[None, :]
    cos, sin = jnp.cos(ang), jnp.sin(ang)
    n = pos.shape[0]
    rest = HEAD_DIM - ROT_DIM
    zh = jnp.zeros((n, half), F32)
    c = jnp.concatenate([cos, cos, jnp.ones((n, rest), F32)], axis=-1)
    sa = jnp.concatenate([-sin, zh, jnp.zeros((n, rest), F32)], axis=-1)
    sb = jnp.concatenate([zh, sin, jnp.zeros((n, rest), F32)], axis=-1)
    two = lambda a: jnp.concatenate([a, a], axis=-1)
    return two(c), two(sa), two(sb)


def kernel(x_prompt, x_sample, cache_k, cache_v, cache_idx_k, cache_conv, page_table, ffn1_norm,
           ffn1_w_gate, ffn1_w_up, ffn1_w_down, mix_norm, w_in, conv_w, conv_b, q_norm, k_norm, w_out,
           ffn2_norm, ffn2_w_gate, ffn2_w_up, ffn2_w_down):
    assert IDX_ROT == ROT_DIM and D_IDX == HEAD_DIM and 2 * HEAD_DIM == LANES
    batch, seq, d = x_prompt.shape
    dec_batch, dec_seq, _ = x_sample.shape
    depth, n_pool, page = cache_k.shape[:3]
    n_pages = page_table.shape[1]
    past = n_pages * page
    d_conv = conv_w.shape[-1]
    assert seq & (seq - 1) == 0 and dec_seq & (dec_seq - 1) == 0 and seq % Q_BLOCK == 0
    assert dec_seq <= SUBLANES and dec_seq >= CONV_W - 1 and page == LANES
    n_s = dec_batch * dec_seq
    tm_p = min(512, seq)

    tabs_p = _rope_tables(jnp.arange(seq, dtype=I32))
    tabs_s = _rope_tables(past + (jnp.arange(n_s, dtype=I32) % dec_seq))
    seg = jnp.kron(jnp.eye(LANES // HEAD_DIM, dtype=F32),
                   jnp.full((HEAD_DIM, HEAD_DIM), 1.0 / HEAD_DIM, F32)).astype(BF16)
    two = lambda a: jnp.concatenate([a, a], axis=-1)[None, :]

    xp = x_prompt.reshape(batch * seq, d)
    xs = x_sample.reshape(n_s, d)
    outs_p, outs_s = [], []
    for l in range(depth):
        wg1, wu1, wd1 = (w[l].astype(BF16) for w in (ffn1_w_gate, ffn1_w_up, ffn1_w_down))
        wg2, wu2, wd2 = (w[l].astype(BF16) for w in (ffn2_w_gate, ffn2_w_up, ffn2_w_down))
        n_in = w_in.shape[-1]
        n_proj = -(-(n_in - N_IDX_HEADS - D_IDX + LANES) // LANES) * LANES
        win = jnp.pad(w_in[l], ((0, 0), (0, n_proj - n_in))).astype(BF16)
        wo = w_out[l].astype(BF16)
        g1, gm, g2 = ffn1_norm[l][None, :], mix_norm[l][None, :], ffn2_norm[l][None, :]
        qn, kn = two(q_norm[l]), two(k_norm[l])
        cw, cb = conv_w[l], conv_b[l][None, :]

        x1 = _ffn_call(xp, g1, wg1, wu1, wd1, tm_p)
        (yc, qb, kd, vtb, kid, qi, wt, u_tail, k_t, v_t, ki_t) = _mix_call(
            x1, gm, win, cw, cb, qn, kn, seg, tabs_p, tm_p, seq)
        attn = _attn_prompt_call(qi, wt, qb, kid, kd, vtb, batch, seq, min(K_TOP, seq // 4))
        xp = _ffn_call(x1, g2, wg2, wu2, wd2, tm_p, mix=(yc, attn, wo))
        tiles_per_seq = seq // tm_p
        conv_p = u_tail.reshape(batch, tiles_per_seq, SUBLANES, d_conv)[:, -1, SUBLANES - (CONV_W - 1):]
        to_cache = lambda a: a.reshape(batch, N_KV, HEAD_DIM, seq).transpose(0, 3, 1, 2)
        outs_p.append((to_cache(k_t), to_cache(v_t), ki_t.transpose(0, 2, 1), conv_p))

        x1s = _ffn_call(xs, g1, wg1, wu1, wd1, n_s)
        prev = cache_conv[l]
        zrow = jnp.zeros((dec_batch, 1, d_conv), F32)
        pad = jnp.zeros((dec_batch, dec_seq - 2, d_conv), F32)
        e1 = jnp.concatenate([prev[:, 1:2], zrow, pad], axis=1).reshape(n_s, d_conv)
        e2 = jnp.concatenate([prev[:, 0:1], prev[:, 1:2], pad], axis=1).reshape(n_s, d_conv)
        (ycs, qbs, _, _, _, qis, wts, us, k_ts, v_ts, ki_ts) = _mix_call(
            x1s, gm, win, cw, cb, qn, kn, seg, tabs_s, n_s, dec_seq, prev=(e1, e2))

        def head_rows(a, width):
            a = a.reshape(dec_batch, dec_seq, N_HEADS, width).transpose(0, 2, 1, 3)
            a = jnp.pad(a, ((0, 0), (0, 0), (0, SUBLANES - dec_seq), (0, 0)))
            return a.reshape(dec_batch, N_HEADS * SUBLANES, width)

        def new_tile(a_t):
            r = a_t.shape[1]
            a = a_t[0].reshape(r, dec_batch, dec_seq).transpose(1, 0, 2)
            return jnp.pad(a, ((0, 0), (0, 0), (0, LANES - dec_seq)))

        qi_rows = head_rows(qis, D_IDX)
        q_rows = head_rows(qbs, HEAD_DIM)
        rep = N_HEADS // N_KV
        grp = (jnp.arange(N_HEADS * SUBLANES) // SUBLANES // rep)[None, :, None]
        zq = jnp.zeros_like(q_rows)
        qp_rows = jnp.concatenate([jnp.where(grp == 0, q_rows, zq), jnp.where(grp == 1, q_rows, zq)],
                                  axis=-1)
        wcol = head_rows(wts.T.reshape(n_s, N_IDX_HEADS), 1)
        ck = cache_k[l].transpose(0, 2, 3, 1).reshape(n_pool, LANES, page)
        cv = cache_v[l].transpose(0, 2, 3, 1).reshape(n_pool, LANES, page)
        ci = cache_idx_k[l].transpose(0, 2, 1)
        o_s = _attn_sample_call(page_table, qi_rows, wcol, qp_rows, new_tile(ki_ts), new_tile(k_ts),
                                new_tile(v_ts), ci, ck, cv, dec_seq, min(K_TOP, (past + dec_seq) // 4))
        o_s = o_s.reshape(dec_batch, N_HEADS, SUBLANES, N_KV, HEAD_DIM)[:, :, :dec_seq]
        hsel = (jnp.arange(N_HEADS) // rep)[None, :, None, None]
        attn_s = jnp.where(hsel == 0, o_s[:, :, :, 0], o_s[:, :, :, 1])
        attn_s = attn_s.transpose(0, 2, 1, 3).reshape(n_s, N_HEADS * HEAD_DIM).astype(BF16)
        xs = _ffn_call(x1s, g2, wg2, wu2, wd2, n_s, mix=(ycs, attn_s, wo))
        conv_s = us.reshape(dec_batch, dec_seq, d_conv)[:, dec_seq - (CONV_W - 1):]
        rows_s = lambda a_t: a_t[0].T.reshape(dec_batch, dec_seq, -1)
        outs_s.append((rows_s(k_ts).reshape(dec_batch, dec_seq, N_KV, HEAD_DIM),
                       rows_s(v_ts).reshape(dec_batch, dec_seq, N_KV, HEAD_DIM),
                       rows_s(ki_ts), conv_s))

    stack = lambda outs, n: jnp.stack([o[n] for o in outs])
    return (xp.reshape(batch, seq, d), xs.reshape(dec_batch, dec_seq, d),
            stack(outs_p, 0), stack(outs_p, 1), stack(outs_p, 2), stack(outs_p, 3),
            stack(outs_s, 0), stack(outs_s, 1), stack(outs_s, 2), stack(outs_s, 3))
```

```python
import functools

import jax
import jax.numpy as jnp
from jax import lax
from jax.experimental import pallas as pl
from jax.experimental.pallas import tpu as pltpu

F32 = jnp.float32
BF16 = jnp.bfloat16
I32 = jnp.int32
I16 = jnp.int16

N_HEADS = 8
HEAD_DIM = 64
N_KV = 2
ROT_DIM = HEAD_DIM // 4
ROPE_THETA = 500000.0
N_IDX_HEADS = 8
D_IDX = 64
IDX_ROT = D_IDX // 4
K_TOP = 256
CONV_W = 3
EPS = 1e-6
IDX_SCALE = D_IDX ** -0.5 * N_IDX_HEADS ** -0.5
ATTN_SCALE = HEAD_DIM ** -0.5

LANES = 128
SUBLANES = 8
Q_BLOCK = 128
SCORE_ROWS = 128
COUNT_ROWS = 64
COUNT16_ROWS = 128
ATTN_ROWS = 256
VMEM_LIMIT_BYTES = 56 * 1024 * 1024

INT_MIN = -2 ** 31
NEG_INF = float("-inf")
_NT = (((1,), (1,)), ((), ()))


def _rmsnorm(x, g):
    y = x * lax.rsqrt(jnp.mean(x * x, axis=-1, keepdims=True) + EPS)
    return y * g


def _swiglu_residual(x, g_ref, wg_ref, wu_ref, wd_ref):
    h = _rmsnorm(x, g_ref[...]).astype(BF16)
    gate = jnp.dot(h, wg_ref[...], preferred_element_type=F32)
    up = jnp.dot(h, wu_ref[...], preferred_element_type=F32)
    act = (gate * jax.nn.sigmoid(gate) * up).astype(BF16)
    return x + 0.5 * jnp.dot(act, wd_ref[...], preferred_element_type=F32)


def _ffn_kernel(x_ref, g_ref, wg_ref, wu_ref, wd_ref, o_ref):
    o_ref[...] = _swiglu_residual(x_ref[...], g_ref, wg_ref, wu_ref, wd_ref)


def _out_ffn_kernel(x_ref, yc_ref, at_ref, wo_ref, g_ref, wg_ref, wu_ref, wd_ref, o_ref):
    mixed = jnp.concatenate([yc_ref[...], at_ref[...]], axis=-1)
    x = x_ref[...] + jnp.dot(mixed, wo_ref[...], preferred_element_type=F32)
    o_ref[...] = _swiglu_residual(x, g_ref, wg_ref, wu_ref, wd_ref)


def _const_spec(shape):
    return pl.BlockSpec(shape, lambda i: (0,) * len(shape))


def _ffn_call(x, g, wg, wu, wd, tm, mix=None):
    n_tok, d = x.shape
    d_ff = wg.shape[1]
    tok = lambda w: pl.BlockSpec((tm, w), lambda i: (i, 0))
    w_specs = [_const_spec((1, d)), _const_spec((d, d_ff)), _const_spec((d, d_ff)), _const_spec((d_ff, d))]
    if mix is None:
        body, args, specs = _ffn_kernel, (x, g, wg, wu, wd), [tok(d)] + w_specs
    else:
        yc, at, wo = mix
        body, args = _out_ffn_kernel, (x, yc, at, wo, g, wg, wu, wd)
        specs = [tok(d), tok(yc.shape[1]), tok(at.shape[1]), _const_spec(wo.shape)] + w_specs
    return pl.pallas_call(
        body,
        out_shape=jax.ShapeDtypeStruct((n_tok, d), F32),
        grid=(n_tok // tm,),
        in_specs=specs,
        out_specs=tok(d),
        compiler_params=pltpu.CompilerParams(
            dimension_semantics=("arbitrary",), vmem_limit_bytes=VMEM_LIMIT_BYTES),
        name="ffn" if mix is None else "out_ffn",
    )(*args)


def _seg_mean(x2, seg):
    hi = x2.astype(BF16)
    lo = (x2 - hi.astype(F32)).astype(BF16)
    return (jnp.dot(hi, seg, preferred_element_type=F32)
            + jnp.dot(lo, seg, preferred_element_type=F32))


def _rope(x, c, sa, sb):
    half = ROT_DIM // 2
    n = x.shape[-1]
    return x * c + pltpu.roll(x, n - half, 1) * sa + pltpu.roll(x, half, 1) * sb


def _dup_halves(x, lane):
    r = pltpu.roll(x, LANES // 2, 1)
    low = lane < LANES // 2
    return jnp.where(low, x, r), jnp.where(low, r, x)


def _mix_kernel(*refs, tm, seq_len, has_prev, d_conv):
    it = iter(refs)
    x_ref, g_ref, win_ref, cw_ref, cb_ref, qn_ref, kn_ref, seg_ref, cos_ref, sa_ref, sb_ref = (
        next(it) for _ in range(11))
    if has_prev:
        e1_ref, e2_ref = next(it), next(it)
    (yc_ref, qb_ref, kd_ref, vtb_ref, kid_ref, qi_ref, wt_ref, u_ref, kt_ref, vt_ref, kit_ref,
     uext_ref) = (next(it) for _ in range(12))

    i = pl.program_id(0)
    h = _rmsnorm(x_ref[...], g_ref[...]).astype(BF16)
    proj = jnp.dot(h, win_ref[...], preferred_element_type=F32)
    o_q = 3 * d_conv
    o_k = o_q + N_HEADS * HEAD_DIM
    o_v = o_k + N_KV * HEAD_DIM
    o_qi = o_v + N_KV * HEAD_DIM
    o_ki = o_qi + N_IDX_HEADS * D_IDX

    gb = proj[:, 0:d_conv]
    u = proj[:, d_conv:2 * d_conv] * proj[:, 2 * d_conv:3 * d_conv]

    @pl.when(i == 0)
    def _():
        uext_ref[0:SUBLANES, :] = jnp.zeros((SUBLANES, d_conv), F32)

    uext_ref[SUBLANES:SUBLANES + tm, :] = u
    u1 = uext_ref[SUBLANES - 1:SUBLANES - 1 + tm, :]
    u2 = uext_ref[SUBLANES - 2:SUBLANES - 2 + tm, :]
    t = (i * tm + lax.broadcasted_iota(I32, (tm, d_conv), 0)) & (seq_len - 1)
    p1 = e1_ref[...] if has_prev else 0.0
    p2 = e2_ref[...] if has_prev else 0.0
    u1 = jnp.where(t >= 1, u1, p1)
    u2 = jnp.where(t >= 2, u2, p2)
    cw = cw_ref[...]
    conv = cb_ref[...] + cw[0:1, :] * u2
    conv = conv + cw[1:2, :] * u1
    conv = conv + cw[2:3, :] * u
    yc_ref[...] = (gb * conv).astype(BF16)
    tail = u[tm - SUBLANES:tm, :]
    uext_ref[0:SUBLANES, :] = tail
    u_ref[...] = u if has_prev else tail

    c, sa, sb = cos_ref[...], sa_ref[...], sb_ref[...]
    seg = seg_ref[...]
    lane = lax.broadcasted_iota(I32, (tm, LANES), 1)

    def qk_norm_rope(xt, gain):
        ms = _seg_mean(xt * xt, seg)
        return _rope(xt * lax.rsqrt(ms + EPS) * gain, c, sa, sb)

    for j in range(N_HEADS * HEAD_DIM // LANES):
        sl = slice(j * LANES, (j + 1) * LANES)
        qt = qk_norm_rope(proj[:, o_q + j * LANES:o_q + (j + 1) * LANES], qn_ref[...])
        qb_ref[:, sl] = (qt * ATTN_SCALE).astype(BF16)
        qit = _rope(proj[:, o_qi + j * LANES:o_qi + (j + 1) * LANES], c, sa, sb)
        qi_ref[:, sl] = qit.astype(BF16)

    kt = qk_norm_rope(proj[:, o_k:o_v], kn_ref[...])
    kt_ref[...] = kt.T
    k0, k1 = _dup_halves(kt, lane)
    kd_ref[:, 0:LANES] = k0.astype(BF16)
    kd_ref[:, LANES:2 * LANES] = k1.astype(BF16)

    v_t = proj[:, o_v:o_qi].T
    vt_ref[...] = v_t
    vtb_ref[...] = v_t.astype(BF16)

    low = lane < D_IDX
    kiw = _rope(proj[:, o_ki:o_ki + LANES], jnp.where(low, c, 1.0), jnp.where(low, sa, 0.0),
                jnp.where(low, sb, 0.0))
    kid_ref[...] = _dup_halves(kiw, lane)[0].astype(BF16)
    kiw_t = kiw.T
    kit_ref[...] = kiw_t[0:D_IDX, :]
    wt_ref[...] = kiw_t[D_IDX:D_IDX + N_IDX_HEADS, :] * IDX_SCALE


def _mix_call(x, g, win, cw, cb, qn, kn, seg, tables, tm, seq_len, prev=None):
    n_tok, d = x.shape
    d_conv = cw.shape[1]
    n_proj = win.shape[1]
    tps = max(seq_len // tm, 1)
    span = tps * tm
    has_prev = prev is not None
    tok = lambda w: pl.BlockSpec((tm, w), lambda i: (i, 0))
    tab = pl.BlockSpec((tm, LANES), lambda i: (i % tps, 0))
    fmaj = lambda r: pl.BlockSpec((None, r, tm), lambda i: (i // tps, 0, i % tps))
    in_specs = [tok(d), _const_spec((1, d)), _const_spec((d, n_proj)), _const_spec(cw.shape),
                _const_spec((1, d_conv)), _const_spec((1, LANES)), _const_spec((1, LANES)),
                _const_spec((LANES, LANES)), tab, tab, tab]
    args = [x, g, win, cw, cb, qn, kn, seg, *tables]
    if has_prev:
        in_specs += [tok(d_conv), tok(d_conv)]
        args += list(prev)
    qw = N_HEADS * HEAD_DIM
    u_rows = tm if has_prev else SUBLANES
    n_tiles = n_tok // tm
    out_shape = [
        jax.ShapeDtypeStruct((n_tok, d_conv), BF16),
        jax.ShapeDtypeStruct((n_tok, qw), BF16),
        jax.ShapeDtypeStruct((n_tok, 2 * LANES), BF16),
        jax.ShapeDtypeStruct((LANES, n_tok), BF16),
        jax.ShapeDtypeStruct((n_tok, LANES), BF16),
        jax.ShapeDtypeStruct((n_tok, N_IDX_HEADS * D_IDX), BF16),
        jax.ShapeDtypeStruct((N_IDX_HEADS, n_tok), F32),
        jax.ShapeDtypeStruct((n_tiles * u_rows, d_conv), F32),
        jax.ShapeDtypeStruct((n_tok // span, LANES, span), F32),
        jax.ShapeDtypeStruct((n_tok // span, LANES, span), F32),
        jax.ShapeDtypeStruct((n_tok // span, D_IDX, span), F32),
    ]
    out_specs = [tok(d_conv), tok(qw), tok(2 * LANES),
                 pl.BlockSpec((LANES, tm), lambda i: (0, i)), tok(LANES), tok(N_IDX_HEADS * D_IDX),
                 pl.BlockSpec((N_IDX_HEADS, tm), lambda i: (0, i)),
                 pl.BlockSpec((u_rows, d_conv), lambda i: (i, 0)),
                 fmaj(LANES), fmaj(LANES), fmaj(D_IDX)]
    return pl.pallas_call(
        functools.partial(_mix_kernel, tm=tm, seq_len=seq_len, has_prev=has_prev, d_conv=d_conv),
        out_shape=out_shape,
        grid=(n_tiles,),
        in_specs=in_specs,
        out_specs=out_specs,
        scratch_shapes=[pltpu.VMEM((tm + SUBLANES, d_conv), F32)],
        compiler_params=pltpu.CompilerParams(
            dimension_semantics=("arbitrary",), vmem_limit_bytes=VMEM_LIMIT_BYTES),
        name="mix_sample" if has_prev else "mix_prompt",
    )(*args)


def _sortable_key(score):
    bits = pltpu.bitcast(jnp.where(score == 0.0, 0.0, score), I32)
    return jnp.where(bits < 0, bits ^ 0x7FFFFFFF, bits)


def _radix_select(count_ge, k_sel, shape):
    def body(i, t):
        cand = t + lax.shift_left(jnp.int32(1), 31 - i)
        return jnp.where(count_ge(cand) >= k_sel, cand, t)
    return lax.fori_loop(0, 32, body, jnp.full(shape, INT_MIN, I32))


def _tie_cut(count_eq_below, need, n_bits, shape):
    def body(i, j):
        cand = j + lax.shift_left(jnp.int32(1), n_bits - 1 - i)
        return jnp.where(count_eq_below(cand) < need, cand, j)
    return lax.fori_loop(0, n_bits, body, jnp.zeros(shape, I32))


def _split_heads_rhs(q_tile, lane):
    zero = jnp.zeros_like(q_tile)
    low = lane < LANES // 2
    return jnp.concatenate([jnp.where(low, q_tile, zero), jnp.where(low, zero, q_tile)], axis=0)


def _count_rows(key_ref, s, pred):
    acc = jnp.zeros((COUNT_ROWS, LANES), I32)
    for c in range(s // COUNT_ROWS):
        r0 = c * COUNT_ROWS
        acc = acc + jnp.where(pred(key_ref[r0:r0 + COUNT_ROWS, :], r0), 1, 0)
    return acc.sum(axis=0, keepdims=True)


def _count16(ref, s, cand, strict=False):
    c16 = jnp.broadcast_to(cand, (COUNT16_ROWS, LANES)).astype(I16)
    acc = jnp.zeros((COUNT16_ROWS, LANES), I16)
    one, zero = jnp.int16(1), jnp.int16(0)
    for c in range(s // COUNT16_ROWS):
        rows = ref[c * COUNT16_ROWS:(c + 1) * COUNT16_ROWS, :]
        acc = acc + jnp.where(rows > c16 if strict else rows >= c16, one, zero)
    return acc.astype(I32).sum(axis=0, keepdims=True)


def _radix_select16(ref, s, k_sel):
    def body(i, t):
        cand = t + lax.shift_left(jnp.int32(1), 15 - i)
        return jnp.where(_count16(ref, s, cand) >= k_sel, cand, t)
    return lax.fori_loop(0, 16, body, jnp.full((1, LANES), -2 ** 15, I32))


def _attn_prompt_kernel(qi_ref, wt_ref, qb_ref, kid_ref, kd_ref, vt_ref, o_ref, key_ref, hi_ref, lo_ref,
                        lo2_ref, bias_ref, lg_ref, e_ref, ot_ref, rhsi_ref, rhsq_ref, *, s_variants,
                        k_sel, n_bits):
    j = pl.program_id(1)
    t0 = j * Q_BLOCK
    gran = s_variants[0]
    lane_q = lax.broadcasted_iota(I32, (Q_BLOCK, LANES), 1)
    n_pairs = N_HEADS // 2
    pairs_per_kv = (N_HEADS // N_KV) // 2
    for p in range(n_pairs):
        rhsi_ref[p] = _split_heads_rhs(qi_ref[:, p * LANES:(p + 1) * LANES], lane_q)
        rhsq_ref[p] = _split_heads_rhs(qb_ref[:, p * LANES:(p + 1) * LANES], lane_q)

    def process(s):
        w = wt_ref[...]
        for c in range(s // SCORE_ROWS):
            r0 = c * SCORE_ROWS
            kid_c = kid_ref[r0:r0 + SCORE_ROWS, :]
            score = None
            for p in range(n_pairs):
                d = lax.dot_general(kid_c, rhsi_ref[p], _NT, preferred_element_type=F32)
                term = (jnp.maximum(d[:, 0:LANES], 0.0) * w[2 * p:2 * p + 1, :]
                        + jnp.maximum(d[:, LANES:], 0.0) * w[2 * p + 1:2 * p + 2, :])
                score = term if score is None else score + term
            key = _sortable_key(score)
            if r0 + SCORE_ROWS > s - gran:
                row = r0 + lax.broadcasted_iota(I32, (SCORE_ROWS, LANES), 0)
                tq = t0 + lax.broadcasted_iota(I32, (SCORE_ROWS, LANES), 1)
                key = jnp.where(row <= tq, key, INT_MIN)
            key_ref[r0:r0 + SCORE_ROWS, :] = key
            hi_ref[r0:r0 + SCORE_ROWS, :] = (key >> 16).astype(I16)
            lo_ref[r0:r0 + SCORE_ROWS, :] = ((key & 0xFFFF) - 2 ** 15).astype(I16)

        t_hi = _radix_select16(hi_ref, s, k_sel)
        n_gt_hi = _count16(hi_ref, s, t_hi, strict=True)
        t_hi16 = jnp.broadcast_to(t_hi, (COUNT16_ROWS, LANES)).astype(I16)
        for c in range(s // COUNT16_ROWS):
            sl = slice(c * COUNT16_ROWS, (c + 1) * COUNT16_ROWS)
            lo2_ref[sl, :] = jnp.where(hi_ref[sl, :] == t_hi16, lo_ref[sl, :], jnp.int16(-2 ** 15))
        t_lo = _radix_select16(lo2_ref, s, k_sel - n_gt_hi)
        thr = lax.shift_left(t_hi, 16) | ((t_lo + 2 ** 15) & 0xFFFF)
        thr = jnp.maximum(thr, INT_MIN + 1)
        n_ge = n_gt_hi + _count16(lo2_ref, s, t_lo)
        bias_ref[0:s, :] = jnp.where(key_ref[0:s, :] >= thr, 0.0, NEG_INF)

        @pl.when(jnp.max(n_ge) > k_sel)
        def _():
            need = k_sel - _count_rows(key_ref, s, lambda k, r0: k > thr)

            def count_eq_below(cand):
                def pred(k, r0):
                    row = r0 + lax.broadcasted_iota(I32, (COUNT_ROWS, LANES), 0)
                    return jnp.where(k == thr, row, cand) < cand
                return _count_rows(key_ref, s, pred)

            cut = _tie_cut(count_eq_below, need, n_bits, (1, LANES))
            key = key_ref[0:s, :]
            row = lax.broadcasted_iota(I32, (s, LANES), 0)
            bias_ref[0:s, :] = jnp.where(key == thr, jnp.where(row <= cut, 0.0, NEG_INF),
                                         jnp.where(key > thr, 0.0, NEG_INF))

        n_chunks = s // ATTN_ROWS
        fold = lambda x, op: functools.reduce(
            op, [x[r:r + 4 * SUBLANES, :] for r in range(0, ATTN_ROWS, 4 * SUBLANES)])

        def logits_chunk(p, c, mx):
            g = p // pairs_per_kv
            sl = slice(c * ATTN_ROWS, (c + 1) * ATTN_ROWS)
            b = bias_ref[sl, :]
            lg = lax.dot_general(kd_ref[sl, g * LANES:(g + 1) * LANES], rhsq_ref[p], _NT,
                                 preferred_element_type=F32) + jnp.concatenate([b, b], axis=1)
            lg_ref[p % 2, sl, :] = lg
            part = fold(lg, jnp.maximum)
            return part if mx is None else jnp.maximum(mx, part)

        def softmax_chunk(p, c, m, den):
            sl = slice(c * ATTN_ROWS, (c + 1) * ATTN_ROWS)
            e = jnp.exp(lg_ref[p % 2, sl, :] - m)
            e_ref[p % 2, sl, :] = e.astype(BF16)
            part = fold(e, jnp.add)
            return part if den is None else den + part

        mx = None
        for c in range(n_chunks):
            mx = logits_chunk(0, c, mx)
        for p in range(n_pairs):
            g = p // pairs_per_kv
            m = jnp.max(mx, axis=0, keepdims=True)
            mx, den = None, None
            for c in range(n_chunks):
                if p + 1 < n_pairs:
                    mx = logits_chunk(p + 1, c, mx)
                den = softmax_chunk(p, c, m, den)
            pv = jnp.dot(vt_ref[g * HEAD_DIM:(g + 1) * HEAD_DIM, 0:s], e_ref[p % 2, 0:s, :],
                         preferred_element_type=F32)
            pv = pv / jnp.sum(den, axis=0, keepdims=True)
            ot_ref[(2 * p) * HEAD_DIM:(2 * p + 1) * HEAD_DIM, :] = pv[:, 0:LANES]
            ot_ref[(2 * p + 1) * HEAD_DIM:(2 * p + 2) * HEAD_DIM, :] = pv[:, LANES:]
        o_ref[...] = ot_ref[...].T.astype(BF16)

    blocks_per_variant = gran // Q_BLOCK
    for vi, s in enumerate(s_variants):
        if len(s_variants) == 1:
            process(s)
        else:
            pl.when(j // blocks_per_variant == vi)(functools.partial(process, s))


def _attn_prompt_call(qi, wt, qb, kid, kd, vt, batch, seq, k_sel):
    n_tok = batch * seq
    nq = seq // Q_BLOCK
    n_var = max(v for v in (1, 2, 4, 8) if nq % v == 0 and seq // v % ATTN_ROWS == 0)
    s_variants = tuple(seq // n_var * (v + 1) for v in range(n_var))
    qw = qb.shape[1]
    qblk = lambda w: pl.BlockSpec((Q_BLOCK, w), lambda b, j: (b * nq + j, 0))
    per_b = lambda w: pl.BlockSpec((seq, w), lambda b, j: (b, 0))
    return pl.pallas_call(
        functools.partial(_attn_prompt_kernel, s_variants=s_variants, k_sel=k_sel,
                          n_bits=(seq - 1).bit_length()),
        out_shape=jax.ShapeDtypeStruct((n_tok, qw), BF16),
        grid=(batch, nq),
        in_specs=[qblk(qi.shape[1]),
                  pl.BlockSpec((N_IDX_HEADS, Q_BLOCK), lambda b, j: (0, b * nq + j)),
                  qblk(qw), per_b(LANES), per_b(2 * LANES),
                  pl.BlockSpec((LANES, seq), lambda b, j: (0, b))],
        out_specs=qblk(qw),
        scratch_shapes=[pltpu.VMEM((seq, LANES), I32), pltpu.VMEM((seq, LANES), I16),
                        pltpu.VMEM((seq, LANES), I16), pltpu.VMEM((seq, LANES), I16),
                        pltpu.VMEM((seq, LANES), F32), pltpu.VMEM((2, seq, 2 * Q_BLOCK), F32),
                        pltpu.VMEM((2, seq, 2 * Q_BLOCK), BF16), pltpu.VMEM((qw, Q_BLOCK), F32),
                        pltpu.VMEM((N_HEADS // 2, 2 * Q_BLOCK, LANES), BF16),
                        pltpu.VMEM((N_HEADS // 2, 2 * Q_BLOCK, LANES), BF16)],
        compiler_params=pltpu.CompilerParams(
            dimension_semantics=("arbitrary", "arbitrary"), vmem_limit_bytes=VMEM_LIMIT_BYTES),
        name="attn_prompt",
    )(qi, wt, qb, kid, kd, vt)


def _attn_sample_kernel(pt_ref, qi_ref, wc_ref, qp_ref, kin_ref, kn_ref, vn_ref, cidx_ref, ck_ref,
                        cv_ref, o_ref, idx_buf, k_buf, v_buf, key_ref, bias_ref, lg_ref, sem, *, n_pages,
                        page, dec_seq, k_sel, chunk):
    b = pl.program_id(0)
    nb = pl.num_programs(0)
    slot = b % 2
    past = n_pages * page
    n_keys = past + LANES
    rows = N_IDX_HEADS * SUBLANES

    def page_cols(pg):
        return pl.ds(pl.multiple_of(pg * page, page), page)

    def idx_copy(bb, sl, pg):
        return pltpu.make_async_copy(cidx_ref.at[pt_ref[bb, pg]], idx_buf.at[sl, :, page_cols(pg)],
                                     sem.at[sl])

    def kv_copy(src, dst, s_id, pg):
        return pltpu.make_async_copy(src.at[pt_ref[b, pg]], dst.at[:, page_cols(pg)], sem.at[s_id])

    def for_pages(fn):
        def body(pg, carry):
            fn(pg)
            return carry
        lax.fori_loop(0, n_pages, body, 0)

    @pl.when(b == 0)
    def _():
        for_pages(lambda pg: idx_copy(0, 0, pg).start())

    for_pages(lambda pg: kv_copy(ck_ref, k_buf, 2, pg).start())
    for_pages(lambda pg: kv_copy(cv_ref, v_buf, 3, pg).start())

    @pl.when(b + 1 < nb)
    def _():
        for_pages(lambda pg: idx_copy(b + 1, 1 - slot, pg).start())

    idx_buf[slot, :, past:n_keys] = kin_ref[...]
    k_buf[:, past:n_keys] = kn_ref[...]
    v_buf[:, past:n_keys] = vn_ref[...]

    for_pages(lambda pg: idx_copy(b, slot, pg).wait())

    chunks = [(c0, min(chunk, past - c0)) for c0 in range(0, past, chunk)] + [(past, LANES)]

    qi = qi_ref[...]
    wc = wc_ref[...]
    for c0, n in chunks:
        d = jnp.dot(qi, idx_buf[slot, :, c0:c0 + n].astype(BF16), preferred_element_type=F32)
        sh = jnp.maximum(d, 0.0) * wc
        score = sh[0:SUBLANES, :]
        for hh in range(1, N_IDX_HEADS):
            score = score + sh[hh * SUBLANES:(hh + 1) * SUBLANES, :]
        tslot = lax.broadcasted_iota(I32, (SUBLANES, n), 0)
        col = c0 + lax.broadcasted_iota(I32, (SUBLANES, n), 1)
        last = jnp.where(tslot < dec_seq, past + tslot, 0)
        key_ref[:, c0:c0 + n] = jnp.where(col <= last, _sortable_key(score), INT_MIN)

    def count_ge(cand):
        return jnp.sum((key_ref[...] >= cand).astype(I32), axis=1, keepdims=True)

    thr = jnp.maximum(_radix_select(count_ge, k_sel, (SUBLANES, 1)), INT_MIN + 1)
    ge = key_ref[...] >= thr
    n_ge = jnp.sum(ge.astype(I32), axis=1, keepdims=True)
    bias_ref[...] = jnp.where(ge, 0.0, NEG_INF)

    @pl.when(jnp.max(n_ge) > k_sel)
    def _():
        need = k_sel - jnp.sum((key_ref[...] > thr).astype(I32), axis=1, keepdims=True)

        def count_eq_below(cand):
            col = lax.broadcasted_iota(I32, (SUBLANES, n_keys), 1)
            return jnp.sum(jnp.where(key_ref[...] == thr, (col < cand).astype(I32), 0), axis=1,
                           keepdims=True)

        cut = _tie_cut(count_eq_below, need, (n_keys - 1).bit_length(), (SUBLANES, 1))
        key = key_ref[...]
        col = lax.broadcasted_iota(I32, (SUBLANES, n_keys), 1)
        bias_ref[...] = jnp.where(key == thr, jnp.where(col <= cut, 0.0, NEG_INF),
                                  jnp.where(key > thr, 0.0, NEG_INF))

    for_pages(lambda pg: kv_copy(ck_ref, k_buf, 2, pg).wait())
    for_pages(lambda pg: kv_copy(cv_ref, v_buf, 3, pg).wait())

    qp = qp_ref[...]
    for c0, n in chunks:
        bias = jnp.concatenate([bias_ref[:, c0:c0 + n]] * N_HEADS, axis=0)
        lg_ref[:, c0:c0 + n] = jnp.dot(qp, k_buf[:, c0:c0 + n].astype(BF16),
                                       preferred_element_type=F32) + bias
    m = jnp.max(lg_ref[...], axis=1, keepdims=True)
    den = jnp.zeros((rows, 1), F32)
    acc = jnp.zeros((rows, LANES), F32)
    for c0, n in chunks:
        e = jnp.exp(lg_ref[:, c0:c0 + n] - m)
        den = den + jnp.sum(e, axis=1, keepdims=True)
        acc = acc + lax.dot_general(e.astype(BF16), v_buf[:, c0:c0 + n].astype(BF16), _NT,
                                    preferred_element_type=F32)
    o_ref[...] = acc / den


def _attn_sample_call(page_table, qi_rows, wcol, qp_rows, ki_new, k_new, v_new, cache_idx, cache_k,
                      cache_v, dec_seq, k_sel):
    dec_batch, n_pages = page_table.shape
    page = cache_idx.shape[2]
    past = n_pages * page
    n_keys = past + LANES
    rows = N_IDX_HEADS * SUBLANES
    per_b = lambda r, w: pl.BlockSpec((None, r, w), lambda b, pt: (b, 0, 0))
    any_spec = pl.BlockSpec(memory_space=pl.ANY)
    grid_spec = pltpu.PrefetchScalarGridSpec(
        num_scalar_prefetch=1,
        grid=(dec_batch,),
        in_specs=[per_b(rows, D_IDX), per_b(rows, 1), per_b(rows, LANES), per_b(D_IDX, LANES),
                  per_b(LANES, LANES), per_b(LANES, LANES), any_spec, any_spec, any_spec],
        out_specs=per_b(rows, LANES),
        scratch_shapes=[pltpu.VMEM((2, D_IDX, n_keys), F32), pltpu.VMEM((LANES, n_keys), F32),
                        pltpu.VMEM((LANES, n_keys), F32), pltpu.VMEM((SUBLANES, n_keys), I32),
                        pltpu.VMEM((SUBLANES, n_keys), F32), pltpu.VMEM((rows, n_keys), F32),
                        pltpu.SemaphoreType.DMA((4,))],
    )
    return pl.pallas_call(
        functools.partial(_attn_sample_kernel, n_pages=n_pages, page=page, dec_seq=dec_seq,
                          k_sel=k_sel, chunk=min(4096, past)),
        out_shape=jax.ShapeDtypeStruct((dec_batch, rows, LANES), F32),
        grid_spec=grid_spec,
        compiler_params=pltpu.CompilerParams(
            dimension_semantics=("arbitrary",), vmem_limit_bytes=VMEM_LIMIT_BYTES),
        name="attn_sample",
    )(page_table, qi_rows, wcol, qp_rows, ki_new, k_new, v_new, cache_idx, cache_k, cache_v)


def _rope_tables(pos):
    half = ROT_DIM // 2
    inv = jnp.power(jnp.float32(ROPE_THETA), -jnp.arange(half, dtype=F32) * (2.0 / ROT_DIM))
    ang = pos.astype(F32)[:, None] * inv[None, :]
    cos, sin = jnp.cos(ang), jnp.sin(ang)
    n = pos.shape[0]
    rest = HEAD_DIM - ROT_DIM
    zh = jnp.zeros((n, half), F32)
    c = jnp.concatenate([cos, cos, jnp.ones((n, rest), F32)], axis=-1)
    sa = jnp.concatenate([-sin, zh, jnp.zeros((n, rest), F32)], axis=-1)
    sb = jnp.concatenate([zh, sin, jnp.zeros((n, rest), F32)], axis=-1)
    two = lambda a: jnp.concatenate([a, a], axis=-1)
    return two(c), two(sa), two(sb)


def kernel(x_prompt, x_sample, cache_k, cache_v, cache_idx_k, cache_conv, page_table, ffn1_norm,
           ffn1_w_gate, ffn1_w_up, ffn1_w_down, mix_norm, w_in, conv_w, conv_b, q_norm, k_norm, w_out,
           ffn2_norm, ffn2_w_gate, ffn2_w_up, ffn2_w_down):
    assert IDX_ROT == ROT_DIM and D_IDX == HEAD_DIM and 2 * HEAD_DIM == LANES
    batch, seq, d = x_prompt.shape
    dec_batch, dec_seq, _ = x_sample.shape
    depth, n_pool, page = cache_k.shape[:3]
    n_pages = page_table.shape[1]
    past = n_pages * page
    d_conv = conv_w.shape[-1]
    assert seq & (seq - 1) == 0 and dec_seq & (dec_seq - 1) == 0 and seq % Q_BLOCK == 0
    assert dec_seq <= SUBLANES and dec_seq >= CONV_W - 1 and page == LANES
    n_s = dec_batch * dec_seq
    tm_p = min(512, seq)

    tabs_p = _rope_tables(jnp.arange(seq, dtype=I32))
    tabs_s = _rope_tables(past + (jnp.arange(n_s, dtype=I32) % dec_seq))
    seg = jnp.kron(jnp.eye(LANES // HEAD_DIM, dtype=F32),
                   jnp.full((HEAD_DIM, HEAD_DIM), 1.0 / HEAD_DIM, F32)).astype(BF16)
    two = lambda a: jnp.concatenate([a, a], axis=-1)[None, :]

    xp = x_prompt.reshape(batch * seq, d)
    xs = x_sample.reshape(n_s, d)
    outs_p, outs_s = [], []
    for l in range(depth):
        wg1, wu1, wd1 = (w[l].astype(BF16) for w in (ffn1_w_gate, ffn1_w_up, ffn1_w_down))
        wg2, wu2, wd2 = (w[l].astype(BF16) for w in (ffn2_w_gate, ffn2_w_up, ffn2_w_down))
        n_in = w_in.shape[-1]
        n_proj = -(-(n_in - N_IDX_HEADS - D_IDX + LANES) // LANES) * LANES
        win = jnp.pad(w_in[l], ((0, 0), (0, n_proj - n_in))).astype(BF16)
        wo = w_out[l].astype(BF16)
        g1, gm, g2 = ffn1_norm[l][None, :], mix_norm[l][None, :], ffn2_norm[l][None, :]
        qn, kn = two(q_norm[l]), two(k_norm[l])
        cw, cb = conv_w[l], conv_b[l][None, :]

        x1 = _ffn_call(xp, g1, wg1, wu1, wd1, tm_p)
        (yc, qb, kd, vtb, kid, qi, wt, u_tail, k_t, v_t, ki_t) = _mix_call(
            x1, gm, win, cw, cb, qn, kn, seg, tabs_p, tm_p, seq)
        attn = _attn_prompt_call(qi, wt, qb, kid, kd, vtb, batch, seq, min(K_TOP, seq // 4))
        xp = _ffn_call(x1, g2, wg2, wu2, wd2, tm_p, mix=(yc, attn, wo))
        tiles_per_seq = seq // tm_p
        conv_p = u_tail.reshape(batch, tiles_per_seq, SUBLANES, d_conv)[:, -1, SUBLANES - (CONV_W - 1):]
        to_cache = lambda a: a.reshape(batch, N_KV, HEAD_DIM, seq).transpose(0, 3, 1, 2)
        outs_p.append((to_cache(k_t), to_cache(v_t), ki_t.transpose(0, 2, 1), conv_p))

        x1s = _ffn_call(xs, g1, wg1, wu1, wd1, n_s)
        prev = cache_conv[l]
        zrow = jnp.zeros((dec_batch, 1, d_conv), F32)
        pad = jnp.zeros((dec_batch, dec_seq - 2, d_conv), F32)
        e1 = jnp.concatenate([prev[:, 1:2], zrow, pad], axis=1).reshape(n_s, d_conv)
        e2 = jnp.concatenate([prev[:, 0:1], prev[:, 1:2], pad], axis=1).reshape(n_s, d_conv)
        (ycs, qbs, _, _, _, qis, wts, us, k_ts, v_ts, ki_ts) = _mix_call(
            x1s, gm, win, cw, cb, qn, kn, seg, tabs_s, n_s, dec_seq, prev=(e1, e2))

        def head_rows(a, width):
            a = a.reshape(dec_batch, dec_seq, N_HEADS, width).transpose(0, 2, 1, 3)
            a = jnp.pad(a, ((0, 0), (0, 0), (0, SUBLANES - dec_seq), (0, 0)))
            return a.reshape(dec_batch, N_HEADS * SUBLANES, width)

        def new_tile(a_t):
            r = a_t.shape[1]
            a = a_t[0].reshape(r, dec_batch, dec_seq).transpose(1, 0, 2)
            return jnp.pad(a, ((0, 0), (0, 0), (0, LANES - dec_seq)))

        qi_rows = head_rows(qis, D_IDX)
        q_rows = head_rows(qbs, HEAD_DIM)
        rep = N_HEADS // N_KV
        grp = (jnp.arange(N_HEADS * SUBLANES) // SUBLANES // rep)[None, :, None]
        zq = jnp.zeros_like(q_rows)
        qp_rows = jnp.concatenate([jnp.where(grp == 0, q_rows, zq), jnp.where(grp == 1, q_rows, zq)],
                                  axis=-1)
        wcol = head_rows(wts.T.reshape(n_s, N_IDX_HEADS), 1)
        ck = cache_k[l].transpose(0, 2, 3, 1).reshape(n_pool, LANES, page)
        cv = cache_v[l].transpose(0, 2, 3, 1).reshape(n_pool, LANES, page)
        ci = cache_idx_k[l].transpose(0, 2, 1)
        o_s = _attn_sample_call(page_table, qi_rows, wcol, qp_rows, new_tile(ki_ts), new_tile(k_ts),
                                new_tile(v_ts), ci, ck, cv, dec_seq, min(K_TOP, (past + dec_seq) // 4))
        o_s = o_s.reshape(dec_batch, N_HEADS, SUBLANES, N_KV, HEAD_DIM)[:, :, :dec_seq]
        hsel = (jnp.arange(N_HEADS) // rep)[None, :, None, None]
        attn_s = jnp.where(hsel == 0, o_s[:, :, :, 0], o_s[:, :, :, 1])
        attn_s = attn_s.transpose(0, 2, 1, 3).reshape(n_s, N_HEADS * HEAD_DIM).astype(BF16)
        xs = _ffn_call(x1s, g2, wg2, wu2, wd2, n_s, mix=(ycs, attn_s, wo))
        conv_s = us.reshape(dec_batch, dec_seq, d_conv)[:, dec_seq - (CONV_W - 1):]
        rows_s = lambda a_t: a_t[0].T.reshape(dec_batch, dec_seq, -1)
        outs_s.append((rows_s(k_ts).reshape(dec_batch, dec_seq, N_KV, HEAD_DIM),
                       rows_s(v_ts).reshape(dec_batch, dec_seq, N_KV, HEAD_DIM),
                       rows_s(ki_ts), conv_s))

    stack = lambda outs, n: jnp.stack([o[n] for o in outs])
    return (xp.reshape(batch, seq, d), xs.reshape(dec_batch, dec_seq, d),
            stack(outs_p, 0), stack(outs_p, 1), stack(outs_p, 2), stack(outs_p, 3),
            stack(outs_s, 0), stack(outs_s, 1), stack(outs_s, 2), stack(outs_s, 3))
```

```python
import functools

import jax
import jax.numpy as jnp
from jax import lax
from jax.experimental import pallas as pl
from jax.experimental.pallas import tpu as pltpu

F32 = jnp.float32
BF16 = jnp.bfloat16
I32 = jnp.int32

N_HEADS = 8
HEAD_DIM = 64
N_KV = 2
ROT_DIM = HEAD_DIM // 4
ROPE_THETA = 500000.0
N_IDX_HEADS = 8
D_IDX = 64
IDX_ROT = D_IDX // 4
K_TOP = 256
CONV_W = 3
EPS = 1e-6
IDX_SCALE = D_IDX ** -0.5 * N_IDX_HEADS ** -0.5
ATTN_SCALE = HEAD_DIM ** -0.5

LANES = 128
SUBLANES = 8
Q_BLOCK = 128
SCORE_ROWS = 128
COUNT_ROWS = 64
ATTN_ROWS = 256
VMEM_LIMIT_BYTES = 56 * 1024 * 1024

INT_MIN = -2 ** 31
LOWEST_FINITE_KEY = INT_MIN + 2 ** 23
NEG_INF = float("-inf")
_NT = (((1,), (1,)), ((), ()))


def _rmsnorm(x, g):
    y = x * lax.rsqrt(jnp.mean(x * x, axis=-1, keepdims=True) + EPS)
    return y * g


def _swiglu_residual(x, g_ref, wg_ref, wu_ref, wd_ref):
    h = _rmsnorm(x, g_ref[...]).astype(BF16)
    gate = jnp.dot(h, wg_ref[...], preferred_element_type=F32)
    up = jnp.dot(h, wu_ref[...], preferred_element_type=F32)
    act = (gate * jax.nn.sigmoid(gate) * up).astype(BF16)
    return x + 0.5 * jnp.dot(act, wd_ref[...], preferred_element_type=F32)


def _ffn_kernel(x_ref, g_ref, wg_ref, wu_ref, wd_ref, o_ref):
    o_ref[...] = _swiglu_residual(x_ref[...], g_ref, wg_ref, wu_ref, wd_ref)


def _out_ffn_kernel(x_ref, yc_ref, at_ref, wo_ref, g_ref, wg_ref, wu_ref, wd_ref, o_ref):
    mixed = jnp.concatenate([yc_ref[...], at_ref[...]], axis=-1)
    x = x_ref[...] + jnp.dot(mixed, wo_ref[...], preferred_element_type=F32)
    o_ref[...] = _swiglu_residual(x, g_ref, wg_ref, wu_ref, wd_ref)


def _const_spec(shape):
    return pl.BlockSpec(shape, lambda i: (0,) * len(shape))


def _ffn_call(x, g, wg, wu, wd, tm, mix=None):
    n_tok, d = x.shape
    d_ff = wg.shape[1]
    tok = lambda w: pl.BlockSpec((tm, w), lambda i: (i, 0))
    w_specs = [_const_spec((1, d)), _const_spec((d, d_ff)), _const_spec((d, d_ff)), _const_spec((d_ff, d))]
    if mix is None:
        body, args, specs = _ffn_kernel, (x, g, wg, wu, wd), [tok(d)] + w_specs
    else:
        yc, at, wo = mix
        body, args = _out_ffn_kernel, (x, yc, at, wo, g, wg, wu, wd)
        specs = [tok(d), tok(yc.shape[1]), tok(at.shape[1]), _const_spec(wo.shape)] + w_specs
    return pl.pallas_call(
        body,
        out_shape=jax.ShapeDtypeStruct((n_tok, d), F32),
        grid=(n_tok // tm,),
        in_specs=specs,
        out_specs=tok(d),
        compiler_params=pltpu.CompilerParams(
            dimension_semantics=("arbitrary",), vmem_limit_bytes=VMEM_LIMIT_BYTES),
        name="ffn" if mix is None else "out_ffn",
    )(*args)


def _seg_mean(x2, seg):
    hi = x2.astype(BF16)
    lo = (x2 - hi.astype(F32)).astype(BF16)
    return (jnp.dot(hi, seg, preferred_element_type=F32)
            + jnp.dot(lo, seg, preferred_element_type=F32))


def _rope(x, c, sa, sb):
    half = ROT_DIM // 2
    n = x.shape[-1]
    return x * c + pltpu.roll(x, n - half, 1) * sa + pltpu.roll(x, half, 1) * sb


def _dup_halves(x, lane):
    r = pltpu.roll(x, LANES // 2, 1)
    low = lane < LANES // 2
    return jnp.where(low, x, r), jnp.where(low, r, x)


def _mix_kernel(*refs, tm, seq_len, has_prev, d_conv):
    it = iter(refs)
    x_ref, g_ref, win_ref, cw_ref, cb_ref, qn_ref, kn_ref, seg_ref, cos_ref, sa_ref, sb_ref = (
        next(it) for _ in range(11))
    if has_prev:
        e1_ref, e2_ref = next(it), next(it)
    (yc_ref, qb_ref, kd_ref, vtb_ref, kid_ref, qi_ref, wt_ref, u_ref, kt_ref, vt_ref, kit_ref,
     uext_ref) = (next(it) for _ in range(12))

    i = pl.program_id(0)
    h = _rmsnorm(x_ref[...], g_ref[...]).astype(BF16)
    proj = jnp.dot(h, win_ref[...], preferred_element_type=F32)
    o_q = 3 * d_conv
    o_k = o_q + N_HEADS * HEAD_DIM
    o_v = o_k + N_KV * HEAD_DIM
    o_qi = o_v + N_KV * HEAD_DIM
    o_ki = o_qi + N_IDX_HEADS * D_IDX

    gb = proj[:, 0:d_conv]
    u = proj[:, d_conv:2 * d_conv] * proj[:, 2 * d_conv:3 * d_conv]

    @pl.when(i == 0)
    def _():
        uext_ref[0:SUBLANES, :] = jnp.zeros((SUBLANES, d_conv), F32)

    uext_ref[SUBLANES:SUBLANES + tm, :] = u
    u1 = uext_ref[SUBLANES - 1:SUBLANES - 1 + tm, :]
    u2 = uext_ref[SUBLANES - 2:SUBLANES - 2 + tm, :]
    t = (i * tm + lax.broadcasted_iota(I32, (tm, d_conv), 0)) & (seq_len - 1)
    p1 = e1_ref[...] if has_prev else 0.0
    p2 = e2_ref[...] if has_prev else 0.0
    u1 = jnp.where(t >= 1, u1, p1)
    u2 = jnp.where(t >= 2, u2, p2)
    cw = cw_ref[...]
    conv = cb_ref[...] + cw[0:1, :] * u2
    conv = conv + cw[1:2, :] * u1
    conv = conv + cw[2:3, :] * u
    yc_ref[...] = (gb * conv).astype(BF16)
    tail = u[tm - SUBLANES:tm, :]
    uext_ref[0:SUBLANES, :] = tail
    u_ref[...] = u if has_prev else tail

    c, sa, sb = cos_ref[...], sa_ref[...], sb_ref[...]
    seg = seg_ref[...]
    lane = lax.broadcasted_iota(I32, (tm, LANES), 1)

    def qk_norm_rope(xt, gain):
        ms = _seg_mean(xt * xt, seg)
        return _rope(xt * lax.rsqrt(ms + EPS) * gain, c, sa, sb)

    for j in range(N_HEADS * HEAD_DIM // LANES):
        sl = slice(j * LANES, (j + 1) * LANES)
        qt = qk_norm_rope(proj[:, o_q + j * LANES:o_q + (j + 1) * LANES], qn_ref[...])
        qb_ref[:, sl] = (qt * ATTN_SCALE).astype(BF16)
        qit = _rope(proj[:, o_qi + j * LANES:o_qi + (j + 1) * LANES], c, sa, sb)
        qi_ref[:, sl] = qit.astype(BF16)

    kt = qk_norm_rope(proj[:, o_k:o_v], kn_ref[...])
    kt_ref[...] = kt.T
    k0, k1 = _dup_halves(kt, lane)
    kd_ref[:, 0:LANES] = k0.astype(BF16)
    kd_ref[:, LANES:2 * LANES] = k1.astype(BF16)

    v_t = proj[:, o_v:o_qi].T
    vt_ref[...] = v_t
    vtb_ref[...] = v_t.astype(BF16)

    low = lane < D_IDX
    kiw = _rope(proj[:, o_ki:o_ki + LANES], jnp.where(low, c, 1.0), jnp.where(low, sa, 0.0),
                jnp.where(low, sb, 0.0))
    kid_ref[...] = _dup_halves(kiw, lane)[0].astype(BF16)
    kiw_t = kiw.T
    kit_ref[...] = kiw_t[0:D_IDX, :]
    wt_ref[...] = kiw_t[D_IDX:D_IDX + N_IDX_HEADS, :] * IDX_SCALE


def _mix_call(x, g, win, cw, cb, qn, kn, seg, tables, tm, seq_len, prev=None):
    n_tok, d = x.shape
    d_conv = cw.shape[1]
    n_proj = win.shape[1]
    tps = max(seq_len // tm, 1)
    span = tps * tm
    has_prev = prev is not None
    tok = lambda w: pl.BlockSpec((tm, w), lambda i: (i, 0))
    tab = pl.BlockSpec((tm, LANES), lambda i: (i % tps, 0))
    fmaj = lambda r: pl.BlockSpec((None, r, tm), lambda i: (i // tps, 0, i % tps))
    in_specs = [tok(d), _const_spec((1, d)), _const_spec((d, n_proj)), _const_spec(cw.shape),
                _const_spec((1, d_conv)), _const_spec((1, LANES)), _const_spec((1, LANES)),
                _const_spec((LANES, LANES)), tab, tab, tab]
    args = [x, g, win, cw, cb, qn, kn, seg, *tables]
    if has_prev:
        in_specs += [tok(d_conv), tok(d_conv)]
        args += list(prev)
    qw = N_HEADS * HEAD_DIM
    u_rows = tm if has_prev else SUBLANES
    n_tiles = n_tok // tm
    out_shape = [
        jax.ShapeDtypeStruct((n_tok, d_conv), BF16),
        jax.ShapeDtypeStruct((n_tok, qw), BF16),
        jax.ShapeDtypeStruct((n_tok, 2 * LANES), BF16),
        jax.ShapeDtypeStruct((LANES, n_tok), BF16),
        jax.ShapeDtypeStruct((n_tok, LANES), BF16),
        jax.ShapeDtypeStruct((n_tok, N_IDX_HEADS * D_IDX), BF16),
        jax.ShapeDtypeStruct((N_IDX_HEADS, n_tok), F32),
        jax.ShapeDtypeStruct((n_tiles * u_rows, d_conv), F32),
        jax.ShapeDtypeStruct((n_tok // span, LANES, span), F32),
        jax.ShapeDtypeStruct((n_tok // span, LANES, span), F32),
        jax.ShapeDtypeStruct((n_tok // span, D_IDX, span), F32),
    ]
    out_specs = [tok(d_conv), tok(qw), tok(2 * LANES),
                 pl.BlockSpec((LANES, tm), lambda i: (0, i)), tok(LANES), tok(N_IDX_HEADS * D_IDX),
                 pl.BlockSpec((N_IDX_HEADS, tm), lambda i: (0, i)),
                 pl.BlockSpec((u_rows, d_conv), lambda i: (i, 0)),
                 fmaj(LANES), fmaj(LANES), fmaj(D_IDX)]
    return pl.pallas_call(
        functools.partial(_mix_kernel, tm=tm, seq_len=seq_len, has_prev=has_prev, d_conv=d_conv),
        out_shape=out_shape,
        grid=(n_tiles,),
        in_specs=in_specs,
        out_specs=out_specs,
        scratch_shapes=[pltpu.VMEM((tm + SUBLANES, d_conv), F32)],
        compiler_params=pltpu.CompilerParams(
            dimension_semantics=("arbitrary",), vmem_limit_bytes=VMEM_LIMIT_BYTES),
        name="mix_sample" if has_prev else "mix_prompt",
    )(*args)


def _threshold_float(t):
    return pltpu.bitcast(jnp.where(t < 0, t ^ 0x7FFFFFFF, t), F32)


def _radix_select(count_ge, k_sel, shape, n_all):
    def body(i, carry):
        t, n_t = carry
        cand = t + lax.shift_left(jnp.int32(1), 31 - i)
        n_c = count_ge(_threshold_float(cand))
        take = n_c >= k_sel
        return jnp.where(take, cand, t), jnp.where(take, n_c, n_t)
    return lax.fori_loop(0, 32, body, (jnp.full(shape, INT_MIN, I32), jnp.full(shape, n_all, I32)))


def _select_threshold(t, n_t):
    thr = _threshold_float(jnp.maximum(t, LOWEST_FINITE_KEY))
    return thr, jnp.where(t == INT_MIN, 0, n_t)


def _tie_cut(count_eq_below, need, n_bits, shape):
    def body(i, j):
        cand = j + lax.shift_left(jnp.int32(1), n_bits - 1 - i)
        return jnp.where(count_eq_below(cand) < need, cand, j)
    return lax.fori_loop(0, n_bits, body, jnp.zeros(shape, I32))


def _split_heads_rhs(q_tile, lane):
    zero = jnp.zeros_like(q_tile)
    low = lane < LANES // 2
    return jnp.concatenate([jnp.where(low, q_tile, zero), jnp.where(low, zero, q_tile)], axis=0)


def _count_rows(ref, s, pred):
    acc = jnp.zeros((COUNT_ROWS, LANES), I32)
    for c in range(s // COUNT_ROWS):
        r0 = c * COUNT_ROWS
        acc = acc + jnp.where(pred(ref[r0:r0 + COUNT_ROWS, :], r0), 1, 0)
    return acc.sum(axis=0, keepdims=True)


def _attn_prompt_kernel(qi_ref, wt_ref, qb_ref, kid_ref, kd_ref, vt_ref, o_ref, score_ref, bias_ref,
                        lg_ref, e_ref, ot_ref, rhsi_ref, rhsq_ref, *, s_variants, k_sel, n_bits):
    j = pl.program_id(1)
    t0 = j * Q_BLOCK
    gran = s_variants[0]
    lane_q = lax.broadcasted_iota(I32, (Q_BLOCK, LANES), 1)
    n_pairs = N_HEADS // 2
    pairs_per_kv = (N_HEADS // N_KV) // 2
    for p in range(n_pairs):
        rhsi_ref[p] = _split_heads_rhs(qi_ref[:, p * LANES:(p + 1) * LANES], lane_q)
        rhsq_ref[p] = _split_heads_rhs(qb_ref[:, p * LANES:(p + 1) * LANES], lane_q)

    def process(s):
        w = wt_ref[...]
        for c in range(s // SCORE_ROWS):
            r0 = c * SCORE_ROWS
            kid_c = kid_ref[r0:r0 + SCORE_ROWS, :]
            score = None
            for p in range(n_pairs):
                d = lax.dot_general(kid_c, rhsi_ref[p], _NT, preferred_element_type=F32)
                term = (jnp.maximum(d[:, 0:LANES], 0.0) * w[2 * p:2 * p + 1, :]
                        + jnp.maximum(d[:, LANES:], 0.0) * w[2 * p + 1:2 * p + 2, :])
                score = term if score is None else score + term
            if r0 + SCORE_ROWS > s - gran:
                row = r0 + lax.broadcasted_iota(I32, (SCORE_ROWS, LANES), 0)
                tq = t0 + lax.broadcasted_iota(I32, (SCORE_ROWS, LANES), 1)
                score = jnp.where(row <= tq, score, NEG_INF)
            score_ref[r0:r0 + SCORE_ROWS, :] = score

        count_ge = lambda thr_c: _count_rows(score_ref, s, lambda x, r0: x >= thr_c)
        thr, n_ge = _select_threshold(*_radix_select(count_ge, k_sel, (1, LANES), s))
        bias_ref[0:s, :] = jnp.where(score_ref[0:s, :] >= thr, 0.0, NEG_INF)

        @pl.when(jnp.max(n_ge) > k_sel)
        def _():
            need = k_sel - _count_rows(score_ref, s, lambda x, r0: x > thr)

            def count_eq_below(cand):
                def pred(x, r0):
                    row = r0 + lax.broadcasted_iota(I32, (COUNT_ROWS, LANES), 0)
                    return jnp.where(x == thr, row, cand) < cand
                return _count_rows(score_ref, s, pred)

            cut = _tie_cut(count_eq_below, need, n_bits, (1, LANES))
            score = score_ref[0:s, :]
            row = lax.broadcasted_iota(I32, (s, LANES), 0)
            bias_ref[0:s, :] = jnp.where(score == thr, jnp.where(row <= cut, 0.0, NEG_INF),
                                         jnp.where(score > thr, 0.0, NEG_INF))

        n_chunks = s // ATTN_ROWS
        fold = lambda x, op: functools.reduce(
            op, [x[r:r + 4 * SUBLANES, :] for r in range(0, ATTN_ROWS, 4 * SUBLANES)])

        def logits_chunk(p, c, mx):
            g = p // pairs_per_kv
            sl = slice(c * ATTN_ROWS, (c + 1) * ATTN_ROWS)
            b = bias_ref[sl, :]
            lg = lax.dot_general(kd_ref[sl, g * LANES:(g + 1) * LANES], rhsq_ref[p], _NT,
                                 preferred_element_type=F32) + jnp.concatenate([b, b], axis=1)
            lg_ref[p % 2, sl, :] = lg
            part = fold(lg, jnp.maximum)
            return part if mx is None else jnp.maximum(mx, part)

        def softmax_chunk(p, c, m, den):
            sl = slice(c * ATTN_ROWS, (c + 1) * ATTN_ROWS)
            e = jnp.exp(lg_ref[p % 2, sl, :] - m)
            e_ref[p % 2, sl, :] = e.astype(BF16)
            part = fold(e, jnp.add)
            return part if den is None else den + part

        mx = None
        for c in range(n_chunks):
            mx = logits_chunk(0, c, mx)
        for p in range(n_pairs):
            g = p // pairs_per_kv
            m = jnp.max(mx, axis=0, keepdims=True)
            mx, den = None, None
            for c in range(n_chunks):
                if p + 1 < n_pairs:
                    mx = logits_chunk(p + 1, c, mx)
                den = softmax_chunk(p, c, m, den)
            pv = jnp.dot(vt_ref[g * HEAD_DIM:(g + 1) * HEAD_DIM, 0:s], e_ref[p % 2, 0:s, :],
                         preferred_element_type=F32)
            pv = pv / jnp.sum(den, axis=0, keepdims=True)
            ot_ref[(2 * p) * HEAD_DIM:(2 * p + 1) * HEAD_DIM, :] = pv[:, 0:LANES]
            ot_ref[(2 * p + 1) * HEAD_DIM:(2 * p + 2) * HEAD_DIM, :] = pv[:, LANES:]
        o_ref[...] = ot_ref[...].T.astype(BF16)

    blocks_per_variant = gran // Q_BLOCK
    for vi, s in enumerate(s_variants):
        if len(s_variants) == 1:
            process(s)
        else:
            pl.when(j // blocks_per_variant == vi)(functools.partial(process, s))


def _attn_prompt_call(qi, wt, qb, kid, kd, vt, batch, seq, k_sel):
    n_tok = batch * seq
    nq = seq // Q_BLOCK
    n_var = max(v for v in (1, 2, 4, 8) if nq % v == 0 and seq // v % ATTN_ROWS == 0)
    s_variants = tuple(seq // n_var * (v + 1) for v in range(n_var))
    qw = qb.shape[1]
    qblk = lambda w: pl.BlockSpec((Q_BLOCK, w), lambda b, j: (b * nq + j, 0))
    per_b = lambda w: pl.BlockSpec((seq, w), lambda b, j: (b, 0))
    return pl.pallas_call(
        functools.partial(_attn_prompt_kernel, s_variants=s_variants, k_sel=k_sel,
                          n_bits=(seq - 1).bit_length()),
        out_shape=jax.ShapeDtypeStruct((n_tok, qw), BF16),
        grid=(batch, nq),
        in_specs=[qblk(qi.shape[1]),
                  pl.BlockSpec((N_IDX_HEADS, Q_BLOCK), lambda b, j: (0, b * nq + j)),
                  qblk(qw), per_b(LANES), per_b(2 * LANES),
                  pl.BlockSpec((LANES, seq), lambda b, j: (0, b))],
        out_specs=qblk(qw),
        scratch_shapes=[pltpu.VMEM((seq, LANES), F32), pltpu.VMEM((seq, LANES), F32),
                        pltpu.VMEM((2, seq, 2 * Q_BLOCK), F32),
                        pltpu.VMEM((2, seq, 2 * Q_BLOCK), BF16), pltpu.VMEM((qw, Q_BLOCK), F32),
                        pltpu.VMEM((N_HEADS // 2, 2 * Q_BLOCK, LANES), BF16),
                        pltpu.VMEM((N_HEADS // 2, 2 * Q_BLOCK, LANES), BF16)],
        compiler_params=pltpu.CompilerParams(
            dimension_semantics=("arbitrary", "arbitrary"), vmem_limit_bytes=VMEM_LIMIT_BYTES),
        name="attn_prompt",
    )(qi, wt, qb, kid, kd, vt)


def _page_loop(n_pages, fn):
    def body(pg, carry):
        fn(pg)
        return carry
    lax.fori_loop(0, n_pages, body, 0)


def _page_cols(pg, page):
    return pl.ds(pl.multiple_of(pg * page, page), page)


def _sample_chunks(past, chunk):
    return [(c0, min(chunk, past - c0)) for c0 in range(0, past, chunk)] + [(past, LANES)]


def _idx_scores_kernel(pt_ref, qi_ref, wc_ref, kin_ref, cidx_ref, sc_ref, idx_buf, sem, *, n_pages, page,
                       dec_seq, chunk):
    b = pl.program_id(0)
    nb = pl.num_programs(0)
    slot = b % 2
    past = n_pages * page
    n_keys = past + LANES

    def idx_copy(bb, sl, pg):
        return pltpu.make_async_copy(cidx_ref.at[pt_ref[bb, pg]],
                                     idx_buf.at[sl, :, _page_cols(pg, page)], sem.at[sl])

    @pl.when(b == 0)
    def _():
        _page_loop(n_pages, lambda pg: idx_copy(0, 0, pg).start())

    @pl.when(b + 1 < nb)
    def _():
        _page_loop(n_pages, lambda pg: idx_copy(b + 1, 1 - slot, pg).start())

    idx_buf[slot, :, past:n_keys] = kin_ref[...]
    _page_loop(n_pages, lambda pg: idx_copy(b, slot, pg).wait())

    qi = qi_ref[...]
    wc = wc_ref[...]
    for c0, n in _sample_chunks(past, chunk):
        d = jnp.dot(qi, idx_buf[slot, :, c0:c0 + n].astype(BF16), preferred_element_type=F32)
        sh = jnp.maximum(d, 0.0) * wc
        score = sh[0:SUBLANES, :]
        for hh in range(1, N_IDX_HEADS):
            score = score + sh[hh * SUBLANES:(hh + 1) * SUBLANES, :]
        tslot = lax.broadcasted_iota(I32, (SUBLANES, n), 0)
        col = c0 + lax.broadcasted_iota(I32, (SUBLANES, n), 1)
        score = jnp.where(col <= past + tslot, score, NEG_INF)
        sc_ref[:, c0:c0 + n] = score[0:dec_seq, :]


def _idx_scores_call(page_table, qi_rows, wcol, ki_new, cache_idx, dec_seq):
    dec_batch, n_pages = page_table.shape
    page = cache_idx.shape[2]
    past = n_pages * page
    n_keys = past + LANES
    rows = N_IDX_HEADS * SUBLANES
    per_b = lambda r, w: pl.BlockSpec((None, r, w), lambda b, pt: (b, 0, 0))
    grid_spec = pltpu.PrefetchScalarGridSpec(
        num_scalar_prefetch=1,
        grid=(dec_batch,),
        in_specs=[per_b(rows, D_IDX), per_b(rows, 1), per_b(D_IDX, LANES),
                  pl.BlockSpec(memory_space=pl.ANY)],
        out_specs=per_b(dec_seq, n_keys),
        scratch_shapes=[pltpu.VMEM((2, D_IDX, n_keys), F32), pltpu.SemaphoreType.DMA((2,))],
    )
    return pl.pallas_call(
        functools.partial(_idx_scores_kernel, n_pages=n_pages, page=page, dec_seq=dec_seq,
                          chunk=min(4096, past)),
        out_shape=jax.ShapeDtypeStruct((dec_batch, dec_seq, n_keys), F32),
        grid_spec=grid_spec,
        compiler_params=pltpu.CompilerParams(
            dimension_semantics=("arbitrary",), vmem_limit_bytes=VMEM_LIMIT_BYTES),
        name="idx_scores_sample",
    )(page_table, qi_rows, wcol, ki_new, cache_idx)


def _select_kernel(sc_ref, bias_ref, *, k_sel):
    nq, n_keys = sc_ref.shape
    n_tiles = n_keys // LANES

    def count(pred):
        acc = jnp.zeros((nq, LANES), I32)
        for c in range(n_tiles):
            acc = acc + jnp.where(pred(sc_ref[:, c * LANES:(c + 1) * LANES], c * LANES), 1, 0)
        return jnp.sum(acc, axis=1, keepdims=True)

    count_ge = lambda thr_c: count(lambda x, c0: x >= thr_c)
    thr, n_ge = _select_threshold(*_radix_select(count_ge, k_sel, (nq, 1), n_keys))
    for c in range(n_tiles):
        sl = slice(c * LANES, (c + 1) * LANES)
        bias_ref[:, sl] = jnp.where(sc_ref[:, sl] >= thr, 0.0, NEG_INF)

    @pl.when(jnp.max(n_ge) > k_sel)
    def _():
        need = k_sel - count(lambda x, c0: x > thr)

        def count_eq_below(cand):
            def pred(x, c0):
                col = c0 + lax.broadcasted_iota(I32, (nq, LANES), 1)
                return jnp.where(x == thr, col, cand) < cand
            return count(pred)

        cut = _tie_cut(count_eq_below, need, (n_keys - 1).bit_length(), (nq, 1))
        for c in range(n_tiles):
            sl = slice(c * LANES, (c + 1) * LANES)
            x = sc_ref[:, sl]
            col = c * LANES + lax.broadcasted_iota(I32, (nq, LANES), 1)
            bias_ref[:, sl] = jnp.where(x == thr, jnp.where(col <= cut, 0.0, NEG_INF),
                                        jnp.where(x > thr, 0.0, NEG_INF))


def _select_call(scores, k_sel):
    return pl.pallas_call(
        functools.partial(_select_kernel, k_sel=k_sel),
        out_shape=jax.ShapeDtypeStruct(scores.shape, F32),
        compiler_params=pltpu.CompilerParams(vmem_limit_bytes=VMEM_LIMIT_BYTES),
        name="select_sample",
    )(scores)


def _attn_sample_kernel(pt_ref, qp_ref, kn_ref, vn_ref, bias_ref, ck_ref, cv_ref, o_ref, k_buf, v_buf,
                        lg_ref, sem, *, n_pages, page, dec_seq, chunk):
    b = pl.program_id(0)
    nb = pl.num_programs(0)
    slot = b % 2
    past = n_pages * page
    n_keys = past + LANES
    rows = N_HEADS * SUBLANES

    def kv_copy(src, dst, bb, sl, s_id, pg):
        return pltpu.make_async_copy(src.at[pt_ref[bb, pg]], dst.at[sl, :, _page_cols(pg, page)],
                                     sem.at[s_id])

    def start_all(bb, sl):
        _page_loop(n_pages, lambda pg: kv_copy(ck_ref, k_buf, bb, sl, sl, pg).start())
        _page_loop(n_pages, lambda pg: kv_copy(cv_ref, v_buf, bb, sl, 2 + sl, pg).start())

    @pl.when(b == 0)
    def _():
        start_all(0, 0)

    @pl.when(b + 1 < nb)
    def _():
        start_all(b + 1, 1 - slot)

    k_buf[slot, :, past:n_keys] = kn_ref[...]
    v_buf[slot, :, past:n_keys] = vn_ref[...]
    _page_loop(n_pages, lambda pg: kv_copy(ck_ref, k_buf, b, slot, slot, pg).wait())
    _page_loop(n_pages, lambda pg: kv_copy(cv_ref, v_buf, b, slot, 2 + slot, pg).wait())

    qp = qp_ref[...]
    chunks = _sample_chunks(past, chunk)
    for c0, n in chunks:
        bias = jnp.concatenate([bias_ref[:, c0:c0 + n], jnp.zeros((SUBLANES - dec_seq, n), F32)], axis=0)
        bias = jnp.concatenate([bias] * N_HEADS, axis=0)
        lg_ref[:, c0:c0 + n] = jnp.dot(qp, k_buf[slot, :, c0:c0 + n].astype(BF16),
                                       preferred_element_type=F32) + bias
    m = jnp.max(lg_ref[...], axis=1, keepdims=True)
    den = jnp.zeros((rows, 1), F32)
    acc = jnp.zeros((rows, LANES), F32)
    for c0, n in chunks:
        e = jnp.exp(lg_ref[:, c0:c0 + n] - m)
        den = den + jnp.sum(e, axis=1, keepdims=True)
        acc = acc + lax.dot_general(e.astype(BF16), v_buf[slot, :, c0:c0 + n].astype(BF16), _NT,
                                    preferred_element_type=F32)
    o_ref[...] = acc / den


def _attn_sample_call(page_table, qp_rows, k_new, v_new, bias, cache_k, cache_v, dec_seq):
    dec_batch, n_pages = page_table.shape
    page = cache_k.shape[2]
    past = n_pages * page
    n_keys = past + LANES
    rows = N_HEADS * SUBLANES
    per_b = lambda r, w: pl.BlockSpec((None, r, w), lambda b, pt: (b, 0, 0))
    any_spec = pl.BlockSpec(memory_space=pl.ANY)
    grid_spec = pltpu.PrefetchScalarGridSpec(
        num_scalar_prefetch=1,
        grid=(dec_batch,),
        in_specs=[per_b(rows, LANES), per_b(LANES, LANES), per_b(LANES, LANES), per_b(dec_seq, n_keys),
                  any_spec, any_spec],
        out_specs=per_b(rows, LANES),
        scratch_shapes=[pltpu.VMEM((2, LANES, n_keys), F32), pltpu.VMEM((2, LANES, n_keys), F32),
                        pltpu.VMEM((rows, n_keys), F32), pltpu.SemaphoreType.DMA((4,))],
    )
    return pl.pallas_call(
        functools.partial(_attn_sample_kernel, n_pages=n_pages, page=page, dec_seq=dec_seq,
                          chunk=min(4096, past)),
        out_shape=jax.ShapeDtypeStruct((dec_batch, rows, LANES), F32),
        grid_spec=grid_spec,
        compiler_params=pltpu.CompilerParams(
            dimension_semantics=("arbitrary",), vmem_limit_bytes=VMEM_LIMIT_BYTES),
        name="attn_sample",
    )(page_table, qp_rows, k_new, v_new, bias, cache_k, cache_v)


def _rope_tables(pos):
    half = ROT_DIM // 2
    inv = jnp.power(jnp.float32(ROPE_THETA), -jnp.arange(half, dtype=F32) * (2.0 / ROT_DIM))
    ang = pos.astype(F32)[:, None] * inv[None, :]
    cos, sin = jnp.cos(ang), jnp.sin(ang)
    n = pos.shape[0]
    rest = HEAD_DIM - ROT_DIM
    zh = jnp.zeros((n, half), F32)
    c = jnp.concatenate([cos, cos, jnp.ones((n, rest), F32)], axis=-1)
    sa = jnp.concatenate([-sin, zh, jnp.zeros((n, rest), F32)], axis=-1)
    sb = jnp.concatenate([zh, sin, jnp.zeros((n, rest), F32)], axis=-1)
    two = lambda a: jnp.concatenate([a, a], axis=-1)
    return two(c), two(sa), two(sb)


def kernel(x_prompt, x_sample, cache_k, cache_v, cache_idx_k, cache_conv, page_table, ffn1_norm,
           ffn1_w_gate, ffn1_w_up, ffn1_w_down, mix_norm, w_in, conv_w, conv_b, q_norm, k_norm, w_out,
           ffn2_norm, ffn2_w_gate, ffn2_w_up, ffn2_w_down):
    assert IDX_ROT == ROT_DIM and D_IDX == HEAD_DIM and 2 * HEAD_DIM == LANES
    batch, seq, d = x_prompt.shape
    dec_batch, dec_seq, _ = x_sample.shape
    depth, n_pool, page = cache_k.shape[:3]
    n_pages = page_table.shape[1]
    past = n_pages * page
    d_conv = conv_w.shape[-1]
    assert seq & (seq - 1) == 0 and dec_seq & (dec_seq - 1) == 0 and seq % Q_BLOCK == 0
    assert dec_seq <= SUBLANES and dec_seq >= CONV_W - 1 and page == LANES
    n_s = dec_batch * dec_seq
    tm_p = min(512, seq)

    tabs_p = _rope_tables(jnp.arange(seq, dtype=I32))
    tabs_s = _rope_tables(past + (jnp.arange(n_s, dtype=I32) % dec_seq))
    seg = jnp.kron(jnp.eye(LANES // HEAD_DIM, dtype=F32),
                   jnp.full((HEAD_DIM, HEAD_DIM), 1.0 / HEAD_DIM, F32)).astype(BF16)
    two = lambda a: jnp.concatenate([a, a], axis=-1)[None, :]

    xp = x_prompt.reshape(batch * seq, d)
    xs = x_sample.reshape(n_s, d)
    outs_p, outs_s = [], []
    for l in range(depth):
        wg1, wu1, wd1 = (w[l].astype(BF16) for w in (ffn1_w_gate, ffn1_w_up, ffn1_w_down))
        wg2, wu2, wd2 = (w[l].astype(BF16) for w in (ffn2_w_gate, ffn2_w_up, ffn2_w_down))
        n_in = w_in.shape[-1]
        n_proj = -(-(n_in - N_IDX_HEADS - D_IDX + LANES) // LANES) * LANES
        win = jnp.pad(w_in[l], ((0, 0), (0, n_proj - n_in))).astype(BF16)
        wo = w_out[l].astype(BF16)
        g1, gm, g2 = ffn1_norm[l][None, :], mix_norm[l][None, :], ffn2_norm[l][None, :]
        qn, kn = two(q_norm[l]), two(k_norm[l])
        cw, cb = conv_w[l], conv_b[l][None, :]

        x1 = _ffn_call(xp, g1, wg1, wu1, wd1, tm_p)
        (yc, qb, kd, vtb, kid, qi, wt, u_tail, k_t, v_t, ki_t) = _mix_call(
            x1, gm, win, cw, cb, qn, kn, seg, tabs_p, tm_p, seq)
        attn = _attn_prompt_call(qi, wt, qb, kid, kd, vtb, batch, seq, min(K_TOP, seq // 4))
        xp = _ffn_call(x1, g2, wg2, wu2, wd2, tm_p, mix=(yc, attn, wo))
        tiles_per_seq = seq // tm_p
        conv_p = u_tail.reshape(batch, tiles_per_seq, SUBLANES, d_conv)[:, -1, SUBLANES - (CONV_W - 1):]
        to_cache = lambda a: a.reshape(batch, N_KV, HEAD_DIM, seq).transpose(0, 3, 1, 2)
        outs_p.append((to_cache(k_t), to_cache(v_t), ki_t.transpose(0, 2, 1), conv_p))

        x1s = _ffn_call(xs, g1, wg1, wu1, wd1, n_s)
        prev = cache_conv[l]
        zrow = jnp.zeros((dec_batch, 1, d_conv), F32)
        pad = jnp.zeros((dec_batch, dec_seq - 2, d_conv), F32)
        e1 = jnp.concatenate([prev[:, 1:2], zrow, pad], axis=1).reshape(n_s, d_conv)
        e2 = jnp.concatenate([prev[:, 0:1], prev[:, 1:2], pad], axis=1).reshape(n_s, d_conv)
        (ycs, qbs, _, _, _, qis, wts, us, k_ts, v_ts, ki_ts) = _mix_call(
            x1s, gm, win, cw, cb, qn, kn, seg, tabs_s, n_s, dec_seq, prev=(e1, e2))

        def head_rows(a, width):
            a = a.reshape(dec_batch, dec_seq, N_HEADS, width).transpose(0, 2, 1, 3)
            a = jnp.pad(a, ((0, 0), (0, 0), (0, SUBLANES - dec_seq), (0, 0)))
            return a.reshape(dec_batch, N_HEADS * SUBLANES, width)

        def new_tile(a_t):
            r = a_t.shape[1]
            a = a_t[0].reshape(r, dec_batch, dec_seq).transpose(1, 0, 2)
            return jnp.pad(a, ((0, 0), (0, 0), (0, LANES - dec_seq)))

        qi_rows = head_rows(qis, D_IDX)
        q_rows = head_rows(qbs, HEAD_DIM)
        rep = N_HEADS // N_KV
        grp = (jnp.arange(N_HEADS * SUBLANES) // SUBLANES // rep)[None, :, None]
        zq = jnp.zeros_like(q_rows)
        qp_rows = jnp.concatenate([jnp.where(grp == 0, q_rows, zq), jnp.where(grp == 1, q_rows, zq)],
                                  axis=-1)
        wcol = head_rows(wts.T.reshape(n_s, N_IDX_HEADS), 1)
        ck = cache_k[l].transpose(0, 2, 3, 1).reshape(n_pool, LANES, page)
        cv = cache_v[l].transpose(0, 2, 3, 1).reshape(n_pool, LANES, page)
        ci = cache_idx_k[l].transpose(0, 2, 1)
        k_sel_s = min(K_TOP, (past + dec_seq) // 4)
        scores_s = _idx_scores_call(page_table, qi_rows, wcol, new_tile(ki_ts), ci, dec_seq)
        bias_s = _select_call(scores_s.reshape(n_s, past + LANES), k_sel_s)
        o_s = _attn_sample_call(page_table, qp_rows, new_tile(k_ts), new_tile(v_ts),
                                bias_s.reshape(dec_batch, dec_seq, past + LANES), ck, cv, dec_seq)
        o_s = o_s.reshape(dec_batch, N_HEADS, SUBLANES, N_KV, HEAD_DIM)[:, :, :dec_seq]
        hsel = (jnp.arange(N_HEADS) // rep)[None, :, None, None]
        attn_s = jnp.where(hsel == 0, o_s[:, :, :, 0], o_s[:, :, :, 1])
        attn_s = attn_s.transpose(0, 2, 1, 3).reshape(n_s, N_HEADS * HEAD_DIM).astype(BF16)
        xs = _ffn_call(x1s, g2, wg2, wu2, wd2, n_s, mix=(ycs, attn_s, wo))
        conv_s = us.reshape(dec_batch, dec_seq, d_conv)[:, dec_seq - (CONV_W - 1):]
        rows_s = lambda a_t: a_t[0].T.reshape(dec_batch, dec_seq, -1)
        outs_s.append((rows_s(k_ts).reshape(dec_batch, dec_seq, N_KV, HEAD_DIM),
                       rows_s(v_ts).reshape(dec_batch, dec_seq, N_KV, HEAD_DIM),
                       rows_s(ki_ts), conv_s))

    stack = lambda outs, n: jnp.stack([o[n] for o in outs])
    return (xp.reshape(batch, seq, d), xs.reshape(dec_batch, dec_seq, d),
            stack(outs_p, 0), stack(outs_p, 1), stack(outs_p, 2), stack(outs_p, 3),
            stack(outs_s, 0), stack(outs_s, 1), stack(outs_s, 2), stack(outs_s, 3))
```

```python
import functools

import jax
import jax.numpy as jnp
from jax import lax
from jax.experimental import pallas as pl
from jax.experimental.pallas import tpu as pltpu

F32 = jnp.float32
BF16 = jnp.bfloat16
I32 = jnp.int32

N_HEADS = 8
HEAD_DIM = 64
N_KV = 2
ROT_DIM = HEAD_DIM // 4
ROPE_THETA = 500000.0
N_IDX_HEADS = 8
D_IDX = 64
IDX_ROT = D_IDX // 4
K_TOP = 256
CONV_W = 3
EPS = 1e-6
IDX_SCALE = D_IDX ** -0.5 * N_IDX_HEADS ** -0.5
ATTN_SCALE = HEAD_DIM ** -0.5

LANES = 128
SUBLANES = 8
Q_BLOCK = 128
SCORE_ROWS = 128
COUNT_ROWS = 64
ATTN_ROWS = 256
VMEM_LIMIT_BYTES = 56 * 1024 * 1024

INT_MIN = -2 ** 31
LOWEST_FINITE_KEY = INT_MIN + 2 ** 23
NEG_INF = float("-inf")
_NT = (((1,), (1,)), ((), ()))


def _rmsnorm(x, g):
    y = x * lax.rsqrt(jnp.mean(x * x, axis=-1, keepdims=True) + EPS)
    return y * g


def _swiglu_residual(x, g_ref, wg_ref, wu_ref, wd_ref):
    h = _rmsnorm(x, g_ref[...]).astype(BF16)
    gate = jnp.dot(h, wg_ref[...], preferred_element_type=F32)
    up = jnp.dot(h, wu_ref[...], preferred_element_type=F32)
    act = (gate * jax.nn.sigmoid(gate) * up).astype(BF16)
    return x + 0.5 * jnp.dot(act, wd_ref[...], preferred_element_type=F32)


def _ffn_kernel(x_ref, g_ref, wg_ref, wu_ref, wd_ref, o_ref):
    o_ref[...] = _swiglu_residual(x_ref[...], g_ref, wg_ref, wu_ref, wd_ref)


def _out_ffn_kernel(x_ref, yc_ref, at_ref, wo_ref, g_ref, wg_ref, wu_ref, wd_ref, o_ref):
    mixed = jnp.concatenate([yc_ref[...], at_ref[...]], axis=-1)
    x = x_ref[...] + jnp.dot(mixed, wo_ref[...], preferred_element_type=F32)
    o_ref[...] = _swiglu_residual(x, g_ref, wg_ref, wu_ref, wd_ref)


def _const_spec(shape):
    return pl.BlockSpec(shape, lambda i: (0,) * len(shape))


def _ffn_call(x, g, wg, wu, wd, tm, mix=None):
    n_tok, d = x.shape
    d_ff = wg.shape[1]
    tok = lambda w: pl.BlockSpec((tm, w), lambda i: (i, 0))
    w_specs = [_const_spec((1, d)), _const_spec((d, d_ff)), _const_spec((d, d_ff)), _const_spec((d_ff, d))]
    if mix is None:
        body, args, specs = _ffn_kernel, (x, g, wg, wu, wd), [tok(d)] + w_specs
    else:
        yc, at, wo = mix
        body, args = _out_ffn_kernel, (x, yc, at, wo, g, wg, wu, wd)
        specs = [tok(d), tok(yc.shape[1]), tok(at.shape[1]), _const_spec(wo.shape)] + w_specs
    return pl.pallas_call(
        body,
        out_shape=jax.ShapeDtypeStruct((n_tok, d), F32),
        grid=(n_tok // tm,),
        in_specs=specs,
        out_specs=tok(d),
        compiler_params=pltpu.CompilerParams(
            dimension_semantics=("arbitrary",), vmem_limit_bytes=VMEM_LIMIT_BYTES),
        name="ffn" if mix is None else "out_ffn",
    )(*args)


def _seg_mean(x2, seg):
    hi = x2.astype(BF16)
    lo = (x2 - hi.astype(F32)).astype(BF16)
    return (jnp.dot(hi, seg, preferred_element_type=F32)
            + jnp.dot(lo, seg, preferred_element_type=F32))


def _rope(x, c, sa, sb):
    half = ROT_DIM // 2
    n = x.shape[-1]
    return x * c + pltpu.roll(x, n - half, 1) * sa + pltpu.roll(x, half, 1) * sb


def _dup_halves(x, lane):
    r = pltpu.roll(x, LANES // 2, 1)
    low = lane < LANES // 2
    return jnp.where(low, x, r), jnp.where(low, r, x)


def _mix_kernel(*refs, tm, seq_len, has_prev, d_conv, pipelined):
    it = iter(refs)
    x_ref, g_ref, win_ref, cw_ref, cb_ref, qn_ref, kn_ref, seg_ref, cos_ref, sa_ref, sb_ref = (
        next(it) for _ in range(11))
    if has_prev:
        e1_ref, e2_ref = next(it), next(it)
    (yc_ref, qb_ref, kd_ref, vtb_ref, kid_ref, qi_ref, wt_ref, u_ref, kt_ref, vt_ref, kit_ref,
     uext_ref) = (next(it) for _ in range(12))
    step = pl.program_id(0)

    def project():
        h = _rmsnorm(x_ref[...], g_ref[...]).astype(BF16)
        return jnp.dot(h, win_ref[...], preferred_element_type=F32)

    post = functools.partial(
        _mix_post, cw_ref=cw_ref, cb_ref=cb_ref, qn_ref=qn_ref, kn_ref=kn_ref, seg_ref=seg_ref,
        cos_ref=cos_ref, sa_ref=sa_ref, sb_ref=sb_ref, prev_refs=(e1_ref, e2_ref) if has_prev else None,
        out_refs=(yc_ref, qb_ref, kd_ref, vtb_ref, kid_ref, qi_ref, wt_ref, u_ref, kt_ref, vt_ref, kit_ref),
        uext_ref=uext_ref, tm=tm, seq_len=seq_len, d_conv=d_conv)

    @pl.when(step == 0)
    def _():
        uext_ref[0:SUBLANES, :] = jnp.zeros((SUBLANES, d_conv), F32)

    if not pipelined:
        post(project(), step)
        return

    proj_a, proj_b = next(it), next(it)

    @pl.when(step == 0)
    def _():
        proj_b[...] = jnp.zeros(proj_b.shape, F32)

    def both(dst, src):
        dst[...] = project()
        post(src, step - 1)

    pl.when(step % 2 == 0)(functools.partial(both, proj_a, proj_b))
    pl.when(step % 2 == 1)(functools.partial(both, proj_b, proj_a))


def _mix_post(proj, i, *, cw_ref, cb_ref, qn_ref, kn_ref, seg_ref, cos_ref, sa_ref, sb_ref, prev_refs,
              out_refs, uext_ref, tm, seq_len, d_conv):
    (yc_ref, qb_ref, kd_ref, vtb_ref, kid_ref, qi_ref, wt_ref, u_ref, kt_ref, vt_ref, kit_ref) = out_refs
    has_prev = prev_refs is not None
    if has_prev:
        e1_ref, e2_ref = prev_refs
    o_q = 3 * d_conv
    o_k = o_q + N_HEADS * HEAD_DIM
    o_v = o_k + N_KV * HEAD_DIM
    o_qi = o_v + N_KV * HEAD_DIM
    o_ki = o_qi + N_IDX_HEADS * D_IDX

    gb = proj[:, 0:d_conv]
    u = proj[:, d_conv:2 * d_conv] * proj[:, 2 * d_conv:3 * d_conv]

    uext_ref[SUBLANES:SUBLANES + tm, :] = u
    u1 = uext_ref[SUBLANES - 1:SUBLANES - 1 + tm, :]
    u2 = uext_ref[SUBLANES - 2:SUBLANES - 2 + tm, :]
    t = (i * tm + lax.broadcasted_iota(I32, (tm, d_conv), 0)) & (seq_len - 1)
    p1 = e1_ref[...] if has_prev else 0.0
    p2 = e2_ref[...] if has_prev else 0.0
    u1 = jnp.where(t >= 1, u1, p1)
    u2 = jnp.where(t >= 2, u2, p2)
    cw = cw_ref[...]
    conv = cb_ref[...] + cw[0:1, :] * u2
    conv = conv + cw[1:2, :] * u1
    conv = conv + cw[2:3, :] * u
    yc_ref[...] = (gb * conv).astype(BF16)
    tail = u[tm - SUBLANES:tm, :]
    uext_ref[0:SUBLANES, :] = tail
    u_ref[...] = u if has_prev else tail

    c, sa, sb = cos_ref[...], sa_ref[...], sb_ref[...]
    seg = seg_ref[...]
    lane = lax.broadcasted_iota(I32, (tm, LANES), 1)

    def qk_norm_rope(xt, gain):
        ms = _seg_mean(xt * xt, seg)
        return _rope(xt * lax.rsqrt(ms + EPS) * gain, c, sa, sb)

    for j in range(N_HEADS * HEAD_DIM // LANES):
        sl = slice(j * LANES, (j + 1) * LANES)
        qt = qk_norm_rope(proj[:, o_q + j * LANES:o_q + (j + 1) * LANES], qn_ref[...])
        qb_ref[:, sl] = (qt * ATTN_SCALE).astype(BF16)
        qit = _rope(proj[:, o_qi + j * LANES:o_qi + (j + 1) * LANES], c, sa, sb)
        qi_ref[:, sl] = qit.astype(BF16)

    kt = qk_norm_rope(proj[:, o_k:o_v], kn_ref[...])
    kt_ref[...] = kt.T
    k0, k1 = _dup_halves(kt, lane)
    kd_ref[:, 0:LANES] = k0.astype(BF16)
    kd_ref[:, LANES:2 * LANES] = k1.astype(BF16)

    v_t = proj[:, o_v:o_qi].T
    vt_ref[...] = v_t
    vtb_ref[...] = v_t.astype(BF16)

    low = lane < D_IDX
    kiw = _rope(proj[:, o_ki:o_ki + LANES], jnp.where(low, c, 1.0), jnp.where(low, sa, 0.0),
                jnp.where(low, sb, 0.0))
    kid_ref[...] = _dup_halves(kiw, lane)[0].astype(BF16)
    kiw_t = kiw.T
    kit_ref[...] = kiw_t[0:D_IDX, :]
    wt_ref[...] = kiw_t[D_IDX:D_IDX + N_IDX_HEADS, :] * IDX_SCALE


def _mix_call(x, g, win, cw, cb, qn, kn, seg, tables, tm, seq_len, prev=None):
    n_tok, d = x.shape
    d_conv = cw.shape[1]
    n_proj = win.shape[1]
    tps = max(seq_len // tm, 1)
    span = tps * tm
    has_prev = prev is not None
    n_tiles = n_tok // tm
    pipelined = n_tiles > 1
    t_in = (lambda i: jnp.minimum(i, n_tiles - 1)) if pipelined else (lambda i: i)
    t_out = (lambda i: jnp.maximum(i - 1, 0)) if pipelined else (lambda i: i)
    tok = lambda w: pl.BlockSpec((tm, w), lambda i: (t_out(i), 0))
    tab = pl.BlockSpec((tm, LANES), lambda i: (t_out(i) % tps, 0))
    fmaj = lambda r: pl.BlockSpec((None, r, tm), lambda i: (t_out(i) // tps, 0, t_out(i) % tps))
    in_specs = [pl.BlockSpec((tm, d), lambda i: (t_in(i), 0)), _const_spec((1, d)), _const_spec((d, n_proj)), _const_spec(cw.shape),
                _const_spec((1, d_conv)), _const_spec((1, LANES)), _const_spec((1, LANES)),
                _const_spec((LANES, LANES)), tab, tab, tab]
    args = [x, g, win, cw, cb, qn, kn, seg, *tables]
    if has_prev:
        in_specs += [tok(d_conv), tok(d_conv)]
        args += list(prev)
    qw = N_HEADS * HEAD_DIM
    u_rows = tm if has_prev else SUBLANES
    out_shape = [
        jax.ShapeDtypeStruct((n_tok, d_conv), BF16),
        jax.ShapeDtypeStruct((n_tok, qw), BF16),
        jax.ShapeDtypeStruct((n_tok, 2 * LANES), BF16),
        jax.ShapeDtypeStruct((LANES, n_tok), BF16),
        jax.ShapeDtypeStruct((n_tok, LANES), BF16),
        jax.ShapeDtypeStruct((n_tok, N_IDX_HEADS * D_IDX), BF16),
        jax.ShapeDtypeStruct((N_IDX_HEADS, n_tok), F32),
        jax.ShapeDtypeStruct((n_tiles * u_rows, d_conv), F32),
        jax.ShapeDtypeStruct((n_tok // span, LANES, span), F32),
        jax.ShapeDtypeStruct((n_tok // span, LANES, span), F32),
        jax.ShapeDtypeStruct((n_tok // span, D_IDX, span), F32),
    ]
    out_specs = [tok(d_conv), tok(qw), tok(2 * LANES),
                 pl.BlockSpec((LANES, tm), lambda i: (0, t_out(i))), tok(LANES), tok(N_IDX_HEADS * D_IDX),
                 pl.BlockSpec((N_IDX_HEADS, tm), lambda i: (0, t_out(i))),
                 pl.BlockSpec((u_rows, d_conv), lambda i: (t_out(i), 0)),
                 fmaj(LANES), fmaj(LANES), fmaj(D_IDX)]
    return pl.pallas_call(
        functools.partial(_mix_kernel, tm=tm, seq_len=seq_len, has_prev=has_prev, d_conv=d_conv,
                          pipelined=pipelined),
        out_shape=out_shape,
        grid=(n_tiles + 1 if pipelined else n_tiles,),
        in_specs=in_specs,
        out_specs=out_specs,
        scratch_shapes=[pltpu.VMEM((tm + SUBLANES, d_conv), F32)]
        + [pltpu.VMEM((tm, n_proj), F32)] * (2 if pipelined else 0),
        compiler_params=pltpu.CompilerParams(
            dimension_semantics=("arbitrary",), vmem_limit_bytes=VMEM_LIMIT_BYTES),
        name="mix_sample" if has_prev else "mix_prompt",
    )(*args)


def _threshold_float(t):
    return pltpu.bitcast(jnp.where(t < 0, t ^ 0x7FFFFFFF, t), F32)


def _radix_select(count_ge, k_sel, shape, n_all):
    def body(i, carry):
        t, n_t = carry
        cand = t + lax.shift_left(jnp.int32(1), 31 - i)
        n_c = count_ge(_threshold_float(cand))
        take = n_c >= k_sel
        return jnp.where(take, cand, t), jnp.where(take, n_c, n_t)
    return lax.fori_loop(0, 32, body, (jnp.full(shape, INT_MIN, I32), jnp.full(shape, n_all, I32)))


def _select_threshold(t, n_t):
    thr = _threshold_float(jnp.maximum(t, LOWEST_FINITE_KEY))
    return thr, jnp.where(t == INT_MIN, 0, n_t)


def _tie_cut(count_eq_below, need, n_bits, shape):
    def body(i, j):
        cand = j + lax.shift_left(jnp.int32(1), n_bits - 1 - i)
        return jnp.where(count_eq_below(cand) < need, cand, j)
    return lax.fori_loop(0, n_bits, body, jnp.zeros(shape, I32))


def _split_heads_rhs(q_tile, lane):
    zero = jnp.zeros_like(q_tile)
    low = lane < LANES // 2
    return jnp.concatenate([jnp.where(low, q_tile, zero), jnp.where(low, zero, q_tile)], axis=0)


def _count_rows(ref, s, pred):
    acc = jnp.zeros((COUNT_ROWS, LANES), I32)
    for c in range(s // COUNT_ROWS):
        r0 = c * COUNT_ROWS
        acc = acc + jnp.where(pred(ref[r0:r0 + COUNT_ROWS, :], r0), 1, 0)
    return acc.sum(axis=0, keepdims=True)


def _attn_prompt_kernel(qi_ref, wt_ref, qb_ref, kid_ref, kd_ref, vt_ref, o_ref, score_ref, bias_ref,
                        lg_ref, e_ref, ot_ref, rhsi_ref, rhsq_ref, *, s_variants, k_sel, n_bits):
    j = pl.program_id(1)
    t0 = j * Q_BLOCK
    gran = s_variants[0]
    lane_q = lax.broadcasted_iota(I32, (Q_BLOCK, LANES), 1)
    n_pairs = N_HEADS // 2
    pairs_per_kv = (N_HEADS // N_KV) // 2
    for p in range(n_pairs):
        rhsi_ref[p] = _split_heads_rhs(qi_ref[:, p * LANES:(p + 1) * LANES], lane_q)
        rhsq_ref[p] = _split_heads_rhs(qb_ref[:, p * LANES:(p + 1) * LANES], lane_q)

    def process(s):
        w = wt_ref[...]
        for c in range(s // SCORE_ROWS):
            r0 = c * SCORE_ROWS
            kid_c = kid_ref[r0:r0 + SCORE_ROWS, :]
            score = None
            for p in range(n_pairs):
                d = lax.dot_general(kid_c, rhsi_ref[p], _NT, preferred_element_type=F32)
                term = (jnp.maximum(d[:, 0:LANES], 0.0) * w[2 * p:2 * p + 1, :]
                        + jnp.maximum(d[:, LANES:], 0.0) * w[2 * p + 1:2 * p + 2, :])
                score = term if score is None else score + term
            if r0 + SCORE_ROWS > s - gran:
                row = r0 + lax.broadcasted_iota(I32, (SCORE_ROWS, LANES), 0)
                tq = t0 + lax.broadcasted_iota(I32, (SCORE_ROWS, LANES), 1)
                score = jnp.where(row <= tq, score, NEG_INF)
            score_ref[r0:r0 + SCORE_ROWS, :] = score

        count_ge = lambda thr_c: _count_rows(score_ref, s, lambda x, r0: x >= thr_c)
        thr, n_ge = _select_threshold(*_radix_select(count_ge, k_sel, (1, LANES), s))
        bias_ref[0:s, :] = jnp.where(score_ref[0:s, :] >= thr, 0.0, NEG_INF)

        @pl.when(jnp.max(n_ge) > k_sel)
        def _():
            need = k_sel - _count_rows(score_ref, s, lambda x, r0: x > thr)

            def count_eq_below(cand):
                def pred(x, r0):
                    row = r0 + lax.broadcasted_iota(I32, (COUNT_ROWS, LANES), 0)
                    return jnp.where(x == thr, row, cand) < cand
                return _count_rows(score_ref, s, pred)

            cut = _tie_cut(count_eq_below, need, n_bits, (1, LANES))
            score = score_ref[0:s, :]
            row = lax.broadcasted_iota(I32, (s, LANES), 0)
            bias_ref[0:s, :] = jnp.where(score == thr, jnp.where(row <= cut, 0.0, NEG_INF),
                                         jnp.where(score > thr, 0.0, NEG_INF))

        n_chunks = s // ATTN_ROWS
        fold = lambda x, op: functools.reduce(
            op, [x[r:r + 4 * SUBLANES, :] for r in range(0, ATTN_ROWS, 4 * SUBLANES)])

        def logits_chunk(p, c, mx):
            g = p // pairs_per_kv
            sl = slice(c * ATTN_ROWS, (c + 1) * ATTN_ROWS)
            b = bias_ref[sl, :]
            lg = lax.dot_general(kd_ref[sl, g * LANES:(g + 1) * LANES], rhsq_ref[p], _NT,
                                 preferred_element_type=F32) + jnp.concatenate([b, b], axis=1)
            lg_ref[p % 2, sl, :] = lg
            part = fold(lg, jnp.maximum)
            return part if mx is None else jnp.maximum(mx, part)

        def softmax_chunk(p, c, m, den):
            sl = slice(c * ATTN_ROWS, (c + 1) * ATTN_ROWS)
            e = jnp.exp(lg_ref[p % 2, sl, :] - m)
            e_ref[p % 2, sl, :] = e.astype(BF16)
            part = fold(e, jnp.add)
            return part if den is None else den + part

        mx = None
        for c in range(n_chunks):
            mx = logits_chunk(0, c, mx)
        for p in range(n_pairs):
            g = p // pairs_per_kv
            m = jnp.max(mx, axis=0, keepdims=True)
            mx, den = None, None
            for c in range(n_chunks):
                if p + 1 < n_pairs:
                    mx = logits_chunk(p + 1, c, mx)
                den = softmax_chunk(p, c, m, den)
            pv = jnp.dot(vt_ref[g * HEAD_DIM:(g + 1) * HEAD_DIM, 0:s], e_ref[p % 2, 0:s, :],
                         preferred_element_type=F32)
            pv = pv / jnp.sum(den, axis=0, keepdims=True)
            ot_ref[(2 * p) * HEAD_DIM:(2 * p + 1) * HEAD_DIM, :] = pv[:, 0:LANES]
            ot_ref[(2 * p + 1) * HEAD_DIM:(2 * p + 2) * HEAD_DIM, :] = pv[:, LANES:]
        o_ref[...] = ot_ref[...].T.astype(BF16)

    blocks_per_variant = gran // Q_BLOCK
    for vi, s in enumerate(s_variants):
        if len(s_variants) == 1:
            process(s)
        else:
            pl.when(j // blocks_per_variant == vi)(functools.partial(process, s))


def _attn_prompt_call(qi, wt, qb, kid, kd, vt, batch, seq, k_sel):
    n_tok = batch * seq
    nq = seq // Q_BLOCK
    n_var = max(v for v in (1, 2, 4, 8) if nq % v == 0 and seq // v % ATTN_ROWS == 0)
    s_variants = tuple(seq // n_var * (v + 1) for v in range(n_var))
    qw = qb.shape[1]
    qblk = lambda w: pl.BlockSpec((Q_BLOCK, w), lambda b, j: (b * nq + j, 0))
    per_b = lambda w: pl.BlockSpec((seq, w), lambda b, j: (b, 0))
    return pl.pallas_call(
        functools.partial(_attn_prompt_kernel, s_variants=s_variants, k_sel=k_sel,
                          n_bits=(seq - 1).bit_length()),
        out_shape=jax.ShapeDtypeStruct((n_tok, qw), BF16),
        grid=(batch, nq),
        in_specs=[qblk(qi.shape[1]),
                  pl.BlockSpec((N_IDX_HEADS, Q_BLOCK), lambda b, j: (0, b * nq + j)),
                  qblk(qw), per_b(LANES), per_b(2 * LANES),
                  pl.BlockSpec((LANES, seq), lambda b, j: (0, b))],
        out_specs=qblk(qw),
        scratch_shapes=[pltpu.VMEM((seq, LANES), F32), pltpu.VMEM((seq, LANES), F32),
                        pltpu.VMEM((2, seq, 2 * Q_BLOCK), F32),
                        pltpu.VMEM((2, seq, 2 * Q_BLOCK), BF16), pltpu.VMEM((qw, Q_BLOCK), F32),
                        pltpu.VMEM((N_HEADS // 2, 2 * Q_BLOCK, LANES), BF16),
                        pltpu.VMEM((N_HEADS // 2, 2 * Q_BLOCK, LANES), BF16)],
        compiler_params=pltpu.CompilerParams(
            dimension_semantics=("arbitrary", "arbitrary"), vmem_limit_bytes=VMEM_LIMIT_BYTES),
        name="attn_prompt",
    )(qi, wt, qb, kid, kd, vt)


def _page_loop(n_pages, fn):
    def body(pg, carry):
        fn(pg)
        return carry
    lax.fori_loop(0, n_pages, body, 0)


def _page_cols(pg, page):
    return pl.ds(pl.multiple_of(pg * page, page), page)


def _sample_chunks(past, chunk):
    return [(c0, min(chunk, past - c0)) for c0 in range(0, past, chunk)] + [(past, LANES)]


def _idx_scores_kernel(pt_ref, qi_ref, wc_ref, kin_ref, cidx_ref, sc_ref, idx_buf, sem, *, n_pages, page,
                       dec_seq, chunk):
    b = pl.program_id(0)
    nb = pl.num_programs(0)
    slot = b % 2
    past = n_pages * page
    n_keys = past + LANES

    def idx_copy(bb, sl, pg):
        return pltpu.make_async_copy(cidx_ref.at[pt_ref[bb, pg]],
                                     idx_buf.at[sl, :, _page_cols(pg, page)], sem.at[sl])

    @pl.when(b == 0)
    def _():
        _page_loop(n_pages, lambda pg: idx_copy(0, 0, pg).start())

    @pl.when(b + 1 < nb)
    def _():
        _page_loop(n_pages, lambda pg: idx_copy(b + 1, 1 - slot, pg).start())

    idx_buf[slot, :, past:n_keys] = kin_ref[...]
    all_pages = idx_buf.at[slot, :, 0:past]
    pltpu.make_async_copy(all_pages, all_pages, sem.at[slot]).wait()

    qi = qi_ref[...]
    wc = wc_ref[...]
    for c0, n in _sample_chunks(past, chunk):
        d = jnp.dot(qi, idx_buf[slot, :, c0:c0 + n].astype(BF16), preferred_element_type=F32)
        sh = jnp.maximum(d, 0.0) * wc
        score = sh[0:SUBLANES, :]
        for hh in range(1, N_IDX_HEADS):
            score = score + sh[hh * SUBLANES:(hh + 1) * SUBLANES, :]
        tslot = lax.broadcasted_iota(I32, (SUBLANES, n), 0)
        col = c0 + lax.broadcasted_iota(I32, (SUBLANES, n), 1)
        score = jnp.where(col <= past + tslot, score, NEG_INF)
        sc_ref[:, c0:c0 + n] = score[0:dec_seq, :]


def _idx_scores_call(page_table, qi_rows, wcol, ki_new, cache_idx, dec_seq):
    dec_batch, n_pages = page_table.shape
    page = cache_idx.shape[2]
    past = n_pages * page
    n_keys = past + LANES
    rows = N_IDX_HEADS * SUBLANES
    per_b = lambda r, w: pl.BlockSpec((None, r, w), lambda b, pt: (b, 0, 0))
    grid_spec = pltpu.PrefetchScalarGridSpec(
        num_scalar_prefetch=1,
        grid=(dec_batch,),
        in_specs=[per_b(rows, D_IDX), per_b(rows, 1), per_b(D_IDX, LANES),
                  pl.BlockSpec(memory_space=pl.ANY)],
        out_specs=per_b(dec_seq, n_keys),
        scratch_shapes=[pltpu.VMEM((2, D_IDX, n_keys), F32), pltpu.SemaphoreType.DMA((2,))],
    )
    return pl.pallas_call(
        functools.partial(_idx_scores_kernel, n_pages=n_pages, page=page, dec_seq=dec_seq,
                          chunk=min(4096, past)),
        out_shape=jax.ShapeDtypeStruct((dec_batch, dec_seq, n_keys), F32),
        grid_spec=grid_spec,
        compiler_params=pltpu.CompilerParams(
            dimension_semantics=("arbitrary",), vmem_limit_bytes=VMEM_LIMIT_BYTES),
        name="idx_scores_sample",
    )(page_table, qi_rows, wcol, ki_new, cache_idx)


def _select_kernel(sc_ref, bias_ref, *, k_sel):
    nq, n_keys = sc_ref.shape
    n_tiles = n_keys // LANES

    def count(pred):
        acc = jnp.zeros((nq, LANES), I32)
        for c in range(n_tiles):
            acc = acc + jnp.where(pred(sc_ref[:, c * LANES:(c + 1) * LANES], c * LANES), 1, 0)
        return jnp.sum(acc, axis=1, keepdims=True)

    count_ge = lambda thr_c: count(lambda x, c0: x >= thr_c)
    thr, n_ge = _select_threshold(*_radix_select(count_ge, k_sel, (nq, 1), n_keys))
    for c in range(n_tiles):
        sl = slice(c * LANES, (c + 1) * LANES)
        bias_ref[:, sl] = jnp.where(sc_ref[:, sl] >= thr, 0.0, NEG_INF)

    @pl.when(jnp.max(n_ge) > k_sel)
    def _():
        need = k_sel - count(lambda x, c0: x > thr)

        def count_eq_below(cand):
            def pred(x, c0):
                col = c0 + lax.broadcasted_iota(I32, (nq, LANES), 1)
                return jnp.where(x == thr, col, cand) < cand
            return count(pred)

        cut = _tie_cut(count_eq_below, need, (n_keys - 1).bit_length(), (nq, 1))
        for c in range(n_tiles):
            sl = slice(c * LANES, (c + 1) * LANES)
            x = sc_ref[:, sl]
            col = c * LANES + lax.broadcasted_iota(I32, (nq, LANES), 1)
            bias_ref[:, sl] = jnp.where(x == thr, jnp.where(col <= cut, 0.0, NEG_INF),
                                        jnp.where(x > thr, 0.0, NEG_INF))


def _select_call(scores, k_sel):
    return pl.pallas_call(
        functools.partial(_select_kernel, k_sel=k_sel),
        out_shape=jax.ShapeDtypeStruct(scores.shape, F32),
        compiler_params=pltpu.CompilerParams(vmem_limit_bytes=VMEM_LIMIT_BYTES),
        name="select_sample",
    )(scores)


def _attn_sample_kernel(pt_ref, qp_ref, kn_ref, vn_ref, bias_ref, ck_ref, cv_ref, o_ref, k_buf, v_buf,
                        lg_ref, sem, *, n_pages, page, dec_seq, chunk):
    b = pl.program_id(0)
    nb = pl.num_programs(0)
    slot = b % 2
    past = n_pages * page
    n_keys = past + LANES
    rows = N_HEADS * SUBLANES

    def kv_copy(src, dst, bb, sl, s_id, pg):
        return pltpu.make_async_copy(src.at[pt_ref[bb, pg]], dst.at[sl, :, _page_cols(pg, page)],
                                     sem.at[s_id])

    def start_all(bb, sl):
        _page_loop(n_pages, lambda pg: kv_copy(ck_ref, k_buf, bb, sl, sl, pg).start())
        _page_loop(n_pages, lambda pg: kv_copy(cv_ref, v_buf, bb, sl, 2 + sl, pg).start())

    @pl.when(b == 0)
    def _():
        start_all(0, 0)

    @pl.when(b + 1 < nb)
    def _():
        start_all(b + 1, 1 - slot)

    k_buf[slot, :, past:n_keys] = kn_ref[...]
    v_buf[slot, :, past:n_keys] = vn_ref[...]
    for buf, s_id in ((k_buf, slot), (v_buf, 2 + slot)):
        all_pages = buf.at[slot, :, 0:past]
        pltpu.make_async_copy(all_pages, all_pages, sem.at[s_id]).wait()

    qp = qp_ref[...]
    chunks = _sample_chunks(past, chunk)
    for c0, n in chunks:
        bias = jnp.concatenate([bias_ref[:, c0:c0 + n], jnp.zeros((SUBLANES - dec_seq, n), F32)], axis=0)
        bias = jnp.concatenate([bias] * N_HEADS, axis=0)
        lg_ref[:, c0:c0 + n] = jnp.dot(qp, k_buf[slot, :, c0:c0 + n].astype(BF16),
                                       preferred_element_type=F32) + bias
    m = jnp.max(lg_ref[...], axis=1, keepdims=True)
    den = jnp.zeros((rows, 1), F32)
    acc = jnp.zeros((rows, LANES), F32)
    for c0, n in chunks:
        e = jnp.exp(lg_ref[:, c0:c0 + n] - m)
        den = den + jnp.sum(e, axis=1, keepdims=True)
        acc = acc + lax.dot_general(e.astype(BF16), v_buf[slot, :, c0:c0 + n].astype(BF16), _NT,
                                    preferred_element_type=F32)
    o_ref[...] = acc / den


def _attn_sample_call(page_table, qp_rows, k_new, v_new, bias, cache_k, cache_v, dec_seq):
    dec_batch, n_pages = page_table.shape
    page = cache_k.shape[2]
    past = n_pages * page
    n_keys = past + LANES
    rows = N_HEADS * SUBLANES
    per_b = lambda r, w: pl.BlockSpec((None, r, w), lambda b, pt: (b, 0, 0))
    any_spec = pl.BlockSpec(memory_space=pl.ANY)
    grid_spec = pltpu.PrefetchScalarGridSpec(
        num_scalar_prefetch=1,
        grid=(dec_batch,),
        in_specs=[per_b(rows, LANES), per_b(LANES, LANES), per_b(LANES, LANES), per_b(dec_seq, n_keys),
                  any_spec, any_spec],
        out_specs=per_b(rows, LANES),
        scratch_shapes=[pltpu.VMEM((2, LANES, n_keys), F32), pltpu.VMEM((2, LANES, n_keys), F32),
                        pltpu.VMEM((rows, n_keys), F32), pltpu.SemaphoreType.DMA((4,))],
    )
    return pl.pallas_call(
        functools.partial(_attn_sample_kernel, n_pages=n_pages, page=page, dec_seq=dec_seq,
                          chunk=min(4096, past)),
        out_shape=jax.ShapeDtypeStruct((dec_batch, rows, LANES), F32),
        grid_spec=grid_spec,
        compiler_params=pltpu.CompilerParams(
            dimension_semantics=("arbitrary",), vmem_limit_bytes=VMEM_LIMIT_BYTES),
        name="attn_sample",
    )(page_table, qp_rows, k_new, v_new, bias, cache_k, cache_v)


def _rope_tables(pos):
    half = ROT_DIM // 2
    inv = jnp.power(jnp.float32(ROPE_THETA), -jnp.arange(half, dtype=F32) * (2.0 / ROT_DIM))
    ang = pos.astype(F32)[:, None] * inv[None, :]
    cos, sin = jnp.cos(ang), jnp.sin(ang)
    n = pos.shape[0]
    rest = HEAD_DIM - ROT_DIM
    zh = jnp.zeros((n, half), F32)
    c = jnp.concatenate([cos, cos, jnp.ones((n, rest), F32)], axis=-1)
    sa = jnp.concatenate([-sin, zh, jnp.zeros((n, rest), F32)], axis=-1)
    sb = jnp.concatenate([zh, sin, jnp.zeros((n, rest), F32)], axis=-1)
    two = lambda a: jnp.concatenate([a, a], axis=-1)
    return two(c), two(sa), two(sb)


def kernel(x_prompt, x_sample, cache_k, cache_v, cache_idx_k, cache_conv, page_table, ffn1_norm,
           ffn1_w_gate, ffn1_w_up, ffn1_w_down, mix_norm, w_in, conv_w, conv_b, q_norm, k_norm, w_out,
           ffn2_norm, ffn2_w_gate, ffn2_w_up, ffn2_w_down):
    assert IDX_ROT == ROT_DIM and D_IDX == HEAD_DIM and 2 * HEAD_DIM == LANES
    batch, seq, d = x_prompt.shape
    dec_batch, dec_seq, _ = x_sample.shape
    depth, n_pool, page = cache_k.shape[:3]
    n_pages = page_table.shape[1]
    past = n_pages * page
    d_conv = conv_w.shape[-1]
    assert seq & (seq - 1) == 0 and dec_seq & (dec_seq - 1) == 0 and seq % Q_BLOCK == 0
    assert dec_seq <= SUBLANES and dec_seq >= CONV_W - 1 and page == LANES
    n_s = dec_batch * dec_seq
    tm_p = min(512, seq)

    tabs_p = _rope_tables(jnp.arange(seq, dtype=I32))
    tabs_s = _rope_tables(past + (jnp.arange(n_s, dtype=I32) % dec_seq))
    seg = jnp.kron(jnp.eye(LANES // HEAD_DIM, dtype=F32),
                   jnp.full((HEAD_DIM, HEAD_DIM), 1.0 / HEAD_DIM, F32)).astype(BF16)
    two = lambda a: jnp.concatenate([a, a], axis=-1)[None, :]

    xp = x_prompt.reshape(batch * seq, d)
    xs = x_sample.reshape(n_s, d)
    outs_p, outs_s = [], []
    for l in range(depth):
        wg1, wu1, wd1 = (w[l].astype(BF16) for w in (ffn1_w_gate, ffn1_w_up, ffn1_w_down))
        wg2, wu2, wd2 = (w[l].astype(BF16) for w in (ffn2_w_gate, ffn2_w_up, ffn2_w_down))
        n_in = w_in.shape[-1]
        n_proj = -(-(n_in - N_IDX_HEADS - D_IDX + LANES) // LANES) * LANES
        win = jnp.pad(w_in[l], ((0, 0), (0, n_proj - n_in))).astype(BF16)
        wo = w_out[l].astype(BF16)
        g1, gm, g2 = ffn1_norm[l][None, :], mix_norm[l][None, :], ffn2_norm[l][None, :]
        qn, kn = two(q_norm[l]), two(k_norm[l])
        cw, cb = conv_w[l], conv_b[l][None, :]

        x1 = _ffn_call(xp, g1, wg1, wu1, wd1, tm_p)
        (yc, qb, kd, vtb, kid, qi, wt, u_tail, k_t, v_t, ki_t) = _mix_call(
            x1, gm, win, cw, cb, qn, kn, seg, tabs_p, tm_p, seq)
        attn = _attn_prompt_call(qi, wt, qb, kid, kd, vtb, batch, seq, min(K_TOP, seq // 4))
        xp = _ffn_call(x1, g2, wg2, wu2, wd2, tm_p, mix=(yc, attn, wo))
        tiles_per_seq = seq // tm_p
        conv_p = u_tail.reshape(batch, tiles_per_seq, SUBLANES, d_conv)[:, -1, SUBLANES - (CONV_W - 1):]
        to_cache = lambda a: a.reshape(batch, N_KV, HEAD_DIM, seq).transpose(0, 3, 1, 2)
        outs_p.append((to_cache(k_t), to_cache(v_t), ki_t.transpose(0, 2, 1), conv_p))

        x1s = _ffn_call(xs, g1, wg1, wu1, wd1, n_s)
        prev = cache_conv[l]
        zrow = jnp.zeros((dec_batch, 1, d_conv), F32)
        pad = jnp.zeros((dec_batch, dec_seq - 2, d_conv), F32)
        e1 = jnp.concatenate([prev[:, 1:2], zrow, pad], axis=1).reshape(n_s, d_conv)
        e2 = jnp.concatenate([prev[:, 0:1], prev[:, 1:2], pad], axis=1).reshape(n_s, d_conv)
        (ycs, qbs, _, _, _, qis, wts, us, k_ts, v_ts, ki_ts) = _mix_call(
            x1s, gm, win, cw, cb, qn, kn, seg, tabs_s, n_s, dec_seq, prev=(e1, e2))

        def head_rows(a, width):
            a = a.reshape(dec_batch, dec_seq, N_HEADS, width).transpose(0, 2, 1, 3)
            a = jnp.pad(a, ((0, 0), (0, 0), (0, SUBLANES - dec_seq), (0, 0)))
            return a.reshape(dec_batch, N_HEADS * SUBLANES, width)

        def new_tile(a_t):
            r = a_t.shape[1]
            a = a_t[0].reshape(r, dec_batch, dec_seq).transpose(1, 0, 2)
            return jnp.pad(a, ((0, 0), (0, 0), (0, LANES - dec_seq)))

        qi_rows = head_rows(qis, D_IDX)
        q_rows = head_rows(qbs, HEAD_DIM)
        rep = N_HEADS // N_KV
        grp = (jnp.arange(N_HEADS * SUBLANES) // SUBLANES // rep)[None, :, None]
        zq = jnp.zeros_like(q_rows)
        qp_rows = jnp.concatenate([jnp.where(grp == 0, q_rows, zq), jnp.where(grp == 1, q_rows, zq)],
                                  axis=-1)
        wcol = head_rows(wts.T.reshape(n_s, N_IDX_HEADS), 1)
        ck = cache_k[l].transpose(0, 2, 3, 1).reshape(n_pool, LANES, page)
        cv = cache_v[l].transpose(0, 2, 3, 1).reshape(n_pool, LANES, page)
        ci = cache_idx_k[l].transpose(0, 2, 1)
        k_sel_s = min(K_TOP, (past + dec_seq) // 4)
        scores_s = _idx_scores_call(page_table, qi_rows, wcol, new_tile(ki_ts), ci, dec_seq)
        bias_s = _select_call(scores_s.reshape(n_s, past + LANES), k_sel_s)
        o_s = _attn_sample_call(page_table, qp_rows, new_tile(k_ts), new_tile(v_ts),
                                bias_s.reshape(dec_batch, dec_seq, past + LANES), ck, cv, dec_seq)
        o_s = o_s.reshape(dec_batch, N_HEADS, SUBLANES, N_KV, HEAD_DIM)[:, :, :dec_seq]
        hsel = (jnp.arange(N_HEADS) // rep)[None, :, None, None]
        attn_s = jnp.where(hsel == 0, o_s[:, :, :, 0], o_s[:, :, :, 1])
        attn_s = attn_s.transpose(0, 2, 1, 3).reshape(n_s, N_HEADS * HEAD_DIM).astype(BF16)
        xs = _ffn_call(x1s, g2, wg2, wu2, wd2, n_s, mix=(ycs, attn_s, wo))
        conv_s = us.reshape(dec_batch, dec_seq, d_conv)[:, dec_seq - (CONV_W - 1):]
        rows_s = lambda a_t: a_t[0].T.reshape(dec_batch, dec_seq, -1)
        outs_s.append((rows_s(k_ts).reshape(dec_batch, dec_seq, N_KV, HEAD_DIM),
                       rows_s(v_ts).reshape(dec_batch, dec_seq, N_KV, HEAD_DIM),
                       rows_s(ki_ts), conv_s))

    stack = lambda outs, n: jnp.stack([o[n] for o in outs])
    return (xp.reshape(batch, seq, d), xs.reshape(dec_batch, dec_seq, d),
            stack(outs_p, 0), stack(outs_p, 1), stack(outs_p, 2), stack(outs_p, 3),
            stack(outs_s, 0), stack(outs_s, 1), stack(outs_s, 2), stack(outs_s, 3))
```

```python
import functools

import jax
import jax.numpy as jnp
from jax import lax
from jax.experimental import pallas as pl
from jax.experimental.pallas import tpu as pltpu

F32 = jnp.float32
BF16 = jnp.bfloat16
I32 = jnp.int32

N_HEADS = 8
HEAD_DIM = 64
N_KV = 2
ROT_DIM = HEAD_DIM // 4
ROPE_THETA = 500000.0
N_IDX_HEADS = 8
D_IDX = 64
IDX_ROT = D_IDX // 4
K_TOP = 256
CONV_W = 3
EPS = 1e-6
IDX_SCALE = D_IDX ** -0.5 * N_IDX_HEADS ** -0.5
ATTN_SCALE = HEAD_DIM ** -0.5

LANES = 128
SUBLANES = 8
Q_BLOCK = 128
SCORE_ROWS = 128
COUNT_ROWS = 64
ATTN_ROWS = 256
VMEM_LIMIT_BYTES = 56 * 1024 * 1024

INT_MIN = -2 ** 31
RADIX_BITS = 32
RADIX_UNROLL = 4
RADIX_UNROLL_ALL_BELOW = 768
LOWEST_FINITE_KEY = INT_MIN + 2 ** 23
NEG_INF = float("-inf")
_NT = (((1,), (1,)), ((), ()))


def _rmsnorm(x, g):
    y = x * lax.rsqrt(jnp.mean(x * x, axis=-1, keepdims=True) + EPS)
    return y * g


def _swiglu_residual(x, g_ref, wg_ref, wu_ref, wd_ref):
    h = _rmsnorm(x, g_ref[...]).astype(BF16)
    gate = jnp.dot(h, wg_ref[...], preferred_element_type=F32)
    up = jnp.dot(h, wu_ref[...], preferred_element_type=F32)
    act = (gate * jax.nn.sigmoid(gate) * up).astype(BF16)
    return x + 0.5 * jnp.dot(act, wd_ref[...], preferred_element_type=F32)


def _ffn_kernel(x_ref, g_ref, wg_ref, wu_ref, wd_ref, o_ref):
    o_ref[...] = _swiglu_residual(x_ref[...], g_ref, wg_ref, wu_ref, wd_ref)


def _out_ffn_kernel(x_ref, yc_ref, at_ref, wo_ref, g_ref, wg_ref, wu_ref, wd_ref, o_ref):
    mixed = jnp.concatenate([yc_ref[...], at_ref[...]], axis=-1)
    x = x_ref[...] + jnp.dot(mixed, wo_ref[...], preferred_element_type=F32)
    o_ref[...] = _swiglu_residual(x, g_ref, wg_ref, wu_ref, wd_ref)


def _const_spec(shape):
    return pl.BlockSpec(shape, lambda i: (0,) * len(shape))


def _ffn_call(x, g, wg, wu, wd, tm, mix=None):
    n_tok, d = x.shape
    d_ff = wg.shape[1]
    tok = lambda w: pl.BlockSpec((tm, w), lambda i: (i, 0))
    w_specs = [_const_spec((1, d)), _const_spec((d, d_ff)), _const_spec((d, d_ff)), _const_spec((d_ff, d))]
    if mix is None:
        body, args, specs = _ffn_kernel, (x, g, wg, wu, wd), [tok(d)] + w_specs
    else:
        yc, at, wo = mix
        body, args = _out_ffn_kernel, (x, yc, at, wo, g, wg, wu, wd)
        specs = [tok(d), tok(yc.shape[1]), tok(at.shape[1]), _const_spec(wo.shape)] + w_specs
    return pl.pallas_call(
        body,
        out_shape=jax.ShapeDtypeStruct((n_tok, d), F32),
        grid=(n_tok // tm,),
        in_specs=specs,
        out_specs=tok(d),
        compiler_params=pltpu.CompilerParams(
            dimension_semantics=("arbitrary",), vmem_limit_bytes=VMEM_LIMIT_BYTES),
        name="ffn" if mix is None else "out_ffn",
    )(*args)


def _seg_mean(x2, seg):
    hi = x2.astype(BF16)
    lo = (x2 - hi.astype(F32)).astype(BF16)
    return (jnp.dot(hi, seg, preferred_element_type=F32)
            + jnp.dot(lo, seg, preferred_element_type=F32))


def _rope(x, c, sa, sb):
    half = ROT_DIM // 2
    n = x.shape[-1]
    return x * c + pltpu.roll(x, n - half, 1) * sa + pltpu.roll(x, half, 1) * sb


def _dup_halves(x, lane):
    r = pltpu.roll(x, LANES // 2, 1)
    low = lane < LANES // 2
    return jnp.where(low, x, r), jnp.where(low, r, x)


def _mix_kernel(*refs, tm, seq_len, has_prev, d_conv, pipelined):
    it = iter(refs)
    x_ref, g_ref, win_ref, cw_ref, cb_ref, qn_ref, kn_ref, seg_ref, cos_ref, sa_ref, sb_ref = (
        next(it) for _ in range(11))
    if has_prev:
        e1_ref, e2_ref = next(it), next(it)
    (yc_ref, qb_ref, kd_ref, vtb_ref, kid_ref, qi_ref, wt_ref, u_ref, kt_ref, vt_ref, kit_ref,
     uext_ref) = (next(it) for _ in range(12))
    step = pl.program_id(0)

    def project():
        h = _rmsnorm(x_ref[...], g_ref[...]).astype(BF16)
        return jnp.dot(h, win_ref[...], preferred_element_type=F32)

    post = functools.partial(
        _mix_post, cw_ref=cw_ref, cb_ref=cb_ref, qn_ref=qn_ref, kn_ref=kn_ref, seg_ref=seg_ref,
        cos_ref=cos_ref, sa_ref=sa_ref, sb_ref=sb_ref, prev_refs=(e1_ref, e2_ref) if has_prev else None,
        out_refs=(yc_ref, qb_ref, kd_ref, vtb_ref, kid_ref, qi_ref, wt_ref, u_ref, kt_ref, vt_ref, kit_ref),
        uext_ref=uext_ref, tm=tm, seq_len=seq_len, d_conv=d_conv)

    @pl.when(step == 0)
    def _():
        uext_ref[0:SUBLANES, :] = jnp.zeros((SUBLANES, d_conv), F32)

    if not pipelined:
        post(project(), step)
        return

    proj_a, proj_b = next(it), next(it)

    @pl.when(step == 0)
    def _():
        proj_b[...] = jnp.zeros(proj_b.shape, F32)

    def both(dst, src):
        dst[...] = project()
        post(src, step - 1)

    pl.when(step % 2 == 0)(functools.partial(both, proj_a, proj_b))
    pl.when(step % 2 == 1)(functools.partial(both, proj_b, proj_a))


def _mix_post(proj, i, *, cw_ref, cb_ref, qn_ref, kn_ref, seg_ref, cos_ref, sa_ref, sb_ref, prev_refs,
              out_refs, uext_ref, tm, seq_len, d_conv):
    (yc_ref, qb_ref, kd_ref, vtb_ref, kid_ref, qi_ref, wt_ref, u_ref, kt_ref, vt_ref, kit_ref) = out_refs
    has_prev = prev_refs is not None
    if has_prev:
        e1_ref, e2_ref = prev_refs
    o_q = 3 * d_conv
    o_k = o_q + N_HEADS * HEAD_DIM
    o_v = o_k + N_KV * HEAD_DIM
    o_qi = o_v + N_KV * HEAD_DIM
    o_ki = o_qi + N_IDX_HEADS * D_IDX

    gb = proj[:, 0:d_conv]
    u = proj[:, d_conv:2 * d_conv] * proj[:, 2 * d_conv:3 * d_conv]

    uext_ref[SUBLANES:SUBLANES + tm, :] = u
    u1 = uext_ref[SUBLANES - 1:SUBLANES - 1 + tm, :]
    u2 = uext_ref[SUBLANES - 2:SUBLANES - 2 + tm, :]
    t = (i * tm + lax.broadcasted_iota(I32, (tm, d_conv), 0)) & (seq_len - 1)
    p1 = e1_ref[...] if has_prev else 0.0
    p2 = e2_ref[...] if has_prev else 0.0
    u1 = jnp.where(t >= 1, u1, p1)
    u2 = jnp.where(t >= 2, u2, p2)
    cw = cw_ref[...]
    conv = cb_ref[...] + cw[0:1, :] * u2
    conv = conv + cw[1:2, :] * u1
    conv = conv + cw[2:3, :] * u
    yc_ref[...] = (gb * conv).astype(BF16)
    tail = u[tm - SUBLANES:tm, :]
    uext_ref[0:SUBLANES, :] = tail
    u_ref[...] = u if has_prev else tail

    c, sa, sb = cos_ref[...], sa_ref[...], sb_ref[...]
    seg = seg_ref[...]
    lane = lax.broadcasted_iota(I32, (tm, LANES), 1)

    def qk_norm_rope(xt, gain):
        ms = _seg_mean(xt * xt, seg)
        return _rope(xt * lax.rsqrt(ms + EPS) * gain, c, sa, sb)

    for j in range(N_HEADS * HEAD_DIM // LANES):
        sl = slice(j * LANES, (j + 1) * LANES)
        qt = qk_norm_rope(proj[:, o_q + j * LANES:o_q + (j + 1) * LANES], qn_ref[...])
        qb_ref[:, sl] = (qt * ATTN_SCALE).astype(BF16)
        qit = _rope(proj[:, o_qi + j * LANES:o_qi + (j + 1) * LANES], c, sa, sb)
        qi_ref[:, sl] = qit.astype(BF16)

    kt = qk_norm_rope(proj[:, o_k:o_v], kn_ref[...])
    kt_ref[...] = kt.T
    k0, k1 = _dup_halves(kt, lane)
    kd_ref[:, 0:LANES] = k0.astype(BF16)
    kd_ref[:, LANES:2 * LANES] = k1.astype(BF16)

    v_t = proj[:, o_v:o_qi].T
    vt_ref[...] = v_t
    vtb_ref[...] = v_t.astype(BF16)

    low = lane < D_IDX
    kiw = _rope(proj[:, o_ki:o_ki + LANES], jnp.where(low, c, 1.0), jnp.where(low, sa, 0.0),
                jnp.where(low, sb, 0.0))
    kid_ref[...] = _dup_halves(kiw, lane)[0].astype(BF16)
    kiw_t = kiw.T
    kit_ref[...] = kiw_t[0:D_IDX, :]
    wt_ref[...] = kiw_t[D_IDX:D_IDX + N_IDX_HEADS, :] * IDX_SCALE


def _mix_call(x, g, win, cw, cb, qn, kn, seg, tables, tm, seq_len, prev=None):
    n_tok, d = x.shape
    d_conv = cw.shape[1]
    n_proj = win.shape[1]
    tps = max(seq_len // tm, 1)
    span = tps * tm
    has_prev = prev is not None
    n_tiles = n_tok // tm
    pipelined = n_tiles > 1
    t_in = (lambda i: jnp.minimum(i, n_tiles - 1)) if pipelined else (lambda i: i)
    t_out = (lambda i: jnp.maximum(i - 1, 0)) if pipelined else (lambda i: i)
    tok = lambda w: pl.BlockSpec((tm, w), lambda i: (t_out(i), 0))
    tab = pl.BlockSpec((tm, LANES), lambda i: (t_out(i) % tps, 0))
    fmaj = lambda r: pl.BlockSpec((None, r, tm), lambda i: (t_out(i) // tps, 0, t_out(i) % tps))
    in_specs = [pl.BlockSpec((tm, d), lambda i: (t_in(i), 0)), _const_spec((1, d)), _const_spec((d, n_proj)), _const_spec(cw.shape),
                _const_spec((1, d_conv)), _const_spec((1, LANES)), _const_spec((1, LANES)),
                _const_spec((LANES, LANES)), tab, tab, tab]
    args = [x, g, win, cw, cb, qn, kn, seg, *tables]
    if has_prev:
        in_specs += [tok(d_conv), tok(d_conv)]
        args += list(prev)
    qw = N_HEADS * HEAD_DIM
    u_rows = tm if has_prev else SUBLANES
    out_shape = [
        jax.ShapeDtypeStruct((n_tok, d_conv), BF16),
        jax.ShapeDtypeStruct((n_tok, qw), BF16),
        jax.ShapeDtypeStruct((n_tok, 2 * LANES), BF16),
        jax.ShapeDtypeStruct((LANES, n_tok), BF16),
        jax.ShapeDtypeStruct((n_tok, LANES), BF16),
        jax.ShapeDtypeStruct((n_tok, N_IDX_HEADS * D_IDX), BF16),
        jax.ShapeDtypeStruct((N_IDX_HEADS, n_tok), F32),
        jax.ShapeDtypeStruct((n_tiles * u_rows, d_conv), F32),
        jax.ShapeDtypeStruct((n_tok // span, LANES, span), F32),
        jax.ShapeDtypeStruct((n_tok // span, LANES, span), F32),
        jax.ShapeDtypeStruct((n_tok // span, D_IDX, span), F32),
    ]
    out_specs = [tok(d_conv), tok(qw), tok(2 * LANES),
                 pl.BlockSpec((LANES, tm), lambda i: (0, t_out(i))), tok(LANES), tok(N_IDX_HEADS * D_IDX),
                 pl.BlockSpec((N_IDX_HEADS, tm), lambda i: (0, t_out(i))),
                 pl.BlockSpec((u_rows, d_conv), lambda i: (t_out(i), 0)),
                 fmaj(LANES), fmaj(LANES), fmaj(D_IDX)]
    return pl.pallas_call(
        functools.partial(_mix_kernel, tm=tm, seq_len=seq_len, has_prev=has_prev, d_conv=d_conv,
                          pipelined=pipelined),
        out_shape=out_shape,
        grid=(n_tiles + 1 if pipelined else n_tiles,),
        in_specs=in_specs,
        out_specs=out_specs,
        scratch_shapes=[pltpu.VMEM((tm + SUBLANES, d_conv), F32)]
        + [pltpu.VMEM((tm, n_proj), F32)] * (2 if pipelined else 0),
        compiler_params=pltpu.CompilerParams(
            dimension_semantics=("arbitrary",), vmem_limit_bytes=VMEM_LIMIT_BYTES),
        name="mix_sample" if has_prev else "mix_prompt",
    )(*args)


def _threshold_float(t):
    return pltpu.bitcast(jnp.where(t < 0, t ^ 0x7FFFFFFF, t), F32)


def _radix_select(count_ge, k_sel, shape, n_all):
    return lax.fori_loop(0, RADIX_BITS, _radix_step(count_ge, k_sel), _radix_init(shape, n_all))


def _radix_init(shape, n_all):
    return jnp.full(shape, INT_MIN, I32), jnp.full(shape, n_all, I32)


def _radix_step(count_ge, k_sel):
    def step(i, carry):
        t, n_t = carry
        cand = t + lax.shift_left(jnp.int32(1), RADIX_BITS - 1 - i)
        n_c = count_ge(_threshold_float(cand))
        take = n_c >= k_sel
        return jnp.where(take, cand, t), jnp.where(take, n_c, n_t)
    return step


def _select_threshold(t, n_t):
    thr = _threshold_float(jnp.maximum(t, LOWEST_FINITE_KEY))
    return thr, jnp.where(t == INT_MIN, 0, n_t)


def _tie_cut(count_eq_below, need, n_bits, shape):
    def body(i, j):
        cand = j + lax.shift_left(jnp.int32(1), n_bits - 1 - i)
        return jnp.where(count_eq_below(cand) < need, cand, j)
    return lax.fori_loop(0, n_bits, body, jnp.zeros(shape, I32))


def _split_heads_rhs(q_tile, lane):
    zero = jnp.zeros_like(q_tile)
    low = lane < LANES // 2
    return jnp.concatenate([jnp.where(low, q_tile, zero), jnp.where(low, zero, q_tile)], axis=0)


def _count_rows(ref, s, pred):
    acc = jnp.zeros((COUNT_ROWS, LANES), I32)
    for c in range(s // COUNT_ROWS):
        r0 = c * COUNT_ROWS
        acc = acc + jnp.where(pred(ref[r0:r0 + COUNT_ROWS, :], r0), 1, 0)
    return acc.sum(axis=0, keepdims=True)


def _attn_prompt_kernel(qi_ref, wt_ref, qb_ref, kid_ref, kd_ref, vt_ref, o_ref, score_ref, bias_ref,
                        lg_ref, e_ref, ot_ref, rhsi_ref, rhsq_ref, *, s_variants, k_sel, n_bits):
    j = pl.program_id(1)
    t0 = j * Q_BLOCK
    gran = s_variants[0]
    lane_q = lax.broadcasted_iota(I32, (Q_BLOCK, LANES), 1)
    n_pairs = N_HEADS // 2
    pairs_per_kv = (N_HEADS // N_KV) // 2
    for p in range(n_pairs):
        rhsi_ref[p] = _split_heads_rhs(qi_ref[:, p * LANES:(p + 1) * LANES], lane_q)
        rhsq_ref[p] = _split_heads_rhs(qb_ref[:, p * LANES:(p + 1) * LANES], lane_q)

    def process(s):
        w = wt_ref[...]
        for c in range(s // SCORE_ROWS):
            r0 = c * SCORE_ROWS
            kid_c = kid_ref[r0:r0 + SCORE_ROWS, :]
            score = None
            for p in range(n_pairs):
                d = lax.dot_general(kid_c, rhsi_ref[p], _NT, preferred_element_type=F32)
                term = (jnp.maximum(d[:, 0:LANES], 0.0) * w[2 * p:2 * p + 1, :]
                        + jnp.maximum(d[:, LANES:], 0.0) * w[2 * p + 1:2 * p + 2, :])
                score = term if score is None else score + term
            if r0 + SCORE_ROWS > s - gran:
                row = r0 + lax.broadcasted_iota(I32, (SCORE_ROWS, LANES), 0)
                tq = t0 + lax.broadcasted_iota(I32, (SCORE_ROWS, LANES), 1)
                score = jnp.where(row <= tq, score, NEG_INF)
            score_ref[r0:r0 + SCORE_ROWS, :] = score

        n_chunks = s // ATTN_ROWS
        steps_per_pair = RADIX_BITS // n_pairs
        unroll = steps_per_pair if s <= RADIX_UNROLL_ALL_BELOW else RADIX_UNROLL
        trips = steps_per_pair // unroll
        jobs_per_trip = -(-n_chunks // trips)
        n_full, n_rem = divmod(n_chunks, jobs_per_trip)
        count_ge = lambda thr_c: _count_rows(score_ref, s, lambda x, r0: x >= thr_c)
        radix_step = _radix_step(count_ge, k_sel)
        carry = _radix_init((1, LANES), s)
        for p in range(n_pairs):
            g = p // pairs_per_kv

            def raw_logits(c, p=p, g=g):
                rows = pl.ds(pl.multiple_of(c * ATTN_ROWS, ATTN_ROWS), ATTN_ROWS)
                lg_ref[p, rows, :] = lax.dot_general(kd_ref[rows, g * LANES:(g + 1) * LANES],
                                                     rhsq_ref[p], _NT, preferred_element_type=F32)

            def trip(k, carry, n_jobs, p=p, raw_logits=raw_logits):
                for jj in range(n_jobs):
                    raw_logits(k * jobs_per_trip + jj)
                for uu in range(unroll):
                    carry = radix_step(p * steps_per_pair + k * unroll + uu, carry)
                return carry

            carry = lax.fori_loop(0, n_full, functools.partial(trip, n_jobs=jobs_per_trip), carry)
            done = n_full
            if n_rem:
                carry = lax.fori_loop(done, done + 1, functools.partial(trip, n_jobs=n_rem), carry)
                done += 1
            carry = lax.fori_loop(done, trips, functools.partial(trip, n_jobs=0), carry)
        thr, n_ge = _select_threshold(*carry)
        bias_ref[0:s, :] = jnp.where(score_ref[0:s, :] >= thr, 0.0, NEG_INF)

        @pl.when(jnp.max(n_ge) > k_sel)
        def _():
            need = k_sel - _count_rows(score_ref, s, lambda x, r0: x > thr)

            def count_eq_below(cand):
                def pred(x, r0):
                    row = r0 + lax.broadcasted_iota(I32, (COUNT_ROWS, LANES), 0)
                    return jnp.where(x == thr, row, cand) < cand
                return _count_rows(score_ref, s, pred)

            cut = _tie_cut(count_eq_below, need, n_bits, (1, LANES))
            score = score_ref[0:s, :]
            row = lax.broadcasted_iota(I32, (s, LANES), 0)
            bias_ref[0:s, :] = jnp.where(score == thr, jnp.where(row <= cut, 0.0, NEG_INF),
                                         jnp.where(score > thr, 0.0, NEG_INF))

        fold = lambda x, op: functools.reduce(
            op, [x[r:r + 4 * SUBLANES, :] for r in range(0, ATTN_ROWS, 4 * SUBLANES)])

        def logits_chunk(p, c, mx):
            sl = slice(c * ATTN_ROWS, (c + 1) * ATTN_ROWS)
            b = bias_ref[sl, :]
            lg = lg_ref[p, sl, :] + jnp.concatenate([b, b], axis=1)
            lg_ref[p, sl, :] = lg
            part = fold(lg, jnp.maximum)
            return part if mx is None else jnp.maximum(mx, part)

        def softmax_chunk(p, c, m, den):
            sl = slice(c * ATTN_ROWS, (c + 1) * ATTN_ROWS)
            e = jnp.exp(lg_ref[p, sl, :] - m)
            e_ref[p % 2, sl, :] = e.astype(BF16)
            part = fold(e, jnp.add)
            return part if den is None else den + part

        mx = None
        for c in range(n_chunks):
            mx = logits_chunk(0, c, mx)
        for p in range(n_pairs):
            g = p // pairs_per_kv
            m = jnp.max(mx, axis=0, keepdims=True)
            mx, den = None, None
            for c in range(n_chunks):
                if p + 1 < n_pairs:
                    mx = logits_chunk(p + 1, c, mx)
                den = softmax_chunk(p, c, m, den)
            pv = jnp.dot(vt_ref[g * HEAD_DIM:(g + 1) * HEAD_DIM, 0:s], e_ref[p % 2, 0:s, :],
                         preferred_element_type=F32)
            pv = pv / jnp.sum(den, axis=0, keepdims=True)
            ot_ref[(2 * p) * HEAD_DIM:(2 * p + 1) * HEAD_DIM, :] = pv[:, 0:LANES]
            ot_ref[(2 * p + 1) * HEAD_DIM:(2 * p + 2) * HEAD_DIM, :] = pv[:, LANES:]
        o_ref[...] = ot_ref[...].T.astype(BF16)

    blocks_per_variant = gran // Q_BLOCK
    for vi, s in enumerate(s_variants):
        if len(s_variants) == 1:
            process(s)
        else:
            pl.when(j // blocks_per_variant == vi)(functools.partial(process, s))


def _attn_prompt_call(qi, wt, qb, kid, kd, vt, batch, seq, k_sel):
    n_tok = batch * seq
    nq = seq // Q_BLOCK
    n_var = max(v for v in (1, 2, 4, 8) if nq % v == 0 and seq // v % ATTN_ROWS == 0)
    s_variants = tuple(seq // n_var * (v + 1) for v in range(n_var))
    qw = qb.shape[1]
    qblk = lambda w: pl.BlockSpec((Q_BLOCK, w), lambda b, j: (b * nq + j, 0))
    per_b = lambda w: pl.BlockSpec((seq, w), lambda b, j: (b, 0))
    return pl.pallas_call(
        functools.partial(_attn_prompt_kernel, s_variants=s_variants, k_sel=k_sel,
                          n_bits=(seq - 1).bit_length()),
        out_shape=jax.ShapeDtypeStruct((n_tok, qw), BF16),
        grid=(batch, nq),
        in_specs=[qblk(qi.shape[1]),
                  pl.BlockSpec((N_IDX_HEADS, Q_BLOCK), lambda b, j: (0, b * nq + j)),
                  qblk(qw), per_b(LANES), per_b(2 * LANES),
                  pl.BlockSpec((LANES, seq), lambda b, j: (0, b))],
        out_specs=qblk(qw),
        scratch_shapes=[pltpu.VMEM((seq, LANES), F32), pltpu.VMEM((seq, LANES), F32),
                        pltpu.VMEM((N_HEADS // 2, seq, 2 * Q_BLOCK), F32),
                        pltpu.VMEM((2, seq, 2 * Q_BLOCK), BF16), pltpu.VMEM((qw, Q_BLOCK), F32),
                        pltpu.VMEM((N_HEADS // 2, 2 * Q_BLOCK, LANES), BF16),
                        pltpu.VMEM((N_HEADS // 2, 2 * Q_BLOCK, LANES), BF16)],
        compiler_params=pltpu.CompilerParams(
            dimension_semantics=("arbitrary", "arbitrary"), vmem_limit_bytes=VMEM_LIMIT_BYTES),
        name="attn_prompt",
    )(qi, wt, qb, kid, kd, vt)


def _page_loop(n_pages, fn):
    def body(pg, carry):
        fn(pg)
        return carry
    lax.fori_loop(0, n_pages, body, 0)


def _page_cols(pg, page):
    return pl.ds(pl.multiple_of(pg * page, page), page)


def _sample_chunks(past, chunk):
    return [(c0, min(chunk, past - c0)) for c0 in range(0, past, chunk)] + [(past, LANES)]


def _idx_scores_kernel(pt_ref, qi_ref, wc_ref, kin_ref, cidx_ref, sc_ref, idx_buf, sem, *, n_pages, page,
                       dec_seq, chunk):
    b = pl.program_id(0)
    nb = pl.num_programs(0)
    slot = b % 2
    past = n_pages * page
    n_keys = past + LANES

    def idx_copy(bb, sl, pg):
        return pltpu.make_async_copy(cidx_ref.at[pt_ref[bb, pg]],
                                     idx_buf.at[sl, :, _page_cols(pg, page)], sem.at[sl])

    @pl.when(b == 0)
    def _():
        _page_loop(n_pages, lambda pg: idx_copy(0, 0, pg).start())

    @pl.when(b + 1 < nb)
    def _():
        _page_loop(n_pages, lambda pg: idx_copy(b + 1, 1 - slot, pg).start())

    idx_buf[slot, :, past:n_keys] = kin_ref[...]
    all_pages = idx_buf.at[slot, :, 0:past]
    pltpu.make_async_copy(all_pages, all_pages, sem.at[slot]).wait()

    qi = qi_ref[...]
    wc = wc_ref[...]
    for c0, n in _sample_chunks(past, chunk):
        d = jnp.dot(qi, idx_buf[slot, :, c0:c0 + n].astype(BF16), preferred_element_type=F32)
        sh = jnp.maximum(d, 0.0) * wc
        score = sh[0:SUBLANES, :]
        for hh in range(1, N_IDX_HEADS):
            score = score + sh[hh * SUBLANES:(hh + 1) * SUBLANES, :]
        tslot = lax.broadcasted_iota(I32, (SUBLANES, n), 0)
        col = c0 + lax.broadcasted_iota(I32, (SUBLANES, n), 1)
        score = jnp.where(col <= past + tslot, score, NEG_INF)
        sc_ref[:, c0:c0 + n] = score[0:dec_seq, :]


def _idx_scores_call(page_table, qi_rows, wcol, ki_new, cache_idx, dec_seq):
    dec_batch, n_pages = page_table.shape
    page = cache_idx.shape[2]
    past = n_pages * page
    n_keys = past + LANES
    rows = N_IDX_HEADS * SUBLANES
    per_b = lambda r, w: pl.BlockSpec((None, r, w), lambda b, pt: (b, 0, 0))
    grid_spec = pltpu.PrefetchScalarGridSpec(
        num_scalar_prefetch=1,
        grid=(dec_batch,),
        in_specs=[per_b(rows, D_IDX), per_b(rows, 1), per_b(D_IDX, LANES),
                  pl.BlockSpec(memory_space=pl.ANY)],
        out_specs=per_b(dec_seq, n_keys),
        scratch_shapes=[pltpu.VMEM((2, D_IDX, n_keys), F32), pltpu.SemaphoreType.DMA((2,))],
    )
    return pl.pallas_call(
        functools.partial(_idx_scores_kernel, n_pages=n_pages, page=page, dec_seq=dec_seq,
                          chunk=min(4096, past)),
        out_shape=jax.ShapeDtypeStruct((dec_batch, dec_seq, n_keys), F32),
        grid_spec=grid_spec,
        compiler_params=pltpu.CompilerParams(
            dimension_semantics=("arbitrary",), vmem_limit_bytes=VMEM_LIMIT_BYTES),
        name="idx_scores_sample",
    )(page_table, qi_rows, wcol, ki_new, cache_idx)


def _select_kernel(sc_ref, bias_ref, *, k_sel):
    nq, n_keys = sc_ref.shape
    n_tiles = n_keys // LANES

    def count(pred):
        acc = jnp.zeros((nq, LANES), I32)
        for c in range(n_tiles):
            acc = acc + jnp.where(pred(sc_ref[:, c * LANES:(c + 1) * LANES], c * LANES), 1, 0)
        return jnp.sum(acc, axis=1, keepdims=True)

    count_ge = lambda thr_c: count(lambda x, c0: x >= thr_c)
    thr, n_ge = _select_threshold(*_radix_select(count_ge, k_sel, (nq, 1), n_keys))
    for c in range(n_tiles):
        sl = slice(c * LANES, (c + 1) * LANES)
        bias_ref[:, sl] = jnp.where(sc_ref[:, sl] >= thr, 0.0, NEG_INF)

    @pl.when(jnp.max(n_ge) > k_sel)
    def _():
        need = k_sel - count(lambda x, c0: x > thr)

        def count_eq_below(cand):
            def pred(x, c0):
                col = c0 + lax.broadcasted_iota(I32, (nq, LANES), 1)
                return jnp.where(x == thr, col, cand) < cand
            return count(pred)

        cut = _tie_cut(count_eq_below, need, (n_keys - 1).bit_length(), (nq, 1))
        for c in range(n_tiles):
            sl = slice(c * LANES, (c + 1) * LANES)
            x = sc_ref[:, sl]
            col = c * LANES + lax.broadcasted_iota(I32, (nq, LANES), 1)
            bias_ref[:, sl] = jnp.where(x == thr, jnp.where(col <= cut, 0.0, NEG_INF),
                                        jnp.where(x > thr, 0.0, NEG_INF))


def _select_call(scores, k_sel):
    return pl.pallas_call(
        functools.partial(_select_kernel, k_sel=k_sel),
        out_shape=jax.ShapeDtypeStruct(scores.shape, F32),
        compiler_params=pltpu.CompilerParams(vmem_limit_bytes=VMEM_LIMIT_BYTES),
        name="select_sample",
    )(scores)


def _attn_sample_kernel(pt_ref, qp_ref, kn_ref, vn_ref, bias_ref, ck_ref, cv_ref, o_ref, k_buf, v_buf,
                        lg_ref, sem, *, n_pages, page, dec_seq, chunk):
    b = pl.program_id(0)
    nb = pl.num_programs(0)
    slot = b % 2
    past = n_pages * page
    n_keys = past + LANES
    rows = N_HEADS * SUBLANES

    def kv_copy(src, dst, bb, sl, s_id, pg):
        return pltpu.make_async_copy(src.at[pt_ref[bb, pg]], dst.at[sl, :, _page_cols(pg, page)],
                                     sem.at[s_id])

    def start_all(bb, sl):
        _page_loop(n_pages, lambda pg: kv_copy(ck_ref, k_buf, bb, sl, sl, pg).start())
        _page_loop(n_pages, lambda pg: kv_copy(cv_ref, v_buf, bb, sl, 2 + sl, pg).start())

    @pl.when(b == 0)
    def _():
        start_all(0, 0)

    @pl.when(b + 1 < nb)
    def _():
        start_all(b + 1, 1 - slot)

    k_buf[slot, :, past:n_keys] = kn_ref[...]
    v_buf[slot, :, past:n_keys] = vn_ref[...]
    for buf, s_id in ((k_buf, slot), (v_buf, 2 + slot)):
        all_pages = buf.at[slot, :, 0:past]
        pltpu.make_async_copy(all_pages, all_pages, sem.at[s_id]).wait()

    qp = qp_ref[...]
    chunks = _sample_chunks(past, chunk)
    for c0, n in chunks:
        bias = jnp.concatenate([bias_ref[:, c0:c0 + n], jnp.zeros((SUBLANES - dec_seq, n), F32)], axis=0)
        bias = jnp.concatenate([bias] * N_HEADS, axis=0)
        lg_ref[:, c0:c0 + n] = jnp.dot(qp, k_buf[slot, :, c0:c0 + n].astype(BF16),
                                       preferred_element_type=F32) + bias
    m = jnp.max(lg_ref[...], axis=1, keepdims=True)
    den = jnp.zeros((rows, 1), F32)
    acc = jnp.zeros((rows, LANES), F32)
    for c0, n in chunks:
        e = jnp.exp(lg_ref[:, c0:c0 + n] - m)
        den = den + jnp.sum(e, axis=1, keepdims=True)
        acc = acc + lax.dot_general(e.astype(BF16), v_buf[slot, :, c0:c0 + n].astype(BF16), _NT,
                                    preferred_element_type=F32)
    o_ref[...] = acc / den


def _attn_sample_call(page_table, qp_rows, k_new, v_new, bias, cache_k, cache_v, dec_seq):
    dec_batch, n_pages = page_table.shape
    page = cache_k.shape[2]
    past = n_pages * page
    n_keys = past + LANES
    rows = N_HEADS * SUBLANES
    per_b = lambda r, w: pl.BlockSpec((None, r, w), lambda b, pt: (b, 0, 0))
    any_spec = pl.BlockSpec(memory_space=pl.ANY)
    grid_spec = pltpu.PrefetchScalarGridSpec(
        num_scalar_prefetch=1,
        grid=(dec_batch,),
        in_specs=[per_b(rows, LANES), per_b(LANES, LANES), per_b(LANES, LANES), per_b(dec_seq, n_keys),
                  any_spec, any_spec],
        out_specs=per_b(rows, LANES),
        scratch_shapes=[pltpu.VMEM((2, LANES, n_keys), F32), pltpu.VMEM((2, LANES, n_keys), F32),
                        pltpu.VMEM((rows, n_keys), F32), pltpu.SemaphoreType.DMA((4,))],
    )
    return pl.pallas_call(
        functools.partial(_attn_sample_kernel, n_pages=n_pages, page=page, dec_seq=dec_seq,
                          chunk=min(4096, past)),
        out_shape=jax.ShapeDtypeStruct((dec_batch, rows, LANES), F32),
        grid_spec=grid_spec,
        compiler_params=pltpu.CompilerParams(
            dimension_semantics=("arbitrary",), vmem_limit_bytes=VMEM_LIMIT_BYTES),
        name="attn_sample",
    )(page_table, qp_rows, k_new, v_new, bias, cache_k, cache_v)


def _rope_tables(pos):
    half = ROT_DIM // 2
    inv = jnp.power(jnp.float32(ROPE_THETA), -jnp.arange(half, dtype=F32) * (2.0 / ROT_DIM))
    ang = pos.astype(F32)[:, None] * inv[None, :]
    cos, sin = jnp.cos(ang), jnp.sin(ang)
    n = pos.shape[0]
    rest = HEAD_DIM - ROT_DIM
    zh = jnp.zeros((n, half), F32)
    c = jnp.concatenate([cos, cos, jnp.ones((n, rest), F32)], axis=-1)
    sa = jnp.concatenate([-sin, zh, jnp.zeros((n, rest), F32)], axis=-1)
    sb = jnp.concatenate([zh, sin, jnp.zeros((n, rest), F32)], axis=-1)
    two = lambda a: jnp.concatenate([a, a], axis=-1)
    return two(c), two(sa), two(sb)


def kernel(x_prompt, x_sample, cache_k, cache_v, cache_idx_k, cache_conv, page_table, ffn1_norm,
           ffn1_w_gate, ffn1_w_up, ffn1_w_down, mix_norm, w_in, conv_w, conv_b, q_norm, k_norm, w_out,
           ffn2_norm, ffn2_w_gate, ffn2_w_up, ffn2_w_down):
    assert IDX_ROT == ROT_DIM and D_IDX == HEAD_DIM and 2 * HEAD_DIM == LANES
    batch, seq, d = x_prompt.shape
    dec_batch, dec_seq, _ = x_sample.shape
    depth, n_pool, page = cache_k.shape[:3]
    n_pages = page_table.shape[1]
    past = n_pages * page
    d_conv = conv_w.shape[-1]
    assert seq & (seq - 1) == 0 and dec_seq & (dec_seq - 1) == 0 and seq % Q_BLOCK == 0
    assert dec_seq <= SUBLANES and dec_seq >= CONV_W - 1 and page == LANES
    n_s = dec_batch * dec_seq
    tm_p = min(512, seq)

    tabs_p = _rope_tables(jnp.arange(seq, dtype=I32))
    tabs_s = _rope_tables(past + (jnp.arange(n_s, dtype=I32) % dec_seq))
    seg = jnp.kron(jnp.eye(LANES // HEAD_DIM, dtype=F32),
                   jnp.full((HEAD_DIM, HEAD_DIM), 1.0 / HEAD_DIM, F32)).astype(BF16)
    two = lambda a: jnp.concatenate([a, a], axis=-1)[None, :]

    xp = x_prompt.reshape(batch * seq, d)
    xs = x_sample.reshape(n_s, d)
    outs_p, outs_s = [], []
    for l in range(depth):
        wg1, wu1, wd1 = (w[l].astype(BF16) for w in (ffn1_w_gate, ffn1_w_up, ffn1_w_down))
        wg2, wu2, wd2 = (w[l].astype(BF16) for w in (ffn2_w_gate, ffn2_w_up, ffn2_w_down))
        n_in = w_in.shape[-1]
        n_proj = -(-(n_in - N_IDX_HEADS - D_IDX + LANES) // LANES) * LANES
        win = jnp.pad(w_in[l], ((0, 0), (0, n_proj - n_in))).astype(BF16)
        wo = w_out[l].astype(BF16)
        g1, gm, g2 = ffn1_norm[l][None, :], mix_norm[l][None, :], ffn2_norm[l][None, :]
        qn, kn = two(q_norm[l]), two(k_norm[l])
        cw, cb = conv_w[l], conv_b[l][None, :]

        x1 = _ffn_call(xp, g1, wg1, wu1, wd1, tm_p)
        (yc, qb, kd, vtb, kid, qi, wt, u_tail, k_t, v_t, ki_t) = _mix_call(
            x1, gm, win, cw, cb, qn, kn, seg, tabs_p, tm_p, seq)
        attn = _attn_prompt_call(qi, wt, qb, kid, kd, vtb, batch, seq, min(K_TOP, seq // 4))
        xp = _ffn_call(x1, g2, wg2, wu2, wd2, tm_p, mix=(yc, attn, wo))
        tiles_per_seq = seq // tm_p
        conv_p = u_tail.reshape(batch, tiles_per_seq, SUBLANES, d_conv)[:, -1, SUBLANES - (CONV_W - 1):]
        to_cache = lambda a: a.reshape(batch, N_KV, HEAD_DIM, seq).transpose(0, 3, 1, 2)
        outs_p.append((to_cache(k_t), to_cache(v_t), ki_t.transpose(0, 2, 1), conv_p))

        x1s = _ffn_call(xs, g1, wg1, wu1, wd1, n_s)
        prev = cache_conv[l]
        zrow = jnp.zeros((dec_batch, 1, d_conv), F32)
        pad = jnp.zeros((dec_batch, dec_seq - 2, d_conv), F32)
        e1 = jnp.concatenate([prev[:, 1:2], zrow, pad], axis=1).reshape(n_s, d_conv)
        e2 = jnp.concatenate([prev[:, 0:1], prev[:, 1:2], pad], axis=1).reshape(n_s, d_conv)
        (ycs, qbs, _, _, _, qis, wts, us, k_ts, v_ts, ki_ts) = _mix_call(
            x1s, gm, win, cw, cb, qn, kn, seg, tabs_s, n_s, dec_seq, prev=(e1, e2))

        def head_rows(a, width):
            a = a.reshape(dec_batch, dec_seq, N_HEADS, width).transpose(0, 2, 1, 3)
            a = jnp.pad(a, ((0, 0), (0, 0), (0, SUBLANES - dec_seq), (0, 0)))
            return a.reshape(dec_batch, N_HEADS * SUBLANES, width)

        def new_tile(a_t):
            r = a_t.shape[1]
            a = a_t[0].reshape(r, dec_batch, dec_seq).transpose(1, 0, 2)
            return jnp.pad(a, ((0, 0), (0, 0), (0, LANES - dec_seq)))

        qi_rows = head_rows(qis, D_IDX)
        q_rows = head_rows(qbs, HEAD_DIM)
        rep = N_HEADS // N_KV
        grp = (jnp.arange(N_HEADS * SUBLANES) // SUBLANES // rep)[None, :, None]
        zq = jnp.zeros_like(q_rows)
        qp_rows = jnp.concatenate([jnp.where(grp == 0, q_rows, zq), jnp.where(grp == 1, q_rows, zq)],
                                  axis=-1)
        wcol = head_rows(wts.T.reshape(n_s, N_IDX_HEADS), 1)
        ck = cache_k[l].transpose(0, 2, 3, 1).reshape(n_pool, LANES, page)
        cv = cache_v[l].transpose(0, 2, 3, 1).reshape(n_pool, LANES, page)
        ci = cache_idx_k[l].transpose(0, 2, 1)
        k_sel_s = min(K_TOP, (past + dec_seq) // 4)
        scores_s = _idx_scores_call(page_table, qi_rows, wcol, new_tile(ki_ts), ci, dec_seq)
        bias_s = _select_call(scores_s.reshape(n_s, past + LANES), k_sel_s)
        o_s = _attn_sample_call(page_table, qp_rows, new_tile(k_ts), new_tile(v_ts),
                                bias_s.reshape(dec_batch, dec_seq, past + LANES), ck, cv, dec_seq)
        o_s = o_s.reshape(dec_batch, N_HEADS, SUBLANES, N_KV, HEAD_DIM)[:, :, :dec_seq]
        hsel = (jnp.arange(N_HEADS) // rep)[None, :, None, None]
        attn_s = jnp.where(hsel == 0, o_s[:, :, :, 0], o_s[:, :, :, 1])
        attn_s = attn_s.transpose(0, 2, 1, 3).reshape(n_s, N_HEADS * HEAD_DIM).astype(BF16)
        xs = _ffn_call(x1s, g2, wg2, wu2, wd2, n_s, mix=(ycs, attn_s, wo))
        conv_s = us.reshape(dec_batch, dec_seq, d_conv)[:, dec_seq - (CONV_W - 1):]
        rows_s = lambda a_t: a_t[0].T.reshape(dec_batch, dec_seq, -1)
        outs_s.append((rows_s(k_ts).reshape(dec_batch, dec_seq, N_KV, HEAD_DIM),
                       rows_s(v_ts).reshape(dec_batch, dec_seq, N_KV, HEAD_DIM),
                       rows_s(ki_ts), conv_s))

    stack = lambda outs, n: jnp.stack([o[n] for o in outs])
    return (xp.reshape(batch, seq, d), xs.reshape(dec_batch, dec_seq, d),
            stack(outs_p, 0), stack(outs_p, 1), stack(outs_p, 2), stack(outs_p, 3),
            stack(outs_s, 0), stack(outs_s, 1), stack(outs_s, 2), stack(outs_s, 3))
```

```python
import functools

import jax
import jax.numpy as jnp
from jax import lax
from jax.experimental import pallas as pl
from jax.experimental.pallas import tpu as pltpu

F32 = jnp.float32
BF16 = jnp.bfloat16
I32 = jnp.int32

N_HEADS = 8
HEAD_DIM = 64
N_KV = 2
ROT_DIM = HEAD_DIM // 4
ROPE_THETA = 500000.0
N_IDX_HEADS = 8
D_IDX = 64
IDX_ROT = D_IDX // 4
K_TOP = 256
CONV_W = 3
EPS = 1e-6
IDX_SCALE = D_IDX ** -0.5 * N_IDX_HEADS ** -0.5
ATTN_SCALE = HEAD_DIM ** -0.5

LANES = 128
SUBLANES = 8
Q_BLOCK = 128
SCORE_ROWS = 128
COUNT_ROWS = 64
ATTN_ROWS = 256
VMEM_LIMIT_BYTES = 56 * 1024 * 1024

INT_MIN = -2 ** 31
RADIX_BITS = 32
RADIX_UNROLL = 4
RADIX_UNROLL_ALL_BELOW = 768
LOWEST_FINITE_KEY = INT_MIN + 2 ** 23
NEG_INF = float("-inf")
_NT = (((1,), (1,)), ((), ()))


def _rmsnorm(x, g):
    y = x * lax.rsqrt(jnp.mean(x * x, axis=-1, keepdims=True) + EPS)
    return y * g


def _swiglu_residual(x, g_ref, wg_ref, wu_ref, wd_ref):
    h = _rmsnorm(x, g_ref[...]).astype(BF16)
    gate = jnp.dot(h, wg_ref[...], preferred_element_type=F32)
    up = jnp.dot(h, wu_ref[...], preferred_element_type=F32)
    act = (gate * jax.nn.sigmoid(gate) * up).astype(BF16)
    return x + 0.5 * jnp.dot(act, wd_ref[...], preferred_element_type=F32)


def _ffn_kernel(x_ref, g_ref, wg_ref, wu_ref, wd_ref, o_ref):
    o_ref[...] = _swiglu_residual(x_ref[...], g_ref, wg_ref, wu_ref, wd_ref)


def _out_ffn_kernel(x_ref, yc_ref, at_ref, wo_ref, g_ref, wg_ref, wu_ref, wd_ref, o_ref):
    mixed = jnp.concatenate([yc_ref[...], at_ref[...]], axis=-1)
    x = x_ref[...] + jnp.dot(mixed, wo_ref[...], preferred_element_type=F32)
    o_ref[...] = _swiglu_residual(x, g_ref, wg_ref, wu_ref, wd_ref)


def _const_spec(shape):
    return pl.BlockSpec(shape, lambda i: (0,) * len(shape))


def _ffn_call(x, g, wg, wu, wd, tm, mix=None):
    n_tok, d = x.shape
    d_ff = wg.shape[1]
    tok = lambda w: pl.BlockSpec((tm, w), lambda i: (i, 0))
    w_specs = [_const_spec((1, d)), _const_spec((d, d_ff)), _const_spec((d, d_ff)), _const_spec((d_ff, d))]
    if mix is None:
        body, args, specs = _ffn_kernel, (x, g, wg, wu, wd), [tok(d)] + w_specs
    else:
        yc, at, wo = mix
        body, args = _out_ffn_kernel, (x, yc, at, wo, g, wg, wu, wd)
        specs = [tok(d), tok(yc.shape[1]), tok(at.shape[1]), _const_spec(wo.shape)] + w_specs
    return pl.pallas_call(
        body,
        out_shape=jax.ShapeDtypeStruct((n_tok, d), F32),
        grid=(n_tok // tm,),
        in_specs=specs,
        out_specs=tok(d),
        compiler_params=pltpu.CompilerParams(
            dimension_semantics=("arbitrary",), vmem_limit_bytes=VMEM_LIMIT_BYTES),
        name="ffn" if mix is None else "out_ffn",
    )(*args)


def _seg_mean(x2, seg):
    hi = x2.astype(BF16)
    lo = (x2 - hi.astype(F32)).astype(BF16)
    return (jnp.dot(hi, seg, preferred_element_type=F32)
            + jnp.dot(lo, seg, preferred_element_type=F32))


def _rope(x, c, sa, sb):
    half = ROT_DIM // 2
    n = x.shape[-1]
    return x * c + pltpu.roll(x, n - half, 1) * sa + pltpu.roll(x, half, 1) * sb


def _dup_halves(x, lane):
    r = pltpu.roll(x, LANES // 2, 1)
    low = lane < LANES // 2
    return jnp.where(low, x, r), jnp.where(low, r, x)


def _mix_kernel(*refs, tm, seq_len, has_prev, d_conv, pipelined):
    it = iter(refs)
    x_ref, g_ref, win_ref, cw_ref, cb_ref, qn_ref, kn_ref, seg_ref, cos_ref, sa_ref, sb_ref = (
        next(it) for _ in range(11))
    if has_prev:
        e1_ref, e2_ref = next(it), next(it)
    (yc_ref, qb_ref, kd_ref, vtb_ref, kid_ref, qi_ref, wt_ref, u_ref, kt_ref, vt_ref, kit_ref,
     uext_ref) = (next(it) for _ in range(12))
    step = pl.program_id(0)

    def project():
        h = _rmsnorm(x_ref[...], g_ref[...]).astype(BF16)
        return jnp.dot(h, win_ref[...], preferred_element_type=F32)

    post = functools.partial(
        _mix_post, cw_ref=cw_ref, cb_ref=cb_ref, qn_ref=qn_ref, kn_ref=kn_ref, seg_ref=seg_ref,
        cos_ref=cos_ref, sa_ref=sa_ref, sb_ref=sb_ref, prev_refs=(e1_ref, e2_ref) if has_prev else None,
        out_refs=(yc_ref, qb_ref, kd_ref, vtb_ref, kid_ref, qi_ref, wt_ref, u_ref, kt_ref, vt_ref, kit_ref),
        uext_ref=uext_ref, tm=tm, seq_len=seq_len, d_conv=d_conv)

    @pl.when(step == 0)
    def _():
        uext_ref[0:SUBLANES, :] = jnp.zeros((SUBLANES, d_conv), F32)

    if not pipelined:
        post(project(), step)
        return

    proj_a, proj_b = next(it), next(it)

    @pl.when(step == 0)
    def _():
        proj_b[...] = jnp.zeros(proj_b.shape, F32)

    def both(dst, src):
        dst[...] = project()
        post(src, step - 1)

    pl.when(step % 2 == 0)(functools.partial(both, proj_a, proj_b))
    pl.when(step % 2 == 1)(functools.partial(both, proj_b, proj_a))


def _mix_post(proj, i, *, cw_ref, cb_ref, qn_ref, kn_ref, seg_ref, cos_ref, sa_ref, sb_ref, prev_refs,
              out_refs, uext_ref, tm, seq_len, d_conv):
    (yc_ref, qb_ref, kd_ref, vtb_ref, kid_ref, qi_ref, wt_ref, u_ref, kt_ref, vt_ref, kit_ref) = out_refs
    has_prev = prev_refs is not None
    if has_prev:
        e1_ref, e2_ref = prev_refs
    o_q = 3 * d_conv
    o_k = o_q + N_HEADS * HEAD_DIM
    o_v = o_k + N_KV * HEAD_DIM
    o_qi = o_v + N_KV * HEAD_DIM
    o_ki = o_qi + N_IDX_HEADS * D_IDX

    gb = proj[:, 0:d_conv]
    u = proj[:, d_conv:2 * d_conv] * proj[:, 2 * d_conv:3 * d_conv]

    uext_ref[SUBLANES:SUBLANES + tm, :] = u
    u1 = uext_ref[SUBLANES - 1:SUBLANES - 1 + tm, :]
    u2 = uext_ref[SUBLANES - 2:SUBLANES - 2 + tm, :]
    t = (i * tm + lax.broadcasted_iota(I32, (tm, d_conv), 0)) & (seq_len - 1)
    p1 = e1_ref[...] if has_prev else 0.0
    p2 = e2_ref[...] if has_prev else 0.0
    u1 = jnp.where(t >= 1, u1, p1)
    u2 = jnp.where(t >= 2, u2, p2)
    cw = cw_ref[...]
    conv = cb_ref[...] + cw[0:1, :] * u2
    conv = conv + cw[1:2, :] * u1
    conv = conv + cw[2:3, :] * u
    yc_ref[...] = (gb * conv).astype(BF16)
    tail = u[tm - SUBLANES:tm, :]
    uext_ref[0:SUBLANES, :] = tail
    u_ref[...] = u if has_prev else tail

    c, sa, sb = cos_ref[...], sa_ref[...], sb_ref[...]
    seg = seg_ref[...]
    lane = lax.broadcasted_iota(I32, (tm, LANES), 1)

    def qk_norm_rope(xt, gain):
        ms = _seg_mean(xt * xt, seg)
        return _rope(xt * lax.rsqrt(ms + EPS) * gain, c, sa, sb)

    for j in range(N_HEADS * HEAD_DIM // LANES):
        sl = slice(j * LANES, (j + 1) * LANES)
        qt = qk_norm_rope(proj[:, o_q + j * LANES:o_q + (j + 1) * LANES], qn_ref[...])
        qb_ref[:, sl] = (qt * ATTN_SCALE).astype(BF16)
        qit = _rope(proj[:, o_qi + j * LANES:o_qi + (j + 1) * LANES], c, sa, sb)
        qi_ref[:, sl] = qit.astype(BF16)

    kt = qk_norm_rope(proj[:, o_k:o_v], kn_ref[...])
    kt_ref[...] = kt.T
    k0, k1 = _dup_halves(kt, lane)
    kd_ref[:, 0:LANES] = k0.astype(BF16)
    kd_ref[:, LANES:2 * LANES] = k1.astype(BF16)

    v_t = proj[:, o_v:o_qi].T
    vt_ref[...] = v_t
    vtb_ref[...] = v_t.astype(BF16)

    low = lane < D_IDX
    kiw = _rope(proj[:, o_ki:o_ki + LANES], jnp.where(low, c, 1.0), jnp.where(low, sa, 0.0),
                jnp.where(low, sb, 0.0))
    kid_ref[...] = _dup_halves(kiw, lane)[0].astype(BF16)
    kiw_t = kiw.T
    kit_ref[...] = kiw_t[0:D_IDX, :]
    wt_ref[...] = kiw_t[D_IDX:D_IDX + N_IDX_HEADS, :] * IDX_SCALE


def _mix_call(x, g, win, cw, cb, qn, kn, seg, tables, tm, seq_len, prev=None):
    n_tok, d = x.shape
    d_conv = cw.shape[1]
    n_proj = win.shape[1]
    tps = max(seq_len // tm, 1)
    span = tps * tm
    has_prev = prev is not None
    n_tiles = n_tok // tm
    pipelined = n_tiles > 1
    t_in = (lambda i: jnp.minimum(i, n_tiles - 1)) if pipelined else (lambda i: i)
    t_out = (lambda i: jnp.maximum(i - 1, 0)) if pipelined else (lambda i: i)
    tok = lambda w: pl.BlockSpec((tm, w), lambda i: (t_out(i), 0))
    tab = pl.BlockSpec((tm, LANES), lambda i: (t_out(i) % tps, 0))
    fmaj = lambda r: pl.BlockSpec((None, r, tm), lambda i: (t_out(i) // tps, 0, t_out(i) % tps))
    in_specs = [pl.BlockSpec((tm, d), lambda i: (t_in(i), 0)), _const_spec((1, d)), _const_spec((d, n_proj)), _const_spec(cw.shape),
                _const_spec((1, d_conv)), _const_spec((1, LANES)), _const_spec((1, LANES)),
                _const_spec((LANES, LANES)), tab, tab, tab]
    args = [x, g, win, cw, cb, qn, kn, seg, *tables]
    if has_prev:
        in_specs += [tok(d_conv), tok(d_conv)]
        args += list(prev)
    qw = N_HEADS * HEAD_DIM
    u_rows = tm if has_prev else SUBLANES
    out_shape = [
        jax.ShapeDtypeStruct((n_tok, d_conv), BF16),
        jax.ShapeDtypeStruct((n_tok, qw), BF16),
        jax.ShapeDtypeStruct((n_tok, 2 * LANES), BF16),
        jax.ShapeDtypeStruct((LANES, n_tok), BF16),
        jax.ShapeDtypeStruct((n_tok, LANES), BF16),
        jax.ShapeDtypeStruct((n_tok, N_IDX_HEADS * D_IDX), BF16),
        jax.ShapeDtypeStruct((N_IDX_HEADS, n_tok), F32),
        jax.ShapeDtypeStruct((n_tiles * u_rows, d_conv), F32),
        jax.ShapeDtypeStruct((n_tok // span, LANES, span), F32),
        jax.ShapeDtypeStruct((n_tok // span, LANES, span), F32),
        jax.ShapeDtypeStruct((n_tok // span, D_IDX, span), F32),
    ]
    out_specs = [tok(d_conv), tok(qw), tok(2 * LANES),
                 pl.BlockSpec((LANES, tm), lambda i: (0, t_out(i))), tok(LANES), tok(N_IDX_HEADS * D_IDX),
                 pl.BlockSpec((N_IDX_HEADS, tm), lambda i: (0, t_out(i))),
                 pl.BlockSpec((u_rows, d_conv), lambda i: (t_out(i), 0)),
                 fmaj(LANES), fmaj(LANES), fmaj(D_IDX)]
    return pl.pallas_call(
        functools.partial(_mix_kernel, tm=tm, seq_len=seq_len, has_prev=has_prev, d_conv=d_conv,
                          pipelined=pipelined),
        out_shape=out_shape,
        grid=(n_tiles + 1 if pipelined else n_tiles,),
        in_specs=in_specs,
        out_specs=out_specs,
        scratch_shapes=[pltpu.VMEM((tm + SUBLANES, d_conv), F32)]
        + [pltpu.VMEM((tm, n_proj), F32)] * (2 if pipelined else 0),
        compiler_params=pltpu.CompilerParams(
            dimension_semantics=("arbitrary",), vmem_limit_bytes=VMEM_LIMIT_BYTES),
        name="mix_sample" if has_prev else "mix_prompt",
    )(*args)


def _threshold_float(t):
    return pltpu.bitcast(jnp.where(t < 0, t ^ 0x7FFFFFFF, t), F32)


def _radix_select(count_ge, k_sel, shape, n_all):
    return lax.fori_loop(0, RADIX_BITS, _radix_step(count_ge, k_sel), _radix_init(shape, n_all))


def _radix_init(shape, n_all):
    return jnp.full(shape, INT_MIN, I32), jnp.full(shape, n_all, I32)


def _radix_step(count_ge, k_sel):
    def step(i, carry):
        t, n_t = carry
        cand = t + lax.shift_left(jnp.int32(1), RADIX_BITS - 1 - i)
        n_c = count_ge(_threshold_float(cand))
        take = n_c >= k_sel
        return jnp.where(take, cand, t), jnp.where(take, n_c, n_t)
    return step


def _select_threshold(t, n_t):
    thr = _threshold_float(jnp.maximum(t, LOWEST_FINITE_KEY))
    return thr, jnp.where(t == INT_MIN, 0, n_t)


def _tie_cut(count_eq_below, need, n_bits, shape):
    def body(i, j):
        cand = j + lax.shift_left(jnp.int32(1), n_bits - 1 - i)
        return jnp.where(count_eq_below(cand) < need, cand, j)
    return lax.fori_loop(0, n_bits, body, jnp.zeros(shape, I32))


def _split_heads_rhs(q_tile, lane):
    zero = jnp.zeros_like(q_tile)
    low = lane < LANES // 2
    return jnp.concatenate([jnp.where(low, q_tile, zero), jnp.where(low, zero, q_tile)], axis=0)


def _count_rows(ref, s, pred):
    acc = jnp.zeros((COUNT_ROWS, LANES), I32)
    for c in range(s // COUNT_ROWS):
        r0 = c * COUNT_ROWS
        acc = acc + jnp.where(pred(ref[r0:r0 + COUNT_ROWS, :], r0), 1, 0)
    return acc.sum(axis=0, keepdims=True)


def _attn_prompt_kernel(qi_ref, wt_ref, qb_ref, kid_ref, kd_ref, vt_ref, o_ref, score_ref, bias_ref,
                        lg_ref, e_ref, ot_ref, rhsi_ref, rhsq_ref, *, s_variants, k_sel, n_bits):
    j = pl.program_id(0)
    t0 = j * Q_BLOCK
    gran = s_variants[0]
    lane_q = lax.broadcasted_iota(I32, (Q_BLOCK, LANES), 1)
    n_pairs = N_HEADS // 2
    pairs_per_kv = (N_HEADS // N_KV) // 2
    for p in range(n_pairs):
        rhsi_ref[p] = _split_heads_rhs(qi_ref[:, p * LANES:(p + 1) * LANES], lane_q)
        rhsq_ref[p] = _split_heads_rhs(qb_ref[:, p * LANES:(p + 1) * LANES], lane_q)

    def process(s):
        w = wt_ref[...]
        for c in range(s // SCORE_ROWS):
            r0 = c * SCORE_ROWS
            kid_c = kid_ref[r0:r0 + SCORE_ROWS, :]
            score = None
            for p in range(n_pairs):
                d = lax.dot_general(kid_c, rhsi_ref[p], _NT, preferred_element_type=F32)
                term = (jnp.maximum(d[:, 0:LANES], 0.0) * w[2 * p:2 * p + 1, :]
                        + jnp.maximum(d[:, LANES:], 0.0) * w[2 * p + 1:2 * p + 2, :])
                score = term if score is None else score + term
            if r0 + SCORE_ROWS > s - gran:
                row = r0 + lax.broadcasted_iota(I32, (SCORE_ROWS, LANES), 0)
                tq = t0 + lax.broadcasted_iota(I32, (SCORE_ROWS, LANES), 1)
                score = jnp.where(row <= tq, score, NEG_INF)
            score_ref[r0:r0 + SCORE_ROWS, :] = score

        n_chunks = s // ATTN_ROWS
        steps_per_pair = RADIX_BITS // n_pairs
        unroll = steps_per_pair if s <= RADIX_UNROLL_ALL_BELOW else RADIX_UNROLL
        trips = steps_per_pair // unroll
        jobs_per_trip = -(-n_chunks // trips)
        n_full, n_rem = divmod(n_chunks, jobs_per_trip)
        count_ge = lambda thr_c: _count_rows(score_ref, s, lambda x, r0: x >= thr_c)
        radix_step = _radix_step(count_ge, k_sel)
        carry = _radix_init((1, LANES), s)
        for p in range(n_pairs):
            g = p // pairs_per_kv

            def raw_logits(c, p=p, g=g):
                rows = pl.ds(pl.multiple_of(c * ATTN_ROWS, ATTN_ROWS), ATTN_ROWS)
                lg_ref[p, rows, :] = lax.dot_general(kd_ref[rows, g * LANES:(g + 1) * LANES],
                                                     rhsq_ref[p], _NT, preferred_element_type=F32)

            def trip(k, carry, n_jobs, p=p, raw_logits=raw_logits):
                for jj in range(n_jobs):
                    raw_logits(k * jobs_per_trip + jj)
                for uu in range(unroll):
                    carry = radix_step(p * steps_per_pair + k * unroll + uu, carry)
                return carry

            carry = lax.fori_loop(0, n_full, functools.partial(trip, n_jobs=jobs_per_trip), carry)
            done = n_full
            if n_rem:
                carry = lax.fori_loop(done, done + 1, functools.partial(trip, n_jobs=n_rem), carry)
                done += 1
            carry = lax.fori_loop(done, trips, functools.partial(trip, n_jobs=0), carry)
        thr, n_ge = _select_threshold(*carry)
        bias_ref[0:s, :] = jnp.where(score_ref[0:s, :] >= thr, 0.0, NEG_INF)

        @pl.when(jnp.max(n_ge) > k_sel)
        def _():
            need = k_sel - _count_rows(score_ref, s, lambda x, r0: x > thr)

            def count_eq_below(cand):
                def pred(x, r0):
                    row = r0 + lax.broadcasted_iota(I32, (COUNT_ROWS, LANES), 0)
                    return jnp.where(x == thr, row, cand) < cand
                return _count_rows(score_ref, s, pred)

            cut = _tie_cut(count_eq_below, need, n_bits, (1, LANES))
            score = score_ref[0:s, :]
            row = lax.broadcasted_iota(I32, (s, LANES), 0)
            bias_ref[0:s, :] = jnp.where(score == thr, jnp.where(row <= cut, 0.0, NEG_INF),
                                         jnp.where(score > thr, 0.0, NEG_INF))

        fold = lambda x, op: functools.reduce(
            op, [x[r:r + 4 * SUBLANES, :] for r in range(0, ATTN_ROWS, 4 * SUBLANES)])

        def logits_chunk(p, c, mx):
            sl = slice(c * ATTN_ROWS, (c + 1) * ATTN_ROWS)
            b = bias_ref[sl, :]
            lg = lg_ref[p, sl, :] + jnp.concatenate([b, b], axis=1)
            lg_ref[p, sl, :] = lg
            part = fold(lg, jnp.maximum)
            return part if mx is None else jnp.maximum(mx, part)

        def softmax_chunk(p, c, m, den):
            sl = slice(c * ATTN_ROWS, (c + 1) * ATTN_ROWS)
            e = jnp.exp(lg_ref[p, sl, :] - m)
            e_ref[p % 2, sl, :] = e.astype(BF16)
            part = fold(e, jnp.add)
            return part if den is None else den + part

        mx = None
        for c in range(n_chunks):
            mx = logits_chunk(0, c, mx)
        for p in range(n_pairs):
            g = p // pairs_per_kv
            m = jnp.max(mx, axis=0, keepdims=True)
            mx, den = None, None
            for c in range(n_chunks):
                if p + 1 < n_pairs:
                    mx = logits_chunk(p + 1, c, mx)
                den = softmax_chunk(p, c, m, den)
            pv = jnp.dot(vt_ref[g * HEAD_DIM:(g + 1) * HEAD_DIM, 0:s], e_ref[p % 2, 0:s, :],
                         preferred_element_type=F32)
            pv = pv / jnp.sum(den, axis=0, keepdims=True)
            ot_ref[(2 * p) * HEAD_DIM:(2 * p + 1) * HEAD_DIM, :] = pv[:, 0:LANES]
            ot_ref[(2 * p + 1) * HEAD_DIM:(2 * p + 2) * HEAD_DIM, :] = pv[:, LANES:]
        o_ref[...] = ot_ref[...].T.astype(BF16)

    blocks_per_variant = gran // Q_BLOCK
    for vi, s in enumerate(s_variants):
        if len(s_variants) == 1:
            process(s)
        else:
            pl.when(j // blocks_per_variant == vi)(functools.partial(process, s))


def _attn_prompt_call(qi, wt, qb, kid, kd, vt, batch, seq, k_sel):
    n_tok = batch * seq
    nq = seq // Q_BLOCK
    n_var = max(v for v in (1, 2, 4, 8) if nq % v == 0 and seq // v % ATTN_ROWS == 0)
    s_variants = tuple(seq // n_var * (v + 1) for v in range(n_var))
    qw = qb.shape[1]
    qblk = lambda w: pl.BlockSpec((Q_BLOCK, w), lambda j, b: (b * nq + j, 0))
    per_b = lambda w: pl.BlockSpec((seq, w), lambda j, b: (b, 0))
    return pl.pallas_call(
        functools.partial(_attn_prompt_kernel, s_variants=s_variants, k_sel=k_sel,
                          n_bits=(seq - 1).bit_length()),
        out_shape=jax.ShapeDtypeStruct((n_tok, qw), BF16),
        grid=(nq, batch),
        in_specs=[qblk(qi.shape[1]),
                  pl.BlockSpec((N_IDX_HEADS, Q_BLOCK), lambda j, b: (0, b * nq + j)),
                  qblk(qw), per_b(LANES), per_b(2 * LANES),
                  pl.BlockSpec((LANES, seq), lambda j, b: (0, b))],
        out_specs=qblk(qw),
        scratch_shapes=[pltpu.VMEM((seq, LANES), F32), pltpu.VMEM((seq, LANES), F32),
                        pltpu.VMEM((N_HEADS // 2, seq, 2 * Q_BLOCK), F32),
                        pltpu.VMEM((2, seq, 2 * Q_BLOCK), BF16), pltpu.VMEM((qw, Q_BLOCK), F32),
                        pltpu.VMEM((N_HEADS // 2, 2 * Q_BLOCK, LANES), BF16),
                        pltpu.VMEM((N_HEADS // 2, 2 * Q_BLOCK, LANES), BF16)],
        compiler_params=pltpu.CompilerParams(
            dimension_semantics=("arbitrary", "arbitrary"), vmem_limit_bytes=VMEM_LIMIT_BYTES),
        name="attn_prompt",
    )(qi, wt, qb, kid, kd, vt)


def _page_loop(n_pages, fn):
    def body(pg, carry):
        fn(pg)
        return carry
    lax.fori_loop(0, n_pages, body, 0)


def _page_cols(pg, page):
    return pl.ds(pl.multiple_of(pg * page, page), page)


def _sample_chunks(past, chunk):
    return [(c0, min(chunk, past - c0)) for c0 in range(0, past, chunk)] + [(past, LANES)]


def _idx_scores_kernel(pt_ref, qi_ref, wc_ref, kin_ref, cidx_ref, sc_ref, idx_buf, sem, *, n_pages, page,
                       dec_seq, chunk):
    b = pl.program_id(0)
    nb = pl.num_programs(0)
    slot = b % 2
    past = n_pages * page
    n_keys = past + LANES

    def idx_copy(bb, sl, pg):
        return pltpu.make_async_copy(cidx_ref.at[pt_ref[bb, pg]],
                                     idx_buf.at[sl, :, _page_cols(pg, page)], sem.at[sl])

    @pl.when(b == 0)
    def _():
        _page_loop(n_pages, lambda pg: idx_copy(0, 0, pg).start())

    @pl.when(b + 1 < nb)
    def _():
        _page_loop(n_pages, lambda pg: idx_copy(b + 1, 1 - slot, pg).start())

    idx_buf[slot, :, past:n_keys] = kin_ref[...]
    all_pages = idx_buf.at[slot, :, 0:past]
    pltpu.make_async_copy(all_pages, all_pages, sem.at[slot]).wait()

    qi = qi_ref[...]
    wc = wc_ref[...]
    for c0, n in _sample_chunks(past, chunk):
        d = jnp.dot(qi, idx_buf[slot, :, c0:c0 + n].astype(BF16), preferred_element_type=F32)
        sh = jnp.maximum(d, 0.0) * wc
        score = sh[0:SUBLANES, :]
        for hh in range(1, N_IDX_HEADS):
            score = score + sh[hh * SUBLANES:(hh + 1) * SUBLANES, :]
        tslot = lax.broadcasted_iota(I32, (SUBLANES, n), 0)
        col = c0 + lax.broadcasted_iota(I32, (SUBLANES, n), 1)
        score = jnp.where(col <= past + tslot, score, NEG_INF)
        sc_ref[:, c0:c0 + n] = score[0:dec_seq, :]


def _idx_scores_call(page_table, qi_rows, wcol, ki_new, cache_idx, dec_seq):
    dec_batch, n_pages = page_table.shape
    page = cache_idx.shape[2]
    past = n_pages * page
    n_keys = past + LANES
    rows = N_IDX_HEADS * SUBLANES
    per_b = lambda r, w: pl.BlockSpec((None, r, w), lambda b, pt: (b, 0, 0))
    grid_spec = pltpu.PrefetchScalarGridSpec(
        num_scalar_prefetch=1,
        grid=(dec_batch,),
        in_specs=[per_b(rows, D_IDX), per_b(rows, 1), per_b(D_IDX, LANES),
                  pl.BlockSpec(memory_space=pl.ANY)],
        out_specs=per_b(dec_seq, n_keys),
        scratch_shapes=[pltpu.VMEM((2, D_IDX, n_keys), F32), pltpu.SemaphoreType.DMA((2,))],
    )
    return pl.pallas_call(
        functools.partial(_idx_scores_kernel, n_pages=n_pages, page=page, dec_seq=dec_seq,
                          chunk=min(4096, past)),
        out_shape=jax.ShapeDtypeStruct((dec_batch, dec_seq, n_keys), F32),
        grid_spec=grid_spec,
        compiler_params=pltpu.CompilerParams(
            dimension_semantics=("arbitrary",), vmem_limit_bytes=VMEM_LIMIT_BYTES),
        name="idx_scores_sample",
    )(page_table, qi_rows, wcol, ki_new, cache_idx)


def _select_kernel(sc_ref, bias_ref, *, k_sel):
    nq, n_keys = sc_ref.shape
    n_tiles = n_keys // LANES

    def count(pred):
        acc = jnp.zeros((nq, LANES), I32)
        for c in range(n_tiles):
            acc = acc + jnp.where(pred(sc_ref[:, c * LANES:(c + 1) * LANES], c * LANES), 1, 0)
        return jnp.sum(acc, axis=1, keepdims=True)

    count_ge = lambda thr_c: count(lambda x, c0: x >= thr_c)
    thr, n_ge = _select_threshold(*_radix_select(count_ge, k_sel, (nq, 1), n_keys))
    for c in range(n_tiles):
        sl = slice(c * LANES, (c + 1) * LANES)
        bias_ref[:, sl] = jnp.where(sc_ref[:, sl] >= thr, 0.0, NEG_INF)

    @pl.when(jnp.max(n_ge) > k_sel)
    def _():
        need = k_sel - count(lambda x, c0: x > thr)

        def count_eq_below(cand):
            def pred(x, c0):
                col = c0 + lax.broadcasted_iota(I32, (nq, LANES), 1)
                return jnp.where(x == thr, col, cand) < cand
            return count(pred)

        cut = _tie_cut(count_eq_below, need, (n_keys - 1).bit_length(), (nq, 1))
        for c in range(n_tiles):
            sl = slice(c * LANES, (c + 1) * LANES)
            x = sc_ref[:, sl]
            col = c * LANES + lax.broadcasted_iota(I32, (nq, LANES), 1)
            bias_ref[:, sl] = jnp.where(x == thr, jnp.where(col <= cut, 0.0, NEG_INF),
                                        jnp.where(x > thr, 0.0, NEG_INF))


def _select_call(scores, k_sel):
    return pl.pallas_call(
        functools.partial(_select_kernel, k_sel=k_sel),
        out_shape=jax.ShapeDtypeStruct(scores.shape, F32),
        compiler_params=pltpu.CompilerParams(vmem_limit_bytes=VMEM_LIMIT_BYTES),
        name="select_sample",
    )(scores)


def _attn_sample_kernel(pt_ref, qp_ref, kn_ref, vn_ref, bias_ref, ck_ref, cv_ref, o_ref, k_buf, v_buf,
                        lg_ref, sem, *, n_pages, page, dec_seq, chunk):
    b = pl.program_id(0)
    nb = pl.num_programs(0)
    slot = b % 2
    past = n_pages * page
    n_keys = past + LANES
    rows = N_HEADS * SUBLANES

    def kv_copy(src, dst, bb, sl, s_id, pg):
        return pltpu.make_async_copy(src.at[pt_ref[bb, pg]], dst.at[sl, :, _page_cols(pg, page)],
                                     sem.at[s_id])

    def start_all(bb, sl):
        _page_loop(n_pages, lambda pg: kv_copy(ck_ref, k_buf, bb, sl, sl, pg).start())
        _page_loop(n_pages, lambda pg: kv_copy(cv_ref, v_buf, bb, sl, 2 + sl, pg).start())

    @pl.when(b == 0)
    def _():
        start_all(0, 0)

    @pl.when(b + 1 < nb)
    def _():
        start_all(b + 1, 1 - slot)

    k_buf[slot, :, past:n_keys] = kn_ref[...]
    v_buf[slot, :, past:n_keys] = vn_ref[...]
    for buf, s_id in ((k_buf, slot), (v_buf, 2 + slot)):
        all_pages = buf.at[slot, :, 0:past]
        pltpu.make_async_copy(all_pages, all_pages, sem.at[s_id]).wait()

    qp = qp_ref[...]
    chunks = _sample_chunks(past, chunk)
    for c0, n in chunks:
        bias = jnp.concatenate([bias_ref[:, c0:c0 + n], jnp.zeros((SUBLANES - dec_seq, n), F32)], axis=0)
        bias = jnp.concatenate([bias] * N_HEADS, axis=0)
        lg_ref[:, c0:c0 + n] = jnp.dot(qp, k_buf[slot, :, c0:c0 + n].astype(BF16),
                                       preferred_element_type=F32) + bias
    m = jnp.max(lg_ref[...], axis=1, keepdims=True)
    den = jnp.zeros((rows, 1), F32)
    acc = jnp.zeros((rows, LANES), F32)
    for c0, n in chunks:
        e = jnp.exp(lg_ref[:, c0:c0 + n] - m)
        den = den + jnp.sum(e, axis=1, keepdims=True)
        acc = acc + lax.dot_general(e.astype(BF16), v_buf[slot, :, c0:c0 + n].astype(BF16), _NT,
                                    preferred_element_type=F32)
    o_ref[...] = acc / den


def _attn_sample_call(page_table, qp_rows, k_new, v_new, bias, cache_k, cache_v, dec_seq):
    dec_batch, n_pages = page_table.shape
    page = cache_k.shape[2]
    past = n_pages * page
    n_keys = past + LANES
    rows = N_HEADS * SUBLANES
    per_b = lambda r, w: pl.BlockSpec((None, r, w), lambda b, pt: (b, 0, 0))
    any_spec = pl.BlockSpec(memory_space=pl.ANY)
    grid_spec = pltpu.PrefetchScalarGridSpec(
        num_scalar_prefetch=1,
        grid=(dec_batch,),
        in_specs=[per_b(rows, LANES), per_b(LANES, LANES), per_b(LANES, LANES), per_b(dec_seq, n_keys),
                  any_spec, any_spec],
        out_specs=per_b(rows, LANES),
        scratch_shapes=[pltpu.VMEM((2, LANES, n_keys), F32), pltpu.VMEM((2, LANES, n_keys), F32),
                        pltpu.VMEM((rows, n_keys), F32), pltpu.SemaphoreType.DMA((4,))],
    )
    return pl.pallas_call(
        functools.partial(_attn_sample_kernel, n_pages=n_pages, page=page, dec_seq=dec_seq,
                          chunk=min(4096, past)),
        out_shape=jax.ShapeDtypeStruct((dec_batch, rows, LANES), F32),
        grid_spec=grid_spec,
        compiler_params=pltpu.CompilerParams(
            dimension_semantics=("arbitrary",), vmem_limit_bytes=VMEM_LIMIT_BYTES),
        name="attn_sample",
    )(page_table, qp_rows, k_new, v_new, bias, cache_k, cache_v)


def _rope_tables(pos):
    half = ROT_DIM // 2
    inv = jnp.power(jnp.float32(ROPE_THETA), -jnp.arange(half, dtype=F32) * (2.0 / ROT_DIM))
    ang = pos.astype(F32)[:, None] * inv[None, :]
    cos, sin = jnp.cos(ang), jnp.sin(ang)
    n = pos.shape[0]
    rest = HEAD_DIM - ROT_DIM
    zh = jnp.zeros((n, half), F32)
    c = jnp.concatenate([cos, cos, jnp.ones((n, rest), F32)], axis=-1)
    sa = jnp.concatenate([-sin, zh, jnp.zeros((n, rest), F32)], axis=-1)
    sb = jnp.concatenate([zh, sin, jnp.zeros((n, rest), F32)], axis=-1)
    two = lambda a: jnp.concatenate([a, a], axis=-1)
    return two(c), two(sa), two(sb)


def kernel(x_prompt, x_sample, cache_k, cache_v, cache_idx_k, cache_conv, page_table, ffn1_norm,
           ffn1_w_gate, ffn1_w_up, ffn1_w_down, mix_norm, w_in, conv_w, conv_b, q_norm, k_norm, w_out,
           ffn2_norm, ffn2_w_gate, ffn2_w_up, ffn2_w_down):
    assert IDX_ROT == ROT_DIM and D_IDX == HEAD_DIM and 2 * HEAD_DIM == LANES
    batch, seq, d = x_prompt.shape
    dec_batch, dec_seq, _ = x_sample.shape
    depth, n_pool, page = cache_k.shape[:3]
    n_pages = page_table.shape[1]
    past = n_pages * page
    d_conv = conv_w.shape[-1]
    assert seq & (seq - 1) == 0 and dec_seq & (dec_seq - 1) == 0 and seq % Q_BLOCK == 0
    assert dec_seq <= SUBLANES and dec_seq >= CONV_W - 1 and page == LANES
    n_s = dec_batch * dec_seq
    tm_p = min(512, seq)

    tabs_p = _rope_tables(jnp.arange(seq, dtype=I32))
    tabs_s = _rope_tables(past + (jnp.arange(n_s, dtype=I32) % dec_seq))
    seg = jnp.kron(jnp.eye(LANES // HEAD_DIM, dtype=F32),
                   jnp.full((HEAD_DIM, HEAD_DIM), 1.0 / HEAD_DIM, F32)).astype(BF16)
    two = lambda a: jnp.concatenate([a, a], axis=-1)[None, :]

    xp = x_prompt.reshape(batch * seq, d)
    xs = x_sample.reshape(n_s, d)
    outs_p, outs_s = [], []
    for l in range(depth):
        wg1, wu1, wd1 = (w[l].astype(BF16) for w in (ffn1_w_gate, ffn1_w_up, ffn1_w_down))
        wg2, wu2, wd2 = (w[l].astype(BF16) for w in (ffn2_w_gate, ffn2_w_up, ffn2_w_down))
        n_in = w_in.shape[-1]
        n_proj = -(-(n_in - N_IDX_HEADS - D_IDX + LANES) // LANES) * LANES
        win = jnp.pad(w_in[l], ((0, 0), (0, n_proj - n_in))).astype(BF16)
        wo = w_out[l].astype(BF16)
        g1, gm, g2 = ffn1_norm[l][None, :], mix_norm[l][None, :], ffn2_norm[l][None, :]
        qn, kn = two(q_norm[l]), two(k_norm[l])
        cw, cb = conv_w[l], conv_b[l][None, :]

        x1 = _ffn_call(xp, g1, wg1, wu1, wd1, tm_p)
        (yc, qb, kd, vtb, kid, qi, wt, u_tail, k_t, v_t, ki_t) = _mix_call(
            x1, gm, win, cw, cb, qn, kn, seg, tabs_p, tm_p, seq)
        attn = _attn_prompt_call(qi, wt, qb, kid, kd, vtb, batch, seq, min(K_TOP, seq // 4))
        xp = _ffn_call(x1, g2, wg2, wu2, wd2, tm_p, mix=(yc, attn, wo))
        tiles_per_seq = seq // tm_p
        conv_p = u_tail.reshape(batch, tiles_per_seq, SUBLANES, d_conv)[:, -1, SUBLANES - (CONV_W - 1):]
        to_cache = lambda a: a.reshape(batch, N_KV, HEAD_DIM, seq).transpose(0, 3, 1, 2)
        outs_p.append((to_cache(k_t), to_cache(v_t), ki_t.transpose(0, 2, 1), conv_p))

        x1s = _ffn_call(xs, g1, wg1, wu1, wd1, n_s)
        prev = cache_conv[l]
        zrow = jnp.zeros((dec_batch, 1, d_conv), F32)
        pad = jnp.zeros((dec_batch, dec_seq - 2, d_conv), F32)
        e1 = jnp.concatenate([prev[:, 1:2], zrow, pad], axis=1).reshape(n_s, d_conv)
        e2 = jnp.concatenate([prev[:, 0:1], prev[:, 1:2], pad], axis=1).reshape(n_s, d_conv)
        (ycs, qbs, _, _, _, qis, wts, us, k_ts, v_ts, ki_ts) = _mix_call(
            x1s, gm, win, cw, cb, qn, kn, seg, tabs_s, n_s, dec_seq, prev=(e1, e2))

        def head_rows(a, width):
            a = a.reshape(dec_batch, dec_seq, N_HEADS, width).transpose(0, 2, 1, 3)
            a = jnp.pad(a, ((0, 0), (0, 0), (0, SUBLANES - dec_seq), (0, 0)))
            return a.reshape(dec_batch, N_HEADS * SUBLANES, width)

        def new_tile(a_t):
            r = a_t.shape[1]
            a = a_t[0].reshape(r, dec_batch, dec_seq).transpose(1, 0, 2)
            return jnp.pad(a, ((0, 0), (0, 0), (0, LANES - dec_seq)))

        qi_rows = head_rows(qis, D_IDX)
        q_rows = head_rows(qbs, HEAD_DIM)
        rep = N_HEADS // N_KV
        grp = (jnp.arange(N_HEADS * SUBLANES) // SUBLANES // rep)[None, :, None]
        zq = jnp.zeros_like(q_rows)
        qp_rows = jnp.concatenate([jnp.where(grp == 0, q_rows, zq), jnp.where(grp == 1, q_rows, zq)],
                                  axis=-1)
        wcol = head_rows(wts.T.reshape(n_s, N_IDX_HEADS), 1)
        ck = cache_k[l].transpose(0, 2, 3, 1).reshape(n_pool, LANES, page)
        cv = cache_v[l].transpose(0, 2, 3, 1).reshape(n_pool, LANES, page)
        ci = cache_idx_k[l].transpose(0, 2, 1)
        k_sel_s = min(K_TOP, (past + dec_seq) // 4)
        scores_s = _idx_scores_call(page_table, qi_rows, wcol, new_tile(ki_ts), ci, dec_seq)
        bias_s = _select_call(scores_s.reshape(n_s, past + LANES), k_sel_s)
        o_s = _attn_sample_call(page_table, qp_rows, new_tile(k_ts), new_tile(v_ts),
                                bias_s.reshape(dec_batch, dec_seq, past + LANES), ck, cv, dec_seq)
        o_s = o_s.reshape(dec_batch, N_HEADS, SUBLANES, N_KV, HEAD_DIM)[:, :, :dec_seq]
        hsel = (jnp.arange(N_HEADS) // rep)[None, :, None, None]
        attn_s = jnp.where(hsel == 0, o_s[:, :, :, 0], o_s[:, :, :, 1])
        attn_s = attn_s.transpose(0, 2, 1, 3).reshape(n_s, N_HEADS * HEAD_DIM).astype(BF16)
        xs = _ffn_call(x1s, g2, wg2, wu2, wd2, n_s, mix=(ycs, attn_s, wo))
        conv_s = us.reshape(dec_batch, dec_seq, d_conv)[:, dec_seq - (CONV_W - 1):]
        rows_s = lambda a_t: a_t[0].T.reshape(dec_batch, dec_seq, -1)
        outs_s.append((rows_s(k_ts).reshape(dec_batch, dec_seq, N_KV, HEAD_DIM),
                       rows_s(v_ts).reshape(dec_batch, dec_seq, N_KV, HEAD_DIM),
                       rows_s(ki_ts), conv_s))

    stack = lambda outs, n: jnp.stack([o[n] for o in outs])
    return (xp.reshape(batch, seq, d), xs.reshape(dec_batch, dec_seq, d),
            stack(outs_p, 0), stack(outs_p, 1), stack(outs_p, 2), stack(outs_p, 3),
            stack(outs_s, 0), stack(outs_s, 1), stack(outs_s, 2), stack(outs_s, 3))
```

```python
import functools

import jax
import jax.numpy as jnp
import numpy as np
from jax import lax
from jax.experimental import pallas as pl
from jax.experimental.pallas import tpu as pltpu

F32 = jnp.float32
BF16 = jnp.bfloat16
I32 = jnp.int32

N_HEADS = 8
HEAD_DIM = 64
N_KV = 2
ROT_DIM = HEAD_DIM // 4
ROPE_THETA = 500000.0
N_IDX_HEADS = 8
D_IDX = 64
IDX_ROT = D_IDX // 4
K_TOP = 256
CONV_W = 3
EPS = 1e-6
IDX_SCALE = D_IDX ** -0.5 * N_IDX_HEADS ** -0.5
ATTN_SCALE = HEAD_DIM ** -0.5

LANES = 128
SUBLANES = 8
Q_BLOCK = 128
SCORE_ROWS = 128
COUNT_ROWS = 64
ATTN_ROWS = 256
VMEM_LIMIT_BYTES = 56 * 1024 * 1024

INT_MIN = -2 ** 31
RADIX_BITS = 32
PAGE_LOOP_UNROLL = 4
RADIX_UNROLL = 4
RADIX_UNROLL_ALL_BELOW = 768
LOWEST_FINITE_KEY = INT_MIN + 2 ** 23
NEG_INF = float("-inf")
_NT = (((1,), (1,)), ((), ()))


def _rmsnorm(x, g):
    y = x * lax.rsqrt(jnp.mean(x * x, axis=-1, keepdims=True) + EPS)
    return y * g


def _swiglu_residual(x, g_ref, wg_ref, wu_ref, wd_ref):
    h = _rmsnorm(x, g_ref[...]).astype(BF16)
    gate = jnp.dot(h, wg_ref[...], preferred_element_type=F32)
    up = jnp.dot(h, wu_ref[...], preferred_element_type=F32)
    act = (gate * jax.nn.sigmoid(gate) * up).astype(BF16)
    return x + 0.5 * jnp.dot(act, wd_ref[...], preferred_element_type=F32)


def _ffn_kernel(x_ref, g_ref, wg_ref, wu_ref, wd_ref, o_ref):
    o_ref[...] = _swiglu_residual(x_ref[...], g_ref, wg_ref, wu_ref, wd_ref)


def _out_ffn_kernel(x_ref, yc_ref, at_ref, wo_ref, g_ref, wg_ref, wu_ref, wd_ref, o_ref):
    mixed = jnp.concatenate([yc_ref[...], at_ref[...]], axis=-1)
    x = x_ref[...] + jnp.dot(mixed, wo_ref[...], preferred_element_type=F32)
    o_ref[...] = _swiglu_residual(x, g_ref, wg_ref, wu_ref, wd_ref)


def _const_spec(shape):
    return pl.BlockSpec(shape, lambda i: (0,) * len(shape))


def _ffn_call(x, g, wg, wu, wd, tm, mix=None):
    n_tok, d = x.shape
    d_ff = wg.shape[1]
    tok = lambda w: pl.BlockSpec((tm, w), lambda i: (i, 0))
    w_specs = [_const_spec((1, d)), _const_spec((d, d_ff)), _const_spec((d, d_ff)), _const_spec((d_ff, d))]
    if mix is None:
        body, args, specs = _ffn_kernel, (x, g, wg, wu, wd), [tok(d)] + w_specs
    else:
        yc, at, wo = mix
        body, args = _out_ffn_kernel, (x, yc, at, wo, g, wg, wu, wd)
        specs = [tok(d), tok(yc.shape[1]), tok(at.shape[1]), _const_spec(wo.shape)] + w_specs
    return pl.pallas_call(
        body,
        out_shape=jax.ShapeDtypeStruct((n_tok, d), F32),
        grid=(n_tok // tm,),
        in_specs=specs,
        out_specs=tok(d),
        compiler_params=pltpu.CompilerParams(
            dimension_semantics=("arbitrary",), vmem_limit_bytes=VMEM_LIMIT_BYTES),
        name="ffn" if mix is None else "out_ffn",
    )(*args)


def _seg_mean(x2, seg):
    hi = x2.astype(BF16)
    lo = (x2 - hi.astype(F32)).astype(BF16)
    return (jnp.dot(hi, seg, preferred_element_type=F32)
            + jnp.dot(lo, seg, preferred_element_type=F32))


def _rope(x, c, sa, sb):
    half = ROT_DIM // 2
    n = x.shape[-1]
    return x * c + pltpu.roll(x, n - half, 1) * sa + pltpu.roll(x, half, 1) * sb


def _dup_halves(x, lane):
    r = pltpu.roll(x, LANES // 2, 1)
    low = lane < LANES // 2
    return jnp.where(low, x, r), jnp.where(low, r, x)


def _mix_kernel(*refs, tm, seq_len, has_prev, d_conv, pipelined):
    it = iter(refs)
    x_ref, g_ref, win_ref, cw_ref, cb_ref, qn_ref, kn_ref, seg_ref, cos_ref, sa_ref, sb_ref = (
        next(it) for _ in range(11))
    if has_prev:
        e1_ref, e2_ref = next(it), next(it)
    (yc_ref, qb_ref, kd_ref, vtb_ref, kid_ref, qi_ref, wt_ref, u_ref, kt_ref, vt_ref, kit_ref,
     uext_ref) = (next(it) for _ in range(12))
    step = pl.program_id(0)

    def project():
        h = _rmsnorm(x_ref[...], g_ref[...]).astype(BF16)
        return jnp.dot(h, win_ref[...], preferred_element_type=F32)

    post = functools.partial(
        _mix_post, cw_ref=cw_ref, cb_ref=cb_ref, qn_ref=qn_ref, kn_ref=kn_ref, seg_ref=seg_ref,
        cos_ref=cos_ref, sa_ref=sa_ref, sb_ref=sb_ref, prev_refs=(e1_ref, e2_ref) if has_prev else None,
        out_refs=(yc_ref, qb_ref, kd_ref, vtb_ref, kid_ref, qi_ref, wt_ref, u_ref, kt_ref, vt_ref, kit_ref),
        uext_ref=uext_ref, tm=tm, seq_len=seq_len, d_conv=d_conv)

    @pl.when(step == 0)
    def _():
        uext_ref[0:SUBLANES, :] = jnp.zeros((SUBLANES, d_conv), F32)

    if not pipelined:
        post(project(), step)
        return

    proj_a, proj_b = next(it), next(it)

    @pl.when(step == 0)
    def _():
        proj_b[...] = jnp.zeros(proj_b.shape, F32)

    def both(dst, src):
        dst[...] = project()
        post(src, step - 1)

    pl.when(step % 2 == 0)(functools.partial(both, proj_a, proj_b))
    pl.when(step % 2 == 1)(functools.partial(both, proj_b, proj_a))


def _mix_post(proj, i, *, cw_ref, cb_ref, qn_ref, kn_ref, seg_ref, cos_ref, sa_ref, sb_ref, prev_refs,
              out_refs, uext_ref, tm, seq_len, d_conv):
    (yc_ref, qb_ref, kd_ref, vtb_ref, kid_ref, qi_ref, wt_ref, u_ref, kt_ref, vt_ref, kit_ref) = out_refs
    has_prev = prev_refs is not None
    if has_prev:
        e1_ref, e2_ref = prev_refs
    o_q = 3 * d_conv
    o_k = o_q + N_HEADS * HEAD_DIM
    o_v = o_k + N_KV * HEAD_DIM
    o_qi = o_v + N_KV * HEAD_DIM
    o_ki = o_qi + N_IDX_HEADS * D_IDX

    gb = proj[:, 0:d_conv]
    u = proj[:, d_conv:2 * d_conv] * proj[:, 2 * d_conv:3 * d_conv]

    uext_ref[SUBLANES:SUBLANES + tm, :] = u
    u1 = uext_ref[SUBLANES - 1:SUBLANES - 1 + tm, :]
    u2 = uext_ref[SUBLANES - 2:SUBLANES - 2 + tm, :]
    t = (i * tm + lax.broadcasted_iota(I32, (tm, d_conv), 0)) & (seq_len - 1)
    p1 = e1_ref[...] if has_prev else 0.0
    p2 = e2_ref[...] if has_prev else 0.0
    u1 = jnp.where(t >= 1, u1, p1)
    u2 = jnp.where(t >= 2, u2, p2)
    cw = cw_ref[...]
    conv = cb_ref[...] + cw[0:1, :] * u2
    conv = conv + cw[1:2, :] * u1
    conv = conv + cw[2:3, :] * u
    yc_ref[...] = (gb * conv).astype(BF16)
    tail = u[tm - SUBLANES:tm, :]
    uext_ref[0:SUBLANES, :] = tail
    u_ref[...] = u if has_prev else tail

    c, sa, sb = cos_ref[...], sa_ref[...], sb_ref[...]
    seg = seg_ref[...]
    lane = lax.broadcasted_iota(I32, (tm, LANES), 1)

    def qk_norm_rope(xt, gain):
        ms = _seg_mean(xt * xt, seg)
        return _rope(xt * lax.rsqrt(ms + EPS) * gain, c, sa, sb)

    for j in range(N_HEADS * HEAD_DIM // LANES):
        sl = slice(j * LANES, (j + 1) * LANES)
        qt = qk_norm_rope(proj[:, o_q + j * LANES:o_q + (j + 1) * LANES], qn_ref[...])
        qb_ref[:, sl] = (qt * ATTN_SCALE).astype(BF16)
        qit = _rope(proj[:, o_qi + j * LANES:o_qi + (j + 1) * LANES], c, sa, sb)
        qi_ref[:, sl] = qit.astype(BF16)

    kt = qk_norm_rope(proj[:, o_k:o_v], kn_ref[...])
    kt_ref[...] = kt.T
    k0, k1 = _dup_halves(kt, lane)
    kd_ref[:, 0:LANES] = k0.astype(BF16)
    kd_ref[:, LANES:2 * LANES] = k1.astype(BF16)

    v_t = proj[:, o_v:o_qi].T
    vt_ref[...] = v_t
    vtb_ref[...] = v_t.astype(BF16)

    low = lane < D_IDX
    kiw = _rope(proj[:, o_ki:o_ki + LANES], jnp.where(low, c, 1.0), jnp.where(low, sa, 0.0),
                jnp.where(low, sb, 0.0))
    kid_ref[...] = _dup_halves(kiw, lane)[0].astype(BF16)
    kiw_t = kiw.T
    kit_ref[...] = kiw_t[0:D_IDX, :]
    wt_ref[...] = kiw_t[D_IDX:D_IDX + N_IDX_HEADS, :] * IDX_SCALE


def _mix_call(x, g, win, cw, cb, qn, kn, seg, tables, tm, seq_len, prev=None):
    n_tok, d = x.shape
    d_conv = cw.shape[1]
    n_proj = win.shape[1]
    tps = max(seq_len // tm, 1)
    span = tps * tm
    has_prev = prev is not None
    n_tiles = n_tok // tm
    pipelined = n_tiles > 1
    t_in = (lambda i: jnp.minimum(i, n_tiles - 1)) if pipelined else (lambda i: i)
    t_out = (lambda i: jnp.maximum(i - 1, 0)) if pipelined else (lambda i: i)
    tok = lambda w: pl.BlockSpec((tm, w), lambda i: (t_out(i), 0))
    tab = pl.BlockSpec((tm, LANES), lambda i: (t_out(i) % tps, 0))
    fmaj = lambda r: pl.BlockSpec((None, r, tm), lambda i: (t_out(i) // tps, 0, t_out(i) % tps))
    in_specs = [pl.BlockSpec((tm, d), lambda i: (t_in(i), 0)), _const_spec((1, d)), _const_spec((d, n_proj)), _const_spec(cw.shape),
                _const_spec((1, d_conv)), _const_spec((1, LANES)), _const_spec((1, LANES)),
                _const_spec((LANES, LANES)), tab, tab, tab]
    args = [x, g, win, cw, cb, qn, kn, seg, *tables]
    if has_prev:
        in_specs += [tok(d_conv), tok(d_conv)]
        args += list(prev)
    qw = N_HEADS * HEAD_DIM
    u_rows = tm if has_prev else SUBLANES
    out_shape = [
        jax.ShapeDtypeStruct((n_tok, d_conv), BF16),
        jax.ShapeDtypeStruct((n_tok, qw), BF16),
        jax.ShapeDtypeStruct((n_tok, 2 * LANES), BF16),
        jax.ShapeDtypeStruct((LANES, n_tok), BF16),
        jax.ShapeDtypeStruct((n_tok, LANES), BF16),
        jax.ShapeDtypeStruct((n_tok, N_IDX_HEADS * D_IDX), BF16),
        jax.ShapeDtypeStruct((N_IDX_HEADS, n_tok), F32),
        jax.ShapeDtypeStruct((n_tiles * u_rows, d_conv), F32),
        jax.ShapeDtypeStruct((n_tok // span, LANES, span), F32),
        jax.ShapeDtypeStruct((n_tok // span, LANES, span), F32),
        jax.ShapeDtypeStruct((n_tok // span, D_IDX, span), F32),
    ]
    out_specs = [tok(d_conv), tok(qw), tok(2 * LANES),
                 pl.BlockSpec((LANES, tm), lambda i: (0, t_out(i))), tok(LANES), tok(N_IDX_HEADS * D_IDX),
                 pl.BlockSpec((N_IDX_HEADS, tm), lambda i: (0, t_out(i))),
                 pl.BlockSpec((u_rows, d_conv), lambda i: (t_out(i), 0)),
                 fmaj(LANES), fmaj(LANES), fmaj(D_IDX)]
    return pl.pallas_call(
        functools.partial(_mix_kernel, tm=tm, seq_len=seq_len, has_prev=has_prev, d_conv=d_conv,
                          pipelined=pipelined),
        out_shape=out_shape,
        grid=(n_tiles + 1 if pipelined else n_tiles,),
        in_specs=in_specs,
        out_specs=out_specs,
        scratch_shapes=[pltpu.VMEM((tm + SUBLANES, d_conv), F32)]
        + [pltpu.VMEM((tm, n_proj), F32)] * (2 if pipelined else 0),
        compiler_params=pltpu.CompilerParams(
            dimension_semantics=("arbitrary",), vmem_limit_bytes=VMEM_LIMIT_BYTES),
        name="mix_sample" if has_prev else "mix_prompt",
    )(*args)


def _threshold_float(t):
    return pltpu.bitcast(jnp.where(t < 0, t ^ 0x7FFFFFFF, t), F32)


def _radix_select(count_ge, k_sel, shape, n_all):
    return lax.fori_loop(0, RADIX_BITS, _radix_step(count_ge, k_sel), _radix_init(shape, n_all))


def _radix_init(shape, n_all):
    return jnp.full(shape, INT_MIN, I32), jnp.full(shape, n_all, I32)


def _radix_step(count_ge, k_sel):
    def step(i, carry):
        t, n_t = carry
        cand = t + lax.shift_left(jnp.int32(1), RADIX_BITS - 1 - i)
        n_c = count_ge(_threshold_float(cand))
        take = n_c >= k_sel
        return jnp.where(take, cand, t), jnp.where(take, n_c, n_t)
    return step


def _select_threshold(t, n_t):
    thr = _threshold_float(jnp.maximum(t, LOWEST_FINITE_KEY))
    return thr, jnp.where(t == INT_MIN, 0, n_t)


def _tie_cut(count_eq_below, need, n_bits, shape):
    def body(i, j):
        cand = j + lax.shift_left(jnp.int32(1), n_bits - 1 - i)
        return jnp.where(count_eq_below(cand) < need, cand, j)
    return lax.fori_loop(0, n_bits, body, jnp.zeros(shape, I32))


def _split_heads_rhs(q_tile, lane):
    zero = jnp.zeros_like(q_tile)
    low = lane < LANES // 2
    return jnp.concatenate([jnp.where(low, q_tile, zero), jnp.where(low, zero, q_tile)], axis=0)


def _count_rows(ref, s, pred):
    acc = jnp.zeros((COUNT_ROWS, LANES), I32)
    for c in range(s // COUNT_ROWS):
        r0 = c * COUNT_ROWS
        acc = acc + jnp.where(pred(ref[r0:r0 + COUNT_ROWS, :], r0), 1, 0)
    return acc.sum(axis=0, keepdims=True)


def _attn_prompt_kernel(qi_ref, wt_ref, qb_ref, kid_ref, kd_ref, vt_ref, o_ref, score_ref, bias_ref,
                        lg_ref, e_ref, ot_ref, rhsi_ref, rhsq_ref, *, s_variants, k_sel, n_bits):
    j = pl.program_id(1)
    t0 = j * Q_BLOCK
    gran = s_variants[0]
    lane_q = lax.broadcasted_iota(I32, (Q_BLOCK, LANES), 1)
    n_pairs = N_HEADS // 2
    pairs_per_kv = (N_HEADS // N_KV) // 2
    for p in range(n_pairs):
        rhsi_ref[p] = _split_heads_rhs(qi_ref[:, p * LANES:(p + 1) * LANES], lane_q)
        rhsq_ref[p] = _split_heads_rhs(qb_ref[:, p * LANES:(p + 1) * LANES], lane_q)

    def process(s):
        w = wt_ref[...]
        for c in range(s // SCORE_ROWS):
            r0 = c * SCORE_ROWS
            kid_c = kid_ref[r0:r0 + SCORE_ROWS, :]
            score = None
            for p in range(n_pairs):
                d = lax.dot_general(kid_c, rhsi_ref[p], _NT, preferred_element_type=F32)
                term = (jnp.maximum(d[:, 0:LANES], 0.0) * w[2 * p:2 * p + 1, :]
                        + jnp.maximum(d[:, LANES:], 0.0) * w[2 * p + 1:2 * p + 2, :])
                score = term if score is None else score + term
            if r0 + SCORE_ROWS > s - gran:
                row = r0 + lax.broadcasted_iota(I32, (SCORE_ROWS, LANES), 0)
                tq = t0 + lax.broadcasted_iota(I32, (SCORE_ROWS, LANES), 1)
                score = jnp.where(row <= tq, score, NEG_INF)
            score_ref[r0:r0 + SCORE_ROWS, :] = score

        n_chunks = s // ATTN_ROWS
        steps_per_pair = RADIX_BITS // n_pairs
        unroll = steps_per_pair if s <= RADIX_UNROLL_ALL_BELOW else RADIX_UNROLL
        trips = steps_per_pair // unroll
        jobs_per_trip = -(-n_chunks // trips)
        n_full, n_rem = divmod(n_chunks, jobs_per_trip)
        count_ge = lambda thr_c: _count_rows(score_ref, s, lambda x, r0: x >= thr_c)
        radix_step = _radix_step(count_ge, k_sel)
        carry = _radix_init((1, LANES), s)
        for p in range(n_pairs):
            g = p // pairs_per_kv

            def raw_logits(c, p=p, g=g):
                rows = pl.ds(pl.multiple_of(c * ATTN_ROWS, ATTN_ROWS), ATTN_ROWS)
                lg_ref[p, rows, :] = lax.dot_general(kd_ref[rows, g * LANES:(g + 1) * LANES],
                                                     rhsq_ref[p], _NT, preferred_element_type=F32)

            def trip(k, carry, n_jobs, p=p, raw_logits=raw_logits):
                for jj in range(n_jobs):
                    raw_logits(k * jobs_per_trip + jj)
                for uu in range(unroll):
                    carry = radix_step(p * steps_per_pair + k * unroll + uu, carry)
                return carry

            carry = lax.fori_loop(0, n_full, functools.partial(trip, n_jobs=jobs_per_trip), carry)
            done = n_full
            if n_rem:
                carry = lax.fori_loop(done, done + 1, functools.partial(trip, n_jobs=n_rem), carry)
                done += 1
            carry = lax.fori_loop(done, trips, functools.partial(trip, n_jobs=0), carry)
        thr, n_ge = _select_threshold(*carry)
        bias_ref[0:s, :] = jnp.where(score_ref[0:s, :] >= thr, 0.0, NEG_INF)

        @pl.when(jnp.max(n_ge) > k_sel)
        def _():
            need = k_sel - _count_rows(score_ref, s, lambda x, r0: x > thr)

            def count_eq_below(cand):
                def pred(x, r0):
                    row = r0 + lax.broadcasted_iota(I32, (COUNT_ROWS, LANES), 0)
                    return jnp.where(x == thr, row, cand) < cand
                return _count_rows(score_ref, s, pred)

            cut = _tie_cut(count_eq_below, need, n_bits, (1, LANES))
            score = score_ref[0:s, :]
            row = lax.broadcasted_iota(I32, (s, LANES), 0)
            bias_ref[0:s, :] = jnp.where(score == thr, jnp.where(row <= cut, 0.0, NEG_INF),
                                         jnp.where(score > thr, 0.0, NEG_INF))

        fold = lambda x, op: functools.reduce(
            op, [x[r:r + 4 * SUBLANES, :] for r in range(0, ATTN_ROWS, 4 * SUBLANES)])

        def logits_chunk(p, c, mx):
            sl = slice(c * ATTN_ROWS, (c + 1) * ATTN_ROWS)
            b = bias_ref[sl, :]
            lg = lg_ref[p, sl, :] + jnp.concatenate([b, b], axis=1)
            lg_ref[p, sl, :] = lg
            part = fold(lg, jnp.maximum)
            return part if mx is None else jnp.maximum(mx, part)

        def softmax_chunk(p, c, m, den):
            sl = slice(c * ATTN_ROWS, (c + 1) * ATTN_ROWS)
            e = jnp.exp(lg_ref[p, sl, :] - m)
            e_ref[p % 2, sl, :] = e.astype(BF16)
            part = fold(e, jnp.add)
            return part if den is None else den + part

        mx = None
        for c in range(n_chunks):
            mx = logits_chunk(0, c, mx)
        for p in range(n_pairs):
            g = p // pairs_per_kv
            m = jnp.max(mx, axis=0, keepdims=True)
            mx, den = None, None
            for c in range(n_chunks):
                if p + 1 < n_pairs:
                    mx = logits_chunk(p + 1, c, mx)
                den = softmax_chunk(p, c, m, den)
            pv = jnp.dot(vt_ref[g * HEAD_DIM:(g + 1) * HEAD_DIM, 0:s], e_ref[p % 2, 0:s, :],
                         preferred_element_type=F32)
            pv = pv / jnp.sum(den, axis=0, keepdims=True)
            ot_ref[(2 * p) * HEAD_DIM:(2 * p + 1) * HEAD_DIM, :] = pv[:, 0:LANES]
            ot_ref[(2 * p + 1) * HEAD_DIM:(2 * p + 2) * HEAD_DIM, :] = pv[:, LANES:]
        o_ref[...] = ot_ref[...].T.astype(BF16)

    blocks_per_variant = gran // Q_BLOCK
    for vi, s in enumerate(s_variants):
        if len(s_variants) == 1:
            process(s)
        else:
            pl.when(j // blocks_per_variant == vi)(functools.partial(process, s))


def _attn_prompt_call(qi, wt, qb, kid, kd, vt, batch, seq, k_sel):
    n_tok = batch * seq
    nq = seq // Q_BLOCK
    n_var = max(v for v in (1, 2, 4, 8) if nq % v == 0 and seq // v % ATTN_ROWS == 0)
    s_variants = tuple(seq // n_var * (v + 1) for v in range(n_var))
    qw = qb.shape[1]
    qblk = lambda w: pl.BlockSpec((Q_BLOCK, w), lambda b, j: (b * nq + j, 0))
    per_b = lambda w: pl.BlockSpec((seq, w), lambda b, j: (b, 0))
    return pl.pallas_call(
        functools.partial(_attn_prompt_kernel, s_variants=s_variants, k_sel=k_sel,
                          n_bits=(seq - 1).bit_length()),
        out_shape=jax.ShapeDtypeStruct((n_tok, qw), BF16),
        grid=(batch, nq),
        in_specs=[qblk(qi.shape[1]),
                  pl.BlockSpec((N_IDX_HEADS, Q_BLOCK), lambda b, j: (0, b * nq + j)),
                  qblk(qw), per_b(LANES), per_b(2 * LANES),
                  pl.BlockSpec((LANES, seq), lambda b, j: (0, b))],
        out_specs=qblk(qw),
        scratch_shapes=[pltpu.VMEM((seq, LANES), F32), pltpu.VMEM((seq, LANES), F32),
                        pltpu.VMEM((N_HEADS // 2, seq, 2 * Q_BLOCK), F32),
                        pltpu.VMEM((2, seq, 2 * Q_BLOCK), BF16), pltpu.VMEM((qw, Q_BLOCK), F32),
                        pltpu.VMEM((N_HEADS // 2, 2 * Q_BLOCK, LANES), BF16),
                        pltpu.VMEM((N_HEADS // 2, 2 * Q_BLOCK, LANES), BF16)],
        compiler_params=pltpu.CompilerParams(
            dimension_semantics=("arbitrary", "arbitrary"), vmem_limit_bytes=VMEM_LIMIT_BYTES),
        name="attn_prompt",
    )(qi, wt, qb, kid, kd, vt)


def _page_loop(n_pages, fn):
    def body(pg, carry):
        fn(pg)
        return carry
    lax.fori_loop(0, n_pages, body, 0, unroll=PAGE_LOOP_UNROLL)


def _page_cols(pg, page):
    return pl.ds(pl.multiple_of(pg * page, page), page)


def _sample_chunks(past, chunk):
    return [(c0, min(chunk, past - c0)) for c0 in range(0, past, chunk)] + [(past, LANES)]


def _idx_scores_kernel(pt_ref, qi_ref, wc_ref, kin_ref, cidx_ref, sc_ref, idx_buf, sem, *, n_pages, page,
                       dec_seq, chunk):
    b = pl.program_id(0)
    nb = pl.num_programs(0)
    slot = b % 2
    past = n_pages * page
    n_keys = past + LANES

    def idx_copy(bb, sl, pg):
        return pltpu.make_async_copy(cidx_ref.at[pt_ref[bb, pg]],
                                     idx_buf.at[sl, :, _page_cols(pg, page)], sem.at[sl])

    @pl.when(b == 0)
    def _():
        _page_loop(n_pages, lambda pg: idx_copy(0, 0, pg).start())

    @pl.when(b + 1 < nb)
    def _():
        _page_loop(n_pages, lambda pg: idx_copy(b + 1, 1 - slot, pg).start())

    idx_buf[slot, :, past:n_keys] = kin_ref[...]
    all_pages = idx_buf.at[slot, :, 0:past]
    pltpu.make_async_copy(all_pages, all_pages, sem.at[slot]).wait()

    qi = qi_ref[...]
    wc = wc_ref[...]
    for c0, n in _sample_chunks(past, chunk):
        d = jnp.dot(qi, idx_buf[slot, :, c0:c0 + n].astype(BF16), preferred_element_type=F32)
        sh = jnp.maximum(d, 0.0) * wc
        score = sh[0:SUBLANES, :]
        for hh in range(1, N_IDX_HEADS):
            score = score + sh[hh * SUBLANES:(hh + 1) * SUBLANES, :]
        tslot = lax.broadcasted_iota(I32, (SUBLANES, n), 0)
        col = c0 + lax.broadcasted_iota(I32, (SUBLANES, n), 1)
        score = jnp.where(col <= past + tslot, score, NEG_INF)
        sc_ref[:, c0:c0 + n] = score[0:dec_seq, :]


def _idx_scores_call(page_table, qi_rows, wcol, ki_new, cache_idx, dec_seq):
    dec_batch, n_pages = page_table.shape
    page = cache_idx.shape[2]
    past = n_pages * page
    n_keys = past + LANES
    rows = N_IDX_HEADS * SUBLANES
    per_b = lambda r, w: pl.BlockSpec((None, r, w), lambda b, pt: (b, 0, 0))
    grid_spec = pltpu.PrefetchScalarGridSpec(
        num_scalar_prefetch=1,
        grid=(dec_batch,),
        in_specs=[per_b(rows, D_IDX), per_b(rows, 1), per_b(D_IDX, LANES),
                  pl.BlockSpec(memory_space=pl.ANY)],
        out_specs=per_b(dec_seq, n_keys),
        scratch_shapes=[pltpu.VMEM((2, D_IDX, n_keys), F32), pltpu.SemaphoreType.DMA((2,))],
    )
    return pl.pallas_call(
        functools.partial(_idx_scores_kernel, n_pages=n_pages, page=page, dec_seq=dec_seq,
                          chunk=min(4096, past)),
        out_shape=jax.ShapeDtypeStruct((dec_batch, dec_seq, n_keys), F32),
        grid_spec=grid_spec,
        compiler_params=pltpu.CompilerParams(
            dimension_semantics=("arbitrary",), vmem_limit_bytes=VMEM_LIMIT_BYTES),
        name="idx_scores_sample",
    )(page_table, qi_rows, wcol, ki_new, cache_idx)


def _select_kernel(sc_ref, bias_ref, *, k_sel):
    nq, n_keys = sc_ref.shape
    n_tiles = n_keys // LANES

    def count(pred):
        acc = jnp.zeros((nq, LANES), I32)
        for c in range(n_tiles):
            acc = acc + jnp.where(pred(sc_ref[:, c * LANES:(c + 1) * LANES], c * LANES), 1, 0)
        return jnp.sum(acc, axis=1, keepdims=True)

    count_ge = lambda thr_c: count(lambda x, c0: x >= thr_c)
    thr, n_ge = _select_threshold(*_radix_select(count_ge, k_sel, (nq, 1), n_keys))
    for c in range(n_tiles):
        sl = slice(c * LANES, (c + 1) * LANES)
        bias_ref[:, sl] = jnp.where(sc_ref[:, sl] >= thr, 0.0, NEG_INF)

    @pl.when(jnp.max(n_ge) > k_sel)
    def _():
        need = k_sel - count(lambda x, c0: x > thr)

        def count_eq_below(cand):
            def pred(x, c0):
                col = c0 + lax.broadcasted_iota(I32, (nq, LANES), 1)
                return jnp.where(x == thr, col, cand) < cand
            return count(pred)

        cut = _tie_cut(count_eq_below, need, (n_keys - 1).bit_length(), (nq, 1))
        for c in range(n_tiles):
            sl = slice(c * LANES, (c + 1) * LANES)
            x = sc_ref[:, sl]
            col = c * LANES + lax.broadcasted_iota(I32, (nq, LANES), 1)
            bias_ref[:, sl] = jnp.where(x == thr, jnp.where(col <= cut, 0.0, NEG_INF),
                                        jnp.where(x > thr, 0.0, NEG_INF))


def _select_call(scores, k_sel):
    return pl.pallas_call(
        functools.partial(_select_kernel, k_sel=k_sel),
        out_shape=jax.ShapeDtypeStruct(scores.shape, F32),
        compiler_params=pltpu.CompilerParams(vmem_limit_bytes=VMEM_LIMIT_BYTES),
        name="select_sample",
    )(scores)


def _attn_sample_kernel(pt_ref, qp_ref, kn_ref, vn_ref, bias_ref, ck_ref, cv_ref, o_ref, k_buf, v_buf,
                        lg_ref, sem, *, n_pages, page, dec_seq, chunk):
    b = pl.program_id(0)
    nb = pl.num_programs(0)
    slot = b % 2
    past = n_pages * page
    n_keys = past + LANES
    rows = N_HEADS * SUBLANES

    def kv_copy(src, dst, bb, sl, s_id, pg):
        return pltpu.make_async_copy(src.at[pt_ref[bb, pg]], dst.at[sl, :, _page_cols(pg, page)],
                                     sem.at[s_id])

    def start_all(bb, sl):
        _page_loop(n_pages, lambda pg: kv_copy(ck_ref, k_buf, bb, sl, sl, pg).start())
        _page_loop(n_pages, lambda pg: kv_copy(cv_ref, v_buf, bb, sl, 2 + sl, pg).start())

    @pl.when(b == 0)
    def _():
        start_all(0, 0)

    @pl.when(b + 1 < nb)
    def _():
        start_all(b + 1, 1 - slot)

    k_buf[slot, :, past:n_keys] = kn_ref[...]
    v_buf[slot, :, past:n_keys] = vn_ref[...]
    for buf, s_id in ((k_buf, slot), (v_buf, 2 + slot)):
        all_pages = buf.at[slot, :, 0:past]
        pltpu.make_async_copy(all_pages, all_pages, sem.at[s_id]).wait()

    qp = qp_ref[...]
    chunks = _sample_chunks(past, chunk)
    for c0, n in chunks:
        bias = jnp.concatenate([bias_ref[:, c0:c0 + n], jnp.zeros((SUBLANES - dec_seq, n), F32)], axis=0)
        bias = jnp.concatenate([bias] * N_HEADS, axis=0)
        lg_ref[:, c0:c0 + n] = jnp.dot(qp, k_buf[slot, :, c0:c0 + n].astype(BF16),
                                       preferred_element_type=F32) + bias
    m = jnp.max(lg_ref[...], axis=1, keepdims=True)
    den = jnp.zeros((rows, 1), F32)
    acc = jnp.zeros((rows, LANES), F32)
    for c0, n in chunks:
        e = jnp.exp(lg_ref[:, c0:c0 + n] - m)
        den = den + jnp.sum(e, axis=1, keepdims=True)
        acc = acc + lax.dot_general(e.astype(BF16), v_buf[slot, :, c0:c0 + n].astype(BF16), _NT,
                                    preferred_element_type=F32)
    o_ref[...] = acc / den


def _attn_sample_call(page_table, qp_rows, k_new, v_new, bias, cache_k, cache_v, dec_seq):
    dec_batch, n_pages = page_table.shape
    page = cache_k.shape[2]
    past = n_pages * page
    n_keys = past + LANES
    rows = N_HEADS * SUBLANES
    per_b = lambda r, w: pl.BlockSpec((None, r, w), lambda b, pt: (b, 0, 0))
    any_spec = pl.BlockSpec(memory_space=pl.ANY)
    grid_spec = pltpu.PrefetchScalarGridSpec(
        num_scalar_prefetch=1,
        grid=(dec_batch,),
        in_specs=[per_b(rows, LANES), per_b(LANES, LANES), per_b(LANES, LANES), per_b(dec_seq, n_keys),
                  any_spec, any_spec],
        out_specs=per_b(rows, LANES),
        scratch_shapes=[pltpu.VMEM((2, LANES, n_keys), F32), pltpu.VMEM((2, LANES, n_keys), F32),
                        pltpu.VMEM((rows, n_keys), F32), pltpu.SemaphoreType.DMA((4,))],
    )
    return pl.pallas_call(
        functools.partial(_attn_sample_kernel, n_pages=n_pages, page=page, dec_seq=dec_seq,
                          chunk=min(4096, past)),
        out_shape=jax.ShapeDtypeStruct((dec_batch, rows, LANES), F32),
        grid_spec=grid_spec,
        compiler_params=pltpu.CompilerParams(
            dimension_semantics=("arbitrary",), vmem_limit_bytes=VMEM_LIMIT_BYTES),
        name="attn_sample",
    )(page_table, qp_rows, k_new, v_new, bias, cache_k, cache_v)


def _rope_tables(pos):
    half = ROT_DIM // 2
    inv = np.power(np.float64(ROPE_THETA), -np.arange(half, dtype=np.float64) * (2.0 / ROT_DIM))
    ang = np.asarray(pos, np.float64)[:, None] * inv[None, :]
    cos, sin = np.cos(ang), np.sin(ang)
    n = len(pos)
    rest = HEAD_DIM - ROT_DIM
    zh = np.zeros((n, half))
    c = np.concatenate([cos, cos, np.ones((n, rest))], axis=-1)
    sa = np.concatenate([-sin, zh, np.zeros((n, rest))], axis=-1)
    sb = np.concatenate([zh, sin, np.zeros((n, rest))], axis=-1)
    two = lambda a: jnp.asarray(np.concatenate([a, a], axis=-1), F32)
    return two(c), two(sa), two(sb)


def kernel(x_prompt, x_sample, cache_k, cache_v, cache_idx_k, cache_conv, page_table, ffn1_norm,
           ffn1_w_gate, ffn1_w_up, ffn1_w_down, mix_norm, w_in, conv_w, conv_b, q_norm, k_norm, w_out,
           ffn2_norm, ffn2_w_gate, ffn2_w_up, ffn2_w_down):
    assert IDX_ROT == ROT_DIM and D_IDX == HEAD_DIM and 2 * HEAD_DIM == LANES
    batch, seq, d = x_prompt.shape
    dec_batch, dec_seq, _ = x_sample.shape
    depth, n_pool, page = cache_k.shape[:3]
    n_pages = page_table.shape[1]
    past = n_pages * page
    d_conv = conv_w.shape[-1]
    assert seq & (seq - 1) == 0 and dec_seq & (dec_seq - 1) == 0 and seq % Q_BLOCK == 0
    assert dec_seq <= SUBLANES and dec_seq >= CONV_W - 1 and page == LANES
    n_s = dec_batch * dec_seq
    tm_p = min(512, seq)

    tabs_p = _rope_tables(np.arange(seq))
    tabs_s = _rope_tables(past + np.arange(n_s) % dec_seq)
    seg = jnp.asarray(np.kron(np.eye(LANES // HEAD_DIM), np.full((HEAD_DIM, HEAD_DIM), 1.0 / HEAD_DIM)), BF16)
    two = lambda a: jnp.concatenate([a, a], axis=-1)[None, :]

    xp = x_prompt.reshape(batch * seq, d)
    xs = x_sample.reshape(n_s, d)
    outs_p, outs_s = [], []
    for l in range(depth):
        wg1, wu1, wd1 = (w[l].astype(BF16) for w in (ffn1_w_gate, ffn1_w_up, ffn1_w_down))
        wg2, wu2, wd2 = (w[l].astype(BF16) for w in (ffn2_w_gate, ffn2_w_up, ffn2_w_down))
        n_in = w_in.shape[-1]
        n_proj = -(-(n_in - N_IDX_HEADS - D_IDX + LANES) // LANES) * LANES
        win = jnp.pad(w_in[l], ((0, 0), (0, n_proj - n_in))).astype(BF16)
        wo = w_out[l].astype(BF16)
        g1, gm, g2 = ffn1_norm[l][None, :], mix_norm[l][None, :], ffn2_norm[l][None, :]
        qn, kn = two(q_norm[l]), two(k_norm[l])
        cw, cb = conv_w[l], conv_b[l][None, :]

        x1 = _ffn_call(xp, g1, wg1, wu1, wd1, tm_p)
        (yc, qb, kd, vtb, kid, qi, wt, u_tail, k_t, v_t, ki_t) = _mix_call(
            x1, gm, win, cw, cb, qn, kn, seg, tabs_p, tm_p, seq)
        attn = _attn_prompt_call(qi, wt, qb, kid, kd, vtb, batch, seq, min(K_TOP, seq // 4))
        xp = _ffn_call(x1, g2, wg2, wu2, wd2, tm_p, mix=(yc, attn, wo))
        tiles_per_seq = seq // tm_p
        conv_p = u_tail.reshape(batch, tiles_per_seq, SUBLANES, d_conv)[:, -1, SUBLANES - (CONV_W - 1):]
        to_cache = lambda a: a.reshape(batch, N_KV, HEAD_DIM, seq).transpose(0, 3, 1, 2)
        outs_p.append((to_cache(k_t), to_cache(v_t), ki_t.transpose(0, 2, 1), conv_p))

        x1s = _ffn_call(xs, g1, wg1, wu1, wd1, n_s)
        prev = cache_conv[l]
        zrow = jnp.zeros((dec_batch, 1, d_conv), F32)
        pad = jnp.zeros((dec_batch, dec_seq - 2, d_conv), F32)
        e1 = jnp.concatenate([prev[:, 1:2], zrow, pad], axis=1).reshape(n_s, d_conv)
        e2 = jnp.concatenate([prev[:, 0:1], prev[:, 1:2], pad], axis=1).reshape(n_s, d_conv)
        (ycs, qbs, _, _, _, qis, wts, us, k_ts, v_ts, ki_ts) = _mix_call(
            x1s, gm, win, cw, cb, qn, kn, seg, tabs_s, n_s, dec_seq, prev=(e1, e2))

        def head_rows(a, width):
            a = a.reshape(dec_batch, dec_seq, N_HEADS, width).transpose(0, 2, 1, 3)
            a = jnp.pad(a, ((0, 0), (0, 0), (0, SUBLANES - dec_seq), (0, 0)))
            return a.reshape(dec_batch, N_HEADS * SUBLANES, width)

        def new_tile(a_t):
            r = a_t.shape[1]
            a = a_t[0].reshape(r, dec_batch, dec_seq).transpose(1, 0, 2)
            return jnp.pad(a, ((0, 0), (0, 0), (0, LANES - dec_seq)))

        qi_rows = head_rows(qis, D_IDX)
        q_rows = head_rows(qbs, HEAD_DIM)
        rep = N_HEADS // N_KV
        grp = (jnp.arange(N_HEADS * SUBLANES) // SUBLANES // rep)[None, :, None]
        zq = jnp.zeros_like(q_rows)
        qp_rows = jnp.concatenate([jnp.where(grp == 0, q_rows, zq), jnp.where(grp == 1, q_rows, zq)],
                                  axis=-1)
        wcol = head_rows(wts.T.reshape(n_s, N_IDX_HEADS), 1)
        ck = cache_k[l].transpose(0, 2, 3, 1).reshape(n_pool, LANES, page)
        cv = cache_v[l].transpose(0, 2, 3, 1).reshape(n_pool, LANES, page)
        ci = cache_idx_k[l].transpose(0, 2, 1)
        k_sel_s = min(K_TOP, (past + dec_seq) // 4)
        scores_s = _idx_scores_call(page_table, qi_rows, wcol, new_tile(ki_ts), ci, dec_seq)
        bias_s = _select_call(scores_s.reshape(n_s, past + LANES), k_sel_s)
        o_s = _attn_sample_call(page_table, qp_rows, new_tile(k_ts), new_tile(v_ts),
                                bias_s.reshape(dec_batch, dec_seq, past + LANES), ck, cv, dec_seq)
        o_s = o_s.reshape(dec_batch, N_HEADS, SUBLANES, N_KV, HEAD_DIM)[:, :, :dec_seq]
        hsel = (jnp.arange(N_HEADS) // rep)[None, :, None, None]
        attn_s = jnp.where(hsel == 0, o_s[:, :, :, 0], o_s[:, :, :, 1])
        attn_s = attn_s.transpose(0, 2, 1, 3).reshape(n_s, N_HEADS * HEAD_DIM).astype(BF16)
        xs = _ffn_call(x1s, g2, wg2, wu2, wd2, n_s, mix=(ycs, attn_s, wo))
        conv_s = us.reshape(dec_batch, dec_seq, d_conv)[:, dec_seq - (CONV_W - 1):]
        rows_s = lambda a_t: a_t[0].T.reshape(dec_batch, dec_seq, -1)
        outs_s.append((rows_s(k_ts).reshape(dec_batch, dec_seq, N_KV, HEAD_DIM),
                       rows_s(v_ts).reshape(dec_batch, dec_seq, N_KV, HEAD_DIM),
                       rows_s(ki_ts), conv_s))

    stack = lambda outs, n: jnp.stack([o[n] for o in outs])
    return (xp.reshape(batch, seq, d), xs.reshape(dec_batch, dec_seq, d),
            stack(outs_p, 0), stack(outs_p, 1), stack(outs_p, 2), stack(outs_p, 3),
            stack(outs_s, 0), stack(outs_s, 1), stack(outs_s, 2), stack(outs_s, 3))
```

```python
import functools

import jax
import jax.numpy as jnp
import numpy as np
from jax import lax
from jax.experimental import pallas as pl
from jax.experimental.pallas import tpu as pltpu

F32 = jnp.float32
BF16 = jnp.bfloat16
I32 = jnp.int32

N_HEADS = 8
HEAD_DIM = 64
N_KV = 2
ROT_DIM = HEAD_DIM // 4
ROPE_THETA = 500000.0
N_IDX_HEADS = 8
D_IDX = 64
IDX_ROT = D_IDX // 4
K_TOP = 256
CONV_W = 3
EPS = 1e-6
IDX_SCALE = D_IDX ** -0.5 * N_IDX_HEADS ** -0.5
ATTN_SCALE = HEAD_DIM ** -0.5

LANES = 128
SUBLANES = 8
Q_BLOCK = 128
SCORE_ROWS = 128
COUNT_ROWS = 64
ATTN_ROWS = 256
VMEM_LIMIT_BYTES = 56 * 1024 * 1024

INT_MIN = -2 ** 31
RADIX_BITS = 32
PAGE_LOOP_UNROLL = 8
RADIX_UNROLL = 2
RADIX_UNROLL_ALL_BELOW = 768
LOWEST_FINITE_KEY = INT_MIN + 2 ** 23
NEG_INF = float("-inf")
_NT = (((1,), (1,)), ((), ()))


def _rmsnorm(x, g):
    y = x * lax.rsqrt(jnp.mean(x * x, axis=-1, keepdims=True) + EPS)
    return y * g


def _swiglu_residual(x, g_ref, wg_ref, wu_ref, wd_ref):
    h = _rmsnorm(x, g_ref[...]).astype(BF16)
    gate = jnp.dot(h, wg_ref[...], preferred_element_type=F32)
    up = jnp.dot(h, wu_ref[...], preferred_element_type=F32)
    act = (gate * jax.nn.sigmoid(gate) * up).astype(BF16)
    return x + 0.5 * jnp.dot(act, wd_ref[...], preferred_element_type=F32)


def _ffn_kernel(x_ref, g_ref, wg_ref, wu_ref, wd_ref, o_ref):
    o_ref[...] = _swiglu_residual(x_ref[...], g_ref, wg_ref, wu_ref, wd_ref)


def _out_ffn_kernel(x_ref, yc_ref, at_ref, wo_ref, g_ref, wg_ref, wu_ref, wd_ref, o_ref):
    mixed = jnp.concatenate([yc_ref[...], at_ref[...]], axis=-1)
    x = x_ref[...] + jnp.dot(mixed, wo_ref[...], preferred_element_type=F32)
    o_ref[...] = _swiglu_residual(x, g_ref, wg_ref, wu_ref, wd_ref)


def _const_spec(shape):
    return pl.BlockSpec(shape, lambda i: (0,) * len(shape))


def _ffn_call(x, g, wg, wu, wd, tm, mix=None):
    n_tok, d = x.shape
    d_ff = wg.shape[1]
    tok = lambda w: pl.BlockSpec((tm, w), lambda i: (i, 0))
    w_specs = [_const_spec((1, d)), _const_spec((d, d_ff)), _const_spec((d, d_ff)), _const_spec((d_ff, d))]
    if mix is None:
        body, args, specs = _ffn_kernel, (x, g, wg, wu, wd), [tok(d)] + w_specs
    else:
        yc, at, wo = mix
        body, args = _out_ffn_kernel, (x, yc, at, wo, g, wg, wu, wd)
        specs = [tok(d), tok(yc.shape[1]), tok(at.shape[1]), _const_spec(wo.shape)] + w_specs
    return pl.pallas_call(
        body,
        out_shape=jax.ShapeDtypeStruct((n_tok, d), F32),
        grid=(n_tok // tm,),
        in_specs=specs,
        out_specs=tok(d),
        compiler_params=pltpu.CompilerParams(
            dimension_semantics=("arbitrary",), vmem_limit_bytes=VMEM_LIMIT_BYTES),
        name="ffn" if mix is None else "out_ffn",
    )(*args)


def _seg_mean(x2, seg):
    hi = x2.astype(BF16)
    lo = (x2 - hi.astype(F32)).astype(BF16)
    return (jnp.dot(hi, seg, preferred_element_type=F32)
            + jnp.dot(lo, seg, preferred_element_type=F32))


def _rope(x, c, sa, sb):
    half = ROT_DIM // 2
    n = x.shape[-1]
    return x * c + pltpu.roll(x, n - half, 1) * sa + pltpu.roll(x, half, 1) * sb


def _dup_halves(x, lane):
    r = pltpu.roll(x, LANES // 2, 1)
    low = lane < LANES // 2
    return jnp.where(low, x, r), jnp.where(low, r, x)


def _mix_kernel(*refs, tm, seq_len, has_prev, d_conv, pipelined):
    it = iter(refs)
    x_ref, g_ref, win_ref, cw_ref, cb_ref, qn_ref, kn_ref, seg_ref, cos_ref, sa_ref, sb_ref = (
        next(it) for _ in range(11))
    if has_prev:
        e1_ref, e2_ref = next(it), next(it)
    (yc_ref, qb_ref, kd_ref, vtb_ref, kid_ref, qi_ref, wt_ref, u_ref, kt_ref, vt_ref, kit_ref,
     uext_ref) = (next(it) for _ in range(12))
    step = pl.program_id(0)

    def project():
        h = _rmsnorm(x_ref[...], g_ref[...]).astype(BF16)
        return jnp.dot(h, win_ref[...], preferred_element_type=F32)

    post = functools.partial(
        _mix_post, cw_ref=cw_ref, cb_ref=cb_ref, qn_ref=qn_ref, kn_ref=kn_ref, seg_ref=seg_ref,
        cos_ref=cos_ref, sa_ref=sa_ref, sb_ref=sb_ref, prev_refs=(e1_ref, e2_ref) if has_prev else None,
        out_refs=(yc_ref, qb_ref, kd_ref, vtb_ref, kid_ref, qi_ref, wt_ref, u_ref, kt_ref, vt_ref, kit_ref),
        uext_ref=uext_ref, tm=tm, seq_len=seq_len, d_conv=d_conv)

    @pl.when(step == 0)
    def _():
        uext_ref[0:SUBLANES, :] = jnp.zeros((SUBLANES, d_conv), F32)

    if not pipelined:
        post(project(), step)
        return

    proj_a, proj_b = next(it), next(it)

    @pl.when(step == 0)
    def _():
        proj_b[...] = jnp.zeros(proj_b.shape, F32)

    def both(dst, src):
        dst[...] = project()
        post(src, step - 1)

    pl.when(step % 2 == 0)(functools.partial(both, proj_a, proj_b))
    pl.when(step % 2 == 1)(functools.partial(both, proj_b, proj_a))


def _mix_post(proj, i, *, cw_ref, cb_ref, qn_ref, kn_ref, seg_ref, cos_ref, sa_ref, sb_ref, prev_refs,
              out_refs, uext_ref, tm, seq_len, d_conv):
    (yc_ref, qb_ref, kd_ref, vtb_ref, kid_ref, qi_ref, wt_ref, u_ref, kt_ref, vt_ref, kit_ref) = out_refs
    has_prev = prev_refs is not None
    if has_prev:
        e1_ref, e2_ref = prev_refs
    o_q = 3 * d_conv
    o_k = o_q + N_HEADS * HEAD_DIM
    o_v = o_k + N_KV * HEAD_DIM
    o_qi = o_v + N_KV * HEAD_DIM
    o_ki = o_qi + N_IDX_HEADS * D_IDX

    gb = proj[:, 0:d_conv]
    u = proj[:, d_conv:2 * d_conv] * proj[:, 2 * d_conv:3 * d_conv]

    uext_ref[SUBLANES:SUBLANES + tm, :] = u
    u1 = uext_ref[SUBLANES - 1:SUBLANES - 1 + tm, :]
    u2 = uext_ref[SUBLANES - 2:SUBLANES - 2 + tm, :]
    t = (i * tm + lax.broadcasted_iota(I32, (tm, d_conv), 0)) & (seq_len - 1)
    p1 = e1_ref[...] if has_prev else 0.0
    p2 = e2_ref[...] if has_prev else 0.0
    u1 = jnp.where(t >= 1, u1, p1)
    u2 = jnp.where(t >= 2, u2, p2)
    cw = cw_ref[...]
    conv = cb_ref[...] + cw[0:1, :] * u2
    conv = conv + cw[1:2, :] * u1
    conv = conv + cw[2:3, :] * u
    yc_ref[...] = (gb * conv).astype(BF16)
    tail = u[tm - SUBLANES:tm, :]
    uext_ref[0:SUBLANES, :] = tail
    u_ref[...] = u if has_prev else tail

    c, sa, sb = cos_ref[...], sa_ref[...], sb_ref[...]
    seg = seg_ref[...]
    lane = lax.broadcasted_iota(I32, (tm, LANES), 1)

    def qk_norm_rope(xt, gain):
        ms = _seg_mean(xt * xt, seg)
        return _rope(xt * lax.rsqrt(ms + EPS) * gain, c, sa, sb)

    for j in range(N_HEADS * HEAD_DIM // LANES):
        sl = slice(j * LANES, (j + 1) * LANES)
        qt = qk_norm_rope(proj[:, o_q + j * LANES:o_q + (j + 1) * LANES], qn_ref[...])
        qb_ref[:, sl] = (qt * ATTN_SCALE).astype(BF16)
        qit = _rope(proj[:, o_qi + j * LANES:o_qi + (j + 1) * LANES], c, sa, sb)
        qi_ref[:, sl] = qit.astype(BF16)

    kt = qk_norm_rope(proj[:, o_k:o_v], kn_ref[...])
    kt_ref[...] = kt.T
    k0, k1 = _dup_halves(kt, lane)
    kd_ref[:, 0:LANES] = k0.astype(BF16)
    kd_ref[:, LANES:2 * LANES] = k1.astype(BF16)

    v_t = proj[:, o_v:o_qi].T
    vt_ref[...] = v_t
    vtb_ref[...] = v_t.astype(BF16)

    low = lane < D_IDX
    kiw = _rope(proj[:, o_ki:o_ki + LANES], jnp.where(low, c, 1.0), jnp.where(low, sa, 0.0),
                jnp.where(low, sb, 0.0))
    kid_ref[...] = _dup_halves(kiw, lane)[0].astype(BF16)
    kiw_t = kiw.T
    kit_ref[...] = kiw_t[0:D_IDX, :]
    wt_ref[...] = kiw_t[D_IDX:D_IDX + N_IDX_HEADS, :] * IDX_SCALE


def _mix_call(x, g, win, cw, cb, qn, kn, seg, tables, tm, seq_len, prev=None):
    n_tok, d = x.shape
    d_conv = cw.shape[1]
    n_proj = win.shape[1]
    tps = max(seq_len // tm, 1)
    span = tps * tm
    has_prev = prev is not None
    n_tiles = n_tok // tm
    pipelined = n_tiles > 1
    t_in = (lambda i: jnp.minimum(i, n_tiles - 1)) if pipelined else (lambda i: i)
    t_out = (lambda i: jnp.maximum(i - 1, 0)) if pipelined else (lambda i: i)
    tok = lambda w: pl.BlockSpec((tm, w), lambda i: (t_out(i), 0))
    tab = pl.BlockSpec((tm, LANES), lambda i: (t_out(i) % tps, 0))
    fmaj = lambda r: pl.BlockSpec((None, r, tm), lambda i: (t_out(i) // tps, 0, t_out(i) % tps))
    in_specs = [pl.BlockSpec((tm, d), lambda i: (t_in(i), 0)), _const_spec((1, d)), _const_spec((d, n_proj)), _const_spec(cw.shape),
                _const_spec((1, d_conv)), _const_spec((1, LANES)), _const_spec((1, LANES)),
                _const_spec((LANES, LANES)), tab, tab, tab]
    args = [x, g, win, cw, cb, qn, kn, seg, *tables]
    if has_prev:
        in_specs += [tok(d_conv), tok(d_conv)]
        args += list(prev)
    qw = N_HEADS * HEAD_DIM
    u_rows = tm if has_prev else SUBLANES
    out_shape = [
        jax.ShapeDtypeStruct((n_tok, d_conv), BF16),
        jax.ShapeDtypeStruct((n_tok, qw), BF16),
        jax.ShapeDtypeStruct((n_tok, 2 * LANES), BF16),
        jax.ShapeDtypeStruct((LANES, n_tok), BF16),
        jax.ShapeDtypeStruct((n_tok, LANES), BF16),
        jax.ShapeDtypeStruct((n_tok, N_IDX_HEADS * D_IDX), BF16),
        jax.ShapeDtypeStruct((N_IDX_HEADS, n_tok), F32),
        jax.ShapeDtypeStruct((n_tiles * u_rows, d_conv), F32),
        jax.ShapeDtypeStruct((n_tok // span, LANES, span), F32),
        jax.ShapeDtypeStruct((n_tok // span, LANES, span), F32),
        jax.ShapeDtypeStruct((n_tok // span, D_IDX, span), F32),
    ]
    out_specs = [tok(d_conv), tok(qw), tok(2 * LANES),
                 pl.BlockSpec((LANES, tm), lambda i: (0, t_out(i))), tok(LANES), tok(N_IDX_HEADS * D_IDX),
                 pl.BlockSpec((N_IDX_HEADS, tm), lambda i: (0, t_out(i))),
                 pl.BlockSpec((u_rows, d_conv), lambda i: (t_out(i), 0)),
                 fmaj(LANES), fmaj(LANES), fmaj(D_IDX)]
    return pl.pallas_call(
        functools.partial(_mix_kernel, tm=tm, seq_len=seq_len, has_prev=has_prev, d_conv=d_conv,
                          pipelined=pipelined),
        out_shape=out_shape,
        grid=(n_tiles + 1 if pipelined else n_tiles,),
        in_specs=in_specs,
        out_specs=out_specs,
        scratch_shapes=[pltpu.VMEM((tm + SUBLANES, d_conv), F32)]
        + [pltpu.VMEM((tm, n_proj), F32)] * (2 if pipelined else 0),
        compiler_params=pltpu.CompilerParams(
            dimension_semantics=("arbitrary",), vmem_limit_bytes=VMEM_LIMIT_BYTES),
        name="mix_sample" if has_prev else "mix_prompt",
    )(*args)


def _threshold_float(t):
    return pltpu.bitcast(jnp.where(t < 0, t ^ 0x7FFFFFFF, t), F32)


def _radix_select(count_ge, k_sel, shape, n_all):
    return lax.fori_loop(0, RADIX_BITS, _radix_step(count_ge, k_sel), _radix_init(shape, n_all))


def _radix_init(shape, n_all):
    return jnp.full(shape, INT_MIN, I32), jnp.full(shape, n_all, I32)


def _radix_step(count_ge, k_sel):
    def step(i, carry):
        t, n_t = carry
        cand = t + lax.shift_left(jnp.int32(1), RADIX_BITS - 1 - i)
        n_c = count_ge(_threshold_float(cand))
        take = n_c >= k_sel
        return jnp.where(take, cand, t), jnp.where(take, n_c, n_t)
    return step


def _select_threshold(t, n_t):
    thr = _threshold_float(jnp.maximum(t, LOWEST_FINITE_KEY))
    return thr, jnp.where(t == INT_MIN, 0, n_t)


def _tie_cut(count_eq_below, need, n_bits, shape):
    def body(i, j):
        cand = j + lax.shift_left(jnp.int32(1), n_bits - 1 - i)
        return jnp.where(count_eq_below(cand) < need, cand, j)
    return lax.fori_loop(0, n_bits, body, jnp.zeros(shape, I32))


def _split_heads_rhs(q_tile, lane):
    zero = jnp.zeros_like(q_tile)
    low = lane < LANES // 2
    return jnp.concatenate([jnp.where(low, q_tile, zero), jnp.where(low, zero, q_tile)], axis=0)


def _count_rows(ref, s, pred):
    acc = jnp.zeros((COUNT_ROWS, LANES), I32)
    for c in range(s // COUNT_ROWS):
        r0 = c * COUNT_ROWS
        acc = acc + jnp.where(pred(ref[r0:r0 + COUNT_ROWS, :], r0), 1, 0)
    return acc.sum(axis=0, keepdims=True)


def _attn_prompt_kernel(qi_ref, wt_ref, qb_ref, kid_ref, kd_ref, vt_ref, o_ref, score_ref, bias_ref,
                        lg_ref, e_ref, ot_ref, rhsi_ref, rhsq_ref, *, s_variants, k_sel, n_bits):
    j = pl.program_id(1)
    t0 = j * Q_BLOCK
    gran = s_variants[0]
    lane_q = lax.broadcasted_iota(I32, (Q_BLOCK, LANES), 1)
    n_pairs = N_HEADS // 2
    pairs_per_kv = (N_HEADS // N_KV) // 2
    for p in range(n_pairs):
        rhsi_ref[p] = _split_heads_rhs(qi_ref[:, p * LANES:(p + 1) * LANES], lane_q)
        rhsq_ref[p] = _split_heads_rhs(qb_ref[:, p * LANES:(p + 1) * LANES], lane_q)

    def process(s):
        w = wt_ref[...]
        for c in range(s // SCORE_ROWS):
            r0 = c * SCORE_ROWS
            kid_c = kid_ref[r0:r0 + SCORE_ROWS, :]
            score = None
            for p in range(n_pairs):
                d = lax.dot_general(kid_c, rhsi_ref[p], _NT, preferred_element_type=F32)
                term = (jnp.maximum(d[:, 0:LANES], 0.0) * w[2 * p:2 * p + 1, :]
                        + jnp.maximum(d[:, LANES:], 0.0) * w[2 * p + 1:2 * p + 2, :])
                score = term if score is None else score + term
            if r0 + SCORE_ROWS > s - gran:
                row = r0 + lax.broadcasted_iota(I32, (SCORE_ROWS, LANES), 0)
                tq = t0 + lax.broadcasted_iota(I32, (SCORE_ROWS, LANES), 1)
                score = jnp.where(row <= tq, score, NEG_INF)
            score_ref[r0:r0 + SCORE_ROWS, :] = score

        n_chunks = s // ATTN_ROWS
        steps_per_pair = RADIX_BITS // n_pairs
        unroll = steps_per_pair if s <= RADIX_UNROLL_ALL_BELOW else RADIX_UNROLL
        trips = steps_per_pair // unroll
        jobs_per_trip = -(-n_chunks // trips)
        n_full, n_rem = divmod(n_chunks, jobs_per_trip)
        count_ge = lambda thr_c: _count_rows(score_ref, s, lambda x, r0: x >= thr_c)
        radix_step = _radix_step(count_ge, k_sel)
        carry = _radix_init((1, LANES), s)
        for p in range(n_pairs):
            g = p // pairs_per_kv

            def raw_logits(c, p=p, g=g):
                rows = pl.ds(pl.multiple_of(c * ATTN_ROWS, ATTN_ROWS), ATTN_ROWS)
                lg_ref[p, rows, :] = lax.dot_general(kd_ref[rows, g * LANES:(g + 1) * LANES],
                                                     rhsq_ref[p], _NT, preferred_element_type=F32)

            def trip(k, carry, n_jobs, p=p, raw_logits=raw_logits):
                for jj in range(n_jobs):
                    raw_logits(k * jobs_per_trip + jj)
                for uu in range(unroll):
                    carry = radix_step(p * steps_per_pair + k * unroll + uu, carry)
                return carry

            carry = lax.fori_loop(0, n_full, functools.partial(trip, n_jobs=jobs_per_trip), carry)
            done = n_full
            if n_rem:
                carry = lax.fori_loop(done, done + 1, functools.partial(trip, n_jobs=n_rem), carry)
                done += 1
            carry = lax.fori_loop(done, trips, functools.partial(trip, n_jobs=0), carry)
        thr, n_ge = _select_threshold(*carry)
        bias_ref[0:s, :] = jnp.where(score_ref[0:s, :] >= thr, 0.0, NEG_INF)

        @pl.when(jnp.max(n_ge) > k_sel)
        def _():
            need = k_sel - _count_rows(score_ref, s, lambda x, r0: x > thr)

            def count_eq_below(cand):
                def pred(x, r0):
                    row = r0 + lax.broadcasted_iota(I32, (COUNT_ROWS, LANES), 0)
                    return jnp.where(x == thr, row, cand) < cand
                return _count_rows(score_ref, s, pred)

            cut = _tie_cut(count_eq_below, need, n_bits, (1, LANES))
            score = score_ref[0:s, :]
            row = lax.broadcasted_iota(I32, (s, LANES), 0)
            bias_ref[0:s, :] = jnp.where(score == thr, jnp.where(row <= cut, 0.0, NEG_INF),
                                         jnp.where(score > thr, 0.0, NEG_INF))

        fold = lambda x, op: functools.reduce(
            op, [x[r:r + 4 * SUBLANES, :] for r in range(0, ATTN_ROWS, 4 * SUBLANES)])

        def logits_chunk(p, c, mx):
            sl = slice(c * ATTN_ROWS, (c + 1) * ATTN_ROWS)
            b = bias_ref[sl, :]
            lg = lg_ref[p, sl, :] + jnp.concatenate([b, b], axis=1)
            lg_ref[p, sl, :] = lg
            part = fold(lg, jnp.maximum)
            return part if mx is None else jnp.maximum(mx, part)

        def softmax_chunk(p, c, m, den):
            sl = slice(c * ATTN_ROWS, (c + 1) * ATTN_ROWS)
            e = jnp.exp(lg_ref[p, sl, :] - m)
            e_ref[p % 2, sl, :] = e.astype(BF16)
            part = fold(e, jnp.add)
            return part if den is None else den + part

        mx = None
        for c in range(n_chunks):
            mx = logits_chunk(0, c, mx)
        for p in range(n_pairs):
            g = p // pairs_per_kv
            m = jnp.max(mx, axis=0, keepdims=True)
            mx, den = None, None
            for c in range(n_chunks):
                if p + 1 < n_pairs:
                    mx = logits_chunk(p + 1, c, mx)
                den = softmax_chunk(p, c, m, den)
            pv = jnp.dot(vt_ref[g * HEAD_DIM:(g + 1) * HEAD_DIM, 0:s], e_ref[p % 2, 0:s, :],
                         preferred_element_type=F32)
            pv = pv / jnp.sum(den, axis=0, keepdims=True)
            ot_ref[(2 * p) * HEAD_DIM:(2 * p + 1) * HEAD_DIM, :] = pv[:, 0:LANES]
            ot_ref[(2 * p + 1) * HEAD_DIM:(2 * p + 2) * HEAD_DIM, :] = pv[:, LANES:]
        o_ref[...] = ot_ref[...].T.astype(BF16)

    blocks_per_variant = gran // Q_BLOCK
    for vi, s in enumerate(s_variants):
        if len(s_variants) == 1:
            process(s)
        else:
            pl.when(j // blocks_per_variant == vi)(functools.partial(process, s))


def _attn_prompt_call(qi, wt, qb, kid, kd, vt, batch, seq, k_sel):
    n_tok = batch * seq
    nq = seq // Q_BLOCK
    n_var = max(v for v in (1, 2, 4, 8) if nq % v == 0 and seq // v % ATTN_ROWS == 0)
    s_variants = tuple(seq // n_var * (v + 1) for v in range(n_var))
    qw = qb.shape[1]
    qblk = lambda w: pl.BlockSpec((Q_BLOCK, w), lambda b, j: (b * nq + j, 0))
    per_b = lambda w: pl.BlockSpec((seq, w), lambda b, j: (b, 0))
    return pl.pallas_call(
        functools.partial(_attn_prompt_kernel, s_variants=s_variants, k_sel=k_sel,
                          n_bits=(seq - 1).bit_length()),
        out_shape=jax.ShapeDtypeStruct((n_tok, qw), BF16),
        grid=(batch, nq),
        in_specs=[qblk(qi.shape[1]),
                  pl.BlockSpec((N_IDX_HEADS, Q_BLOCK), lambda b, j: (0, b * nq + j)),
                  qblk(qw), per_b(LANES), per_b(2 * LANES),
                  pl.BlockSpec((LANES, seq), lambda b, j: (0, b))],
        out_specs=qblk(qw),
        scratch_shapes=[pltpu.VMEM((seq, LANES), F32), pltpu.VMEM((seq, LANES), F32),
                        pltpu.VMEM((N_HEADS // 2, seq, 2 * Q_BLOCK), F32),
                        pltpu.VMEM((2, seq, 2 * Q_BLOCK), BF16), pltpu.VMEM((qw, Q_BLOCK), F32),
                        pltpu.VMEM((N_HEADS // 2, 2 * Q_BLOCK, LANES), BF16),
                        pltpu.VMEM((N_HEADS // 2, 2 * Q_BLOCK, LANES), BF16)],
        compiler_params=pltpu.CompilerParams(
            dimension_semantics=("arbitrary", "arbitrary"), vmem_limit_bytes=VMEM_LIMIT_BYTES),
        name="attn_prompt",
    )(qi, wt, qb, kid, kd, vt)


def _page_loop(n_pages, fn):
    def body(pg, carry):
        fn(pg)
        return carry
    lax.fori_loop(0, n_pages, body, 0, unroll=PAGE_LOOP_UNROLL)


def _page_cols(pg, page):
    return pl.ds(pl.multiple_of(pg * page, page), page)


def _sample_chunks(past, chunk):
    return [(c0, min(chunk, past - c0)) for c0 in range(0, past, chunk)] + [(past, LANES)]


def _idx_scores_kernel(pt_ref, qi_ref, wc_ref, kin_ref, cidx_ref, sc_ref, idx_buf, sem, *, n_pages, page,
                       dec_seq, chunk):
    b = pl.program_id(0)
    nb = pl.num_programs(0)
    slot = b % 2
    past = n_pages * page
    n_keys = past + LANES

    def idx_copy(bb, sl, pg):
        return pltpu.make_async_copy(cidx_ref.at[pt_ref[bb, pg]],
                                     idx_buf.at[sl, :, _page_cols(pg, page)], sem.at[sl])

    @pl.when(b == 0)
    def _():
        _page_loop(n_pages, lambda pg: idx_copy(0, 0, pg).start())

    @pl.when(b + 1 < nb)
    def _():
        _page_loop(n_pages, lambda pg: idx_copy(b + 1, 1 - slot, pg).start())

    idx_buf[slot, :, past:n_keys] = kin_ref[...]
    all_pages = idx_buf.at[slot, :, 0:past]
    pltpu.make_async_copy(all_pages, all_pages, sem.at[slot]).wait()

    qi = qi_ref[...]
    wc = wc_ref[...]
    for c0, n in _sample_chunks(past, chunk):
        d = jnp.dot(qi, idx_buf[slot, :, c0:c0 + n].astype(BF16), preferred_element_type=F32)
        sh = jnp.maximum(d, 0.0) * wc
        score = sh[0:SUBLANES, :]
        for hh in range(1, N_IDX_HEADS):
            score = score + sh[hh * SUBLANES:(hh + 1) * SUBLANES, :]
        tslot = lax.broadcasted_iota(I32, (SUBLANES, n), 0)
        col = c0 + lax.broadcasted_iota(I32, (SUBLANES, n), 1)
        score = jnp.where(col <= past + tslot, score, NEG_INF)
        sc_ref[:, c0:c0 + n] = score[0:dec_seq, :]


def _idx_scores_call(page_table, qi_rows, wcol, ki_new, cache_idx, dec_seq):
    dec_batch, n_pages = page_table.shape
    page = cache_idx.shape[2]
    past = n_pages * page
    n_keys = past + LANES
    rows = N_IDX_HEADS * SUBLANES
    per_b = lambda r, w: pl.BlockSpec((None, r, w), lambda b, pt: (b, 0, 0))
    grid_spec = pltpu.PrefetchScalarGridSpec(
        num_scalar_prefetch=1,
        grid=(dec_batch,),
        in_specs=[per_b(rows, D_IDX), per_b(rows, 1), per_b(D_IDX, LANES),
                  pl.BlockSpec(memory_space=pl.ANY)],
        out_specs=per_b(dec_seq, n_keys),
        scratch_shapes=[pltpu.VMEM((2, D_IDX, n_keys), F32), pltpu.SemaphoreType.DMA((2,))],
    )
    return pl.pallas_call(
        functools.partial(_idx_scores_kernel, n_pages=n_pages, page=page, dec_seq=dec_seq,
                          chunk=min(4096, past)),
        out_shape=jax.ShapeDtypeStruct((dec_batch, dec_seq, n_keys), F32),
        grid_spec=grid_spec,
        compiler_params=pltpu.CompilerParams(
            dimension_semantics=("arbitrary",), vmem_limit_bytes=VMEM_LIMIT_BYTES),
        name="idx_scores_sample",
    )(page_table, qi_rows, wcol, ki_new, cache_idx)


def _select_kernel(sc_ref, bias_ref, *, k_sel):
    nq, n_keys = sc_ref.shape
    n_tiles = n_keys // LANES

    def count(pred):
        acc = jnp.zeros((nq, LANES), I32)
        for c in range(n_tiles):
            acc = acc + jnp.where(pred(sc_ref[:, c * LANES:(c + 1) * LANES], c * LANES), 1, 0)
        return jnp.sum(acc, axis=1, keepdims=True)

    count_ge = lambda thr_c: count(lambda x, c0: x >= thr_c)
    thr, n_ge = _select_threshold(*_radix_select(count_ge, k_sel, (nq, 1), n_keys))
    for c in range(n_tiles):
        sl = slice(c * LANES, (c + 1) * LANES)
        bias_ref[:, sl] = jnp.where(sc_ref[:, sl] >= thr, 0.0, NEG_INF)

    @pl.when(jnp.max(n_ge) > k_sel)
    def _():
        need = k_sel - count(lambda x, c0: x > thr)

        def count_eq_below(cand):
            def pred(x, c0):
                col = c0 + lax.broadcasted_iota(I32, (nq, LANES), 1)
                return jnp.where(x == thr, col, cand) < cand
            return count(pred)

        cut = _tie_cut(count_eq_below, need, (n_keys - 1).bit_length(), (nq, 1))
        for c in range(n_tiles):
            sl = slice(c * LANES, (c + 1) * LANES)
            x = sc_ref[:, sl]
            col = c * LANES + lax.broadcasted_iota(I32, (nq, LANES), 1)
            bias_ref[:, sl] = jnp.where(x == thr, jnp.where(col <= cut, 0.0, NEG_INF),
                                        jnp.where(x > thr, 0.0, NEG_INF))


def _select_call(scores, k_sel):
    return pl.pallas_call(
        functools.partial(_select_kernel, k_sel=k_sel),
        out_shape=jax.ShapeDtypeStruct(scores.shape, F32),
        compiler_params=pltpu.CompilerParams(vmem_limit_bytes=VMEM_LIMIT_BYTES),
        name="select_sample",
    )(scores)


def _attn_sample_kernel(pt_ref, qp_ref, kn_ref, vn_ref, bias_ref, ck_ref, cv_ref, o_ref, k_buf, v_buf,
                        lg_ref, sem, *, n_pages, page, dec_seq, chunk):
    b = pl.program_id(0)
    nb = pl.num_programs(0)
    slot = b % 2
    past = n_pages * page
    n_keys = past + LANES
    rows = N_HEADS * SUBLANES

    def kv_copy(src, dst, bb, sl, s_id, pg):
        return pltpu.make_async_copy(src.at[pt_ref[bb, pg]], dst.at[sl, :, _page_cols(pg, page)],
                                     sem.at[s_id])

    def start_all(bb, sl):
        _page_loop(n_pages, lambda pg: kv_copy(ck_ref, k_buf, bb, sl, sl, pg).start())
        _page_loop(n_pages, lambda pg: kv_copy(cv_ref, v_buf, bb, sl, 2 + sl, pg).start())

    @pl.when(b == 0)
    def _():
        start_all(0, 0)

    @pl.when(b + 1 < nb)
    def _():
        start_all(b + 1, 1 - slot)

    k_buf[slot, :, past:n_keys] = kn_ref[...]
    v_buf[slot, :, past:n_keys] = vn_ref[...]
    for buf, s_id in ((k_buf, slot), (v_buf, 2 + slot)):
        all_pages = buf.at[slot, :, 0:past]
        pltpu.make_async_copy(all_pages, all_pages, sem.at[s_id]).wait()

    qp = qp_ref[...]
    chunks = _sample_chunks(past, chunk)
    for c0, n in chunks:
        bias = jnp.concatenate([bias_ref[:, c0:c0 + n], jnp.zeros((SUBLANES - dec_seq, n), F32)], axis=0)
        bias = jnp.concatenate([bias] * N_HEADS, axis=0)
        lg_ref[:, c0:c0 + n] = jnp.dot(qp, k_buf[slot, :, c0:c0 + n].astype(BF16),
                                       preferred_element_type=F32) + bias
    m = jnp.max(lg_ref[...], axis=1, keepdims=True)
    den = jnp.zeros((rows, 1), F32)
    acc = jnp.zeros((rows, LANES), F32)
    for c0, n in chunks:
        e = jnp.exp(lg_ref[:, c0:c0 + n] - m)
        den = den + jnp.sum(e, axis=1, keepdims=True)
        acc = acc + lax.dot_general(e.astype(BF16), v_buf[slot, :, c0:c0 + n].astype(BF16), _NT,
                                    preferred_element_type=F32)
    o_ref[...] = acc / den


def _attn_sample_call(page_table, qp_rows, k_new, v_new, bias, cache_k, cache_v, dec_seq):
    dec_batch, n_pages = page_table.shape
    page = cache_k.shape[2]
    past = n_pages * page
    n_keys = past + LANES
    rows = N_HEADS * SUBLANES
    per_b = lambda r, w: pl.BlockSpec((None, r, w), lambda b, pt: (b, 0, 0))
    any_spec = pl.BlockSpec(memory_space=pl.ANY)
    grid_spec = pltpu.PrefetchScalarGridSpec(
        num_scalar_prefetch=1,
        grid=(dec_batch,),
        in_specs=[per_b(rows, LANES), per_b(LANES, LANES), per_b(LANES, LANES), per_b(dec_seq, n_keys),
                  any_spec, any_spec],
        out_specs=per_b(rows, LANES),
        scratch_shapes=[pltpu.VMEM((2, LANES, n_keys), F32), pltpu.VMEM((2, LANES, n_keys), F32),
                        pltpu.VMEM((rows, n_keys), F32), pltpu.SemaphoreType.DMA((4,))],
    )
    return pl.pallas_call(
        functools.partial(_attn_sample_kernel, n_pages=n_pages, page=page, dec_seq=dec_seq,
                          chunk=min(4096, past)),
        out_shape=jax.ShapeDtypeStruct((dec_batch, rows, LANES), F32),
        grid_spec=grid_spec,
        compiler_params=pltpu.CompilerParams(
            dimension_semantics=("arbitrary",), vmem_limit_bytes=VMEM_LIMIT_BYTES),
        name="attn_sample",
    )(page_table, qp_rows, k_new, v_new, bias, cache_k, cache_v)


def _rope_tables(pos):
    half = ROT_DIM // 2
    inv = np.power(np.float64(ROPE_THETA), -np.arange(half, dtype=np.float64) * (2.0 / ROT_DIM))
    ang = np.asarray(pos, np.float64)[:, None] * inv[None, :]
    cos, sin = np.cos(ang), np.sin(ang)
    n = len(pos)
    rest = HEAD_DIM - ROT_DIM
    zh = np.zeros((n, half))
    c = np.concatenate([cos, cos, np.ones((n, rest))], axis=-1)
    sa = np.concatenate([-sin, zh, np.zeros((n, rest))], axis=-1)
    sb = np.concatenate([zh, sin, np.zeros((n, rest))], axis=-1)
    two = lambda a: jnp.asarray(np.concatenate([a, a], axis=-1), F32)
    return two(c), two(sa), two(sb)


def kernel(x_prompt, x_sample, cache_k, cache_v, cache_idx_k, cache_conv, page_table, ffn1_norm,
           ffn1_w_gate, ffn1_w_up, ffn1_w_down, mix_norm, w_in, conv_w, conv_b, q_norm, k_norm, w_out,
           ffn2_norm, ffn2_w_gate, ffn2_w_up, ffn2_w_down):
    assert IDX_ROT == ROT_DIM and D_IDX == HEAD_DIM and 2 * HEAD_DIM == LANES
    batch, seq, d = x_prompt.shape
    dec_batch, dec_seq, _ = x_sample.shape
    depth, n_pool, page = cache_k.shape[:3]
    n_pages = page_table.shape[1]
    past = n_pages * page
    d_conv = conv_w.shape[-1]
    assert seq & (seq - 1) == 0 and dec_seq & (dec_seq - 1) == 0 and seq % Q_BLOCK == 0
    assert dec_seq <= SUBLANES and dec_seq >= CONV_W - 1 and page == LANES
    n_s = dec_batch * dec_seq
    tm_p = min(512, seq)

    tabs_p = _rope_tables(np.arange(seq))
    tabs_s = _rope_tables(past + np.arange(n_s) % dec_seq)
    seg = jnp.asarray(np.kron(np.eye(LANES // HEAD_DIM), np.full((HEAD_DIM, HEAD_DIM), 1.0 / HEAD_DIM)), BF16)
    two = lambda a: jnp.concatenate([a, a], axis=-1)[None, :]

    xp = x_prompt.reshape(batch * seq, d)
    xs = x_sample.reshape(n_s, d)
    outs_p, outs_s = [], []
    for l in range(depth):
        wg1, wu1, wd1 = (w[l].astype(BF16) for w in (ffn1_w_gate, ffn1_w_up, ffn1_w_down))
        wg2, wu2, wd2 = (w[l].astype(BF16) for w in (ffn2_w_gate, ffn2_w_up, ffn2_w_down))
        n_in = w_in.shape[-1]
        n_proj = -(-(n_in - N_IDX_HEADS - D_IDX + LANES) // LANES) * LANES
        win = jnp.pad(w_in[l].astype(BF16), ((0, 0), (0, n_proj - n_in)))
        wo = w_out[l].astype(BF16)
        g1, gm, g2 = ffn1_norm[l][None, :], mix_norm[l][None, :], ffn2_norm[l][None, :]
        qn, kn = two(q_norm[l]), two(k_norm[l])
        cw, cb = conv_w[l], conv_b[l][None, :]

        x1 = _ffn_call(xp, g1, wg1, wu1, wd1, tm_p)
        (yc, qb, kd, vtb, kid, qi, wt, u_tail, k_t, v_t, ki_t) = _mix_call(
            x1, gm, win, cw, cb, qn, kn, seg, tabs_p, tm_p, seq)
        attn = _attn_prompt_call(qi, wt, qb, kid, kd, vtb, batch, seq, min(K_TOP, seq // 4))
        xp = _ffn_call(x1, g2, wg2, wu2, wd2, tm_p, mix=(yc, attn, wo))
        tiles_per_seq = seq // tm_p
        conv_p = u_tail.reshape(batch, tiles_per_seq, SUBLANES, d_conv)[:, -1, SUBLANES - (CONV_W - 1):]
        to_cache = lambda a: a.reshape(batch, N_KV, HEAD_DIM, seq).transpose(0, 3, 1, 2)
        outs_p.append((to_cache(k_t), to_cache(v_t), ki_t.transpose(0, 2, 1), conv_p))

        x1s = _ffn_call(xs, g1, wg1, wu1, wd1, n_s)
        prev = cache_conv[l]
        zrow = jnp.zeros((dec_batch, 1, d_conv), F32)
        pad = jnp.zeros((dec_batch, dec_seq - 2, d_conv), F32)
        e1 = jnp.concatenate([prev[:, 1:2], zrow, pad], axis=1).reshape(n_s, d_conv)
        e2 = jnp.concatenate([prev[:, 0:1], prev[:, 1:2], pad], axis=1).reshape(n_s, d_conv)
        (ycs, qbs, _, _, _, qis, wts, us, k_ts, v_ts, ki_ts) = _mix_call(
            x1s, gm, win, cw, cb, qn, kn, seg, tabs_s, n_s, dec_seq, prev=(e1, e2))

        def head_rows(a, width):
            a = a.reshape(dec_batch, dec_seq, N_HEADS, width).transpose(0, 2, 1, 3)
            a = jnp.pad(a, ((0, 0), (0, 0), (0, SUBLANES - dec_seq), (0, 0)))
            return a.reshape(dec_batch, N_HEADS * SUBLANES, width)

        def new_tile(a_t):
            r = a_t.shape[1]
            a = a_t[0].reshape(r, dec_batch, dec_seq).transpose(1, 0, 2)
            return jnp.pad(a, ((0, 0), (0, 0), (0, LANES - dec_seq)))

        qi_rows = head_rows(qis, D_IDX)
        q_rows = head_rows(qbs, HEAD_DIM)
        rep = N_HEADS // N_KV
        grp = (jnp.arange(N_HEADS * SUBLANES) // SUBLANES // rep)[None, :, None]
        zq = jnp.zeros_like(q_rows)
        qp_rows = jnp.concatenate([jnp.where(grp == 0, q_rows, zq), jnp.where(grp == 1, q_rows, zq)],
                                  axis=-1)
        wcol = head_rows(wts.T.reshape(n_s, N_IDX_HEADS), 1)
        ck = cache_k[l].transpose(0, 2, 3, 1).reshape(n_pool, LANES, page)
        cv = cache_v[l].transpose(0, 2, 3, 1).reshape(n_pool, LANES, page)
        ci = cache_idx_k[l].transpose(0, 2, 1)
        k_sel_s = min(K_TOP, (past + dec_seq) // 4)
        scores_s = _idx_scores_call(page_table, qi_rows, wcol, new_tile(ki_ts), ci, dec_seq)
        bias_s = _select_call(scores_s.reshape(n_s, past + LANES), k_sel_s)
        o_s = _attn_sample_call(page_table, qp_rows, new_tile(k_ts), new_tile(v_ts),
                                bias_s.reshape(dec_batch, dec_seq, past + LANES), ck, cv, dec_seq)
        o_s = o_s.reshape(dec_batch, N_HEADS, SUBLANES, N_KV, HEAD_DIM)[:, :, :dec_seq]
        hsel = (jnp.arange(N_HEADS) // rep)[None, :, None, None]
        attn_s = jnp.where(hsel == 0, o_s[:, :, :, 0], o_s[:, :, :, 1])
        attn_s = attn_s.transpose(0, 2, 1, 3).reshape(n_s, N_HEADS * HEAD_DIM).astype(BF16)
        xs = _ffn_call(x1s, g2, wg2, wu2, wd2, n_s, mix=(ycs, attn_s, wo))
        conv_s = us.reshape(dec_batch, dec_seq, d_conv)[:, dec_seq - (CONV_W - 1):]
        rows_s = lambda a_t: a_t[0].T.reshape(dec_batch, dec_seq, -1)
        outs_s.append((rows_s(k_ts).reshape(dec_batch, dec_seq, N_KV, HEAD_DIM),
                       rows_s(v_ts).reshape(dec_batch, dec_seq, N_KV, HEAD_DIM),
                       rows_s(ki_ts), conv_s))

    stack = lambda outs, n: jnp.stack([o[n] for o in outs])
    return (xp.reshape(batch, seq, d), xs.reshape(dec_batch, dec_seq, d),
            stack(outs_p, 0), stack(outs_p, 1), stack(outs_p, 2), stack(outs_p, 3),
            stack(outs_s, 0), stack(outs_s, 1), stack(outs_s, 2), stack(outs_s, 3))
```

```python
import functools

import jax
import jax.numpy as jnp
import numpy as np
from jax import lax
from jax.experimental import pallas as pl
from jax.experimental.pallas import tpu as pltpu

F32 = jnp.float32
BF16 = jnp.bfloat16
I32 = jnp.int32

N_HEADS = 8
HEAD_DIM = 64
N_KV = 2
ROT_DIM = HEAD_DIM // 4
ROPE_THETA = 500000.0
N_IDX_HEADS = 8
D_IDX = 64
IDX_ROT = D_IDX // 4
K_TOP = 256
CONV_W = 3
EPS = 1e-6
IDX_SCALE = D_IDX ** -0.5 * N_IDX_HEADS ** -0.5
ATTN_SCALE = HEAD_DIM ** -0.5

LANES = 128
SUBLANES = 8
Q_BLOCK = 128
SCORE_ROWS = 128
COUNT_ROWS = 64
ATTN_ROWS = 256
VMEM_LIMIT_BYTES = 56 * 1024 * 1024

INT_MIN = -2 ** 31
RADIX_BITS = 32
PAGE_LOOP_UNROLL = 8
RADIX_UNROLL = 8
RADIX_UNROLL_ALL_BELOW = 768
LOWEST_FINITE_KEY = INT_MIN + 2 ** 23
NEG_INF = float("-inf")
_NT = (((1,), (1,)), ((), ()))


def _rmsnorm(x, g):
    y = x * lax.rsqrt(jnp.mean(x * x, axis=-1, keepdims=True) + EPS)
    return y * g


def _swiglu_residual(x, g_ref, wg_ref, wu_ref, wd_ref):
    h = _rmsnorm(x, g_ref[...]).astype(BF16)
    gate = jnp.dot(h, wg_ref[...], preferred_element_type=F32)
    up = jnp.dot(h, wu_ref[...], preferred_element_type=F32)
    act = (gate * jax.nn.sigmoid(gate) * up).astype(BF16)
    return x + 0.5 * jnp.dot(act, wd_ref[...], preferred_element_type=F32)


def _ffn_kernel(x_ref, g_ref, wg_ref, wu_ref, wd_ref, o_ref):
    o_ref[...] = _swiglu_residual(x_ref[...], g_ref, wg_ref, wu_ref, wd_ref)


def _out_ffn_kernel(x_ref, yc_ref, at_ref, wo_ref, g_ref, wg_ref, wu_ref, wd_ref, o_ref):
    mixed = jnp.concatenate([yc_ref[...], at_ref[...]], axis=-1)
    x = x_ref[...] + jnp.dot(mixed, wo_ref[...], preferred_element_type=F32)
    o_ref[...] = _swiglu_residual(x, g_ref, wg_ref, wu_ref, wd_ref)


def _const_spec(shape):
    return pl.BlockSpec(shape, lambda i: (0,) * len(shape))


def _ffn_call(x, g, wg, wu, wd, tm, mix=None):
    n_tok, d = x.shape
    d_ff = wg.shape[1]
    tok = lambda w: pl.BlockSpec((tm, w), lambda i: (i, 0))
    w_specs = [_const_spec((1, d)), _const_spec((d, d_ff)), _const_spec((d, d_ff)), _const_spec((d_ff, d))]
    if mix is None:
        body, args, specs = _ffn_kernel, (x, g, wg, wu, wd), [tok(d)] + w_specs
    else:
        yc, at, wo = mix
        body, args = _out_ffn_kernel, (x, yc, at, wo, g, wg, wu, wd)
        specs = [tok(d), tok(yc.shape[1]), tok(at.shape[1]), _const_spec(wo.shape)] + w_specs
    return pl.pallas_call(
        body,
        out_shape=jax.ShapeDtypeStruct((n_tok, d), F32),
        grid=(n_tok // tm,),
        in_specs=specs,
        out_specs=tok(d),
        compiler_params=pltpu.CompilerParams(
            dimension_semantics=("arbitrary",), vmem_limit_bytes=VMEM_LIMIT_BYTES),
        name="ffn" if mix is None else "out_ffn",
    )(*args)


def _seg_mean(x2, seg):
    hi = x2.astype(BF16)
    lo = (x2 - hi.astype(F32)).astype(BF16)
    return (jnp.dot(hi, seg, preferred_element_type=F32)
            + jnp.dot(lo, seg, preferred_element_type=F32))


def _rope(x, c, sa, sb):
    half = ROT_DIM // 2
    n = x.shape[-1]
    return x * c + pltpu.roll(x, n - half, 1) * sa + pltpu.roll(x, half, 1) * sb


def _dup_halves(x, lane):
    r = pltpu.roll(x, LANES // 2, 1)
    low = lane < LANES // 2
    return jnp.where(low, x, r), jnp.where(low, r, x)


def _mix_kernel(*refs, tm, seq_len, has_prev, d_conv, pipelined):
    it = iter(refs)
    x_ref, g_ref, win_ref, cw_ref, cb_ref, qn_ref, kn_ref, seg_ref, cos_ref, sa_ref, sb_ref = (
        next(it) for _ in range(11))
    if has_prev:
        e1_ref, e2_ref = next(it), next(it)
    (yc_ref, qb_ref, kd_ref, vtb_ref, kid_ref, qi_ref, wt_ref, u_ref, kt_ref, vt_ref, kit_ref,
     uext_ref) = (next(it) for _ in range(12))
    step = pl.program_id(0)

    def project():
        h = _rmsnorm(x_ref[...], g_ref[...]).astype(BF16)
        return jnp.dot(h, win_ref[...], preferred_element_type=F32)

    post = functools.partial(
        _mix_post, cw_ref=cw_ref, cb_ref=cb_ref, qn_ref=qn_ref, kn_ref=kn_ref, seg_ref=seg_ref,
        cos_ref=cos_ref, sa_ref=sa_ref, sb_ref=sb_ref, prev_refs=(e1_ref, e2_ref) if has_prev else None,
        out_refs=(yc_ref, qb_ref, kd_ref, vtb_ref, kid_ref, qi_ref, wt_ref, u_ref, kt_ref, vt_ref, kit_ref),
        uext_ref=uext_ref, tm=tm, seq_len=seq_len, d_conv=d_conv)

    @pl.when(step == 0)
    def _():
        uext_ref[0:SUBLANES, :] = jnp.zeros((SUBLANES, d_conv), F32)

    if not pipelined:
        post(project(), step)
        return

    proj_a, proj_b = next(it), next(it)

    @pl.when(step == 0)
    def _():
        proj_b[...] = jnp.zeros(proj_b.shape, F32)

    def both(dst, src):
        dst[...] = project()
        post(src, step - 1)

    pl.when(step % 2 == 0)(functools.partial(both, proj_a, proj_b))
    pl.when(step % 2 == 1)(functools.partial(both, proj_b, proj_a))


def _mix_post(proj, i, *, cw_ref, cb_ref, qn_ref, kn_ref, seg_ref, cos_ref, sa_ref, sb_ref, prev_refs,
              out_refs, uext_ref, tm, seq_len, d_conv):
    (yc_ref, qb_ref, kd_ref, vtb_ref, kid_ref, qi_ref, wt_ref, u_ref, kt_ref, vt_ref, kit_ref) = out_refs
    has_prev = prev_refs is not None
    if has_prev:
        e1_ref, e2_ref = prev_refs
    o_q = 3 * d_conv
    o_k = o_q + N_HEADS * HEAD_DIM
    o_v = o_k + N_KV * HEAD_DIM
    o_qi = o_v + N_KV * HEAD_DIM
    o_ki = o_qi + N_IDX_HEADS * D_IDX

    gb = proj[:, 0:d_conv]
    u = proj[:, d_conv:2 * d_conv] * proj[:, 2 * d_conv:3 * d_conv]

    uext_ref[SUBLANES:SUBLANES + tm, :] = u
    u1 = uext_ref[SUBLANES - 1:SUBLANES - 1 + tm, :]
    u2 = uext_ref[SUBLANES - 2:SUBLANES - 2 + tm, :]
    t = (i * tm + lax.broadcasted_iota(I32, (tm, d_conv), 0)) & (seq_len - 1)
    p1 = e1_ref[...] if has_prev else 0.0
    p2 = e2_ref[...] if has_prev else 0.0
    u1 = jnp.where(t >= 1, u1, p1)
    u2 = jnp.where(t >= 2, u2, p2)
    cw = cw_ref[...]
    conv = cb_ref[...] + cw[0:1, :] * u2
    conv = conv + cw[1:2, :] * u1
    conv = conv + cw[2:3, :] * u
    yc_ref[...] = (gb * conv).astype(BF16)
    tail = u[tm - SUBLANES:tm, :]
    uext_ref[0:SUBLANES, :] = tail
    u_ref[...] = u if has_prev else tail

    c, sa, sb = cos_ref[...], sa_ref[...], sb_ref[...]
    seg = seg_ref[...]
    lane = lax.broadcasted_iota(I32, (tm, LANES), 1)

    def qk_norm_rope(xt, gain):
        ms = _seg_mean(xt * xt, seg)
        return _rope(xt * lax.rsqrt(ms + EPS) * gain, c, sa, sb)

    for j in range(N_HEADS * HEAD_DIM // LANES):
        sl = slice(j * LANES, (j + 1) * LANES)
        qt = qk_norm_rope(proj[:, o_q + j * LANES:o_q + (j + 1) * LANES], qn_ref[...])
        qb_ref[:, sl] = (qt * ATTN_SCALE).astype(BF16)
        qit = _rope(proj[:, o_qi + j * LANES:o_qi + (j + 1) * LANES], c, sa, sb)
        qi_ref[:, sl] = qit.astype(BF16)

    kt = qk_norm_rope(proj[:, o_k:o_v], kn_ref[...])
    kt_ref[...] = kt.T
    k0, k1 = _dup_halves(kt, lane)
    kd_ref[:, 0:LANES] = k0.astype(BF16)
    kd_ref[:, LANES:2 * LANES] = k1.astype(BF16)

    v_t = proj[:, o_v:o_qi].T
    vt_ref[...] = v_t
    vtb_ref[...] = v_t.astype(BF16)

    low = lane < D_IDX
    kiw = _rope(proj[:, o_ki:o_ki + LANES], jnp.where(low, c, 1.0), jnp.where(low, sa, 0.0),
                jnp.where(low, sb, 0.0))
    kid_ref[...] = _dup_halves(kiw, lane)[0].astype(BF16)
    kiw_t = kiw.T
    kit_ref[...] = kiw_t[0:D_IDX, :]
    wt_ref[...] = kiw_t[D_IDX:D_IDX + N_IDX_HEADS, :] * IDX_SCALE


def _mix_call(x, g, win, cw, cb, qn, kn, seg, tables, tm, seq_len, prev=None):
    n_tok, d = x.shape
    d_conv = cw.shape[1]
    n_proj = win.shape[1]
    tps = max(seq_len // tm, 1)
    span = tps * tm
    has_prev = prev is not None
    n_tiles = n_tok // tm
    pipelined = n_tiles > 1
    t_in = (lambda i: jnp.minimum(i, n_tiles - 1)) if pipelined else (lambda i: i)
    t_out = (lambda i: jnp.maximum(i - 1, 0)) if pipelined else (lambda i: i)
    tok = lambda w: pl.BlockSpec((tm, w), lambda i: (t_out(i), 0))
    tab = pl.BlockSpec((tm, LANES), lambda i: (t_out(i) % tps, 0))
    fmaj = lambda r: pl.BlockSpec((None, r, tm), lambda i: (t_out(i) // tps, 0, t_out(i) % tps))
    in_specs = [pl.BlockSpec((tm, d), lambda i: (t_in(i), 0)), _const_spec((1, d)), _const_spec((d, n_proj)), _const_spec(cw.shape),
                _const_spec((1, d_conv)), _const_spec((1, LANES)), _const_spec((1, LANES)),
                _const_spec((LANES, LANES)), tab, tab, tab]
    args = [x, g, win, cw, cb, qn, kn, seg, *tables]
    if has_prev:
        in_specs += [tok(d_conv), tok(d_conv)]
        args += list(prev)
    qw = N_HEADS * HEAD_DIM
    u_rows = tm if has_prev else SUBLANES
    out_shape = [
        jax.ShapeDtypeStruct((n_tok, d_conv), BF16),
        jax.ShapeDtypeStruct((n_tok, qw), BF16),
        jax.ShapeDtypeStruct((n_tok, 2 * LANES), BF16),
        jax.ShapeDtypeStruct((LANES, n_tok), BF16),
        jax.ShapeDtypeStruct((n_tok, LANES), BF16),
        jax.ShapeDtypeStruct((n_tok, N_IDX_HEADS * D_IDX), BF16),
        jax.ShapeDtypeStruct((N_IDX_HEADS, n_tok), F32),
        jax.ShapeDtypeStruct((n_tiles * u_rows, d_conv), F32),
        jax.ShapeDtypeStruct((n_tok // span, LANES, span), F32),
        jax.ShapeDtypeStruct((n_tok // span, LANES, span), F32),
        jax.ShapeDtypeStruct((n_tok // span, D_IDX, span), F32),
    ]
    out_specs = [tok(d_conv), tok(qw), tok(2 * LANES),
                 pl.BlockSpec((LANES, tm), lambda i: (0, t_out(i))), tok(LANES), tok(N_IDX_HEADS * D_IDX),
                 pl.BlockSpec((N_IDX_HEADS, tm), lambda i: (0, t_out(i))),
                 pl.BlockSpec((u_rows, d_conv), lambda i: (t_out(i), 0)),
                 fmaj(LANES), fmaj(LANES), fmaj(D_IDX)]
    return pl.pallas_call(
        functools.partial(_mix_kernel, tm=tm, seq_len=seq_len, has_prev=has_prev, d_conv=d_conv,
                          pipelined=pipelined),
        out_shape=out_shape,
        grid=(n_tiles + 1 if pipelined else n_tiles,),
        in_specs=in_specs,
        out_specs=out_specs,
        scratch_shapes=[pltpu.VMEM((tm + SUBLANES, d_conv), F32)]
        + [pltpu.VMEM((tm, n_proj), F32)] * (2 if pipelined else 0),
        compiler_params=pltpu.CompilerParams(
            dimension_semantics=("arbitrary",), vmem_limit_bytes=VMEM_LIMIT_BYTES),
        name="mix_sample" if has_prev else "mix_prompt",
    )(*args)


def _threshold_float(t):
    return pltpu.bitcast(jnp.where(t < 0, t ^ 0x7FFFFFFF, t), F32)


def _radix_select(count_ge, k_sel, shape, n_all):
    return lax.fori_loop(0, RADIX_BITS, _radix_step(count_ge, k_sel), _radix_init(shape, n_all))


def _radix_init(shape, n_all):
    return jnp.full(shape, INT_MIN, I32), jnp.full(shape, n_all, I32)


def _radix_step(count_ge, k_sel):
    def step(i, carry):
        t, n_t = carry
        cand = t + lax.shift_left(jnp.int32(1), RADIX_BITS - 1 - i)
        n_c = count_ge(_threshold_float(cand))
        take = n_c >= k_sel
        return jnp.where(take, cand, t), jnp.where(take, n_c, n_t)
    return step


def _select_threshold(t, n_t):
    thr = _threshold_float(jnp.maximum(t, LOWEST_FINITE_KEY))
    return thr, jnp.where(t == INT_MIN, 0, n_t)


def _tie_cut(count_eq_below, need, n_bits, shape):
    def body(i, j):
        cand = j + lax.shift_left(jnp.int32(1), n_bits - 1 - i)
        return jnp.where(count_eq_below(cand) < need, cand, j)
    return lax.fori_loop(0, n_bits, body, jnp.zeros(shape, I32))


def _split_heads_rhs(q_tile, lane):
    zero = jnp.zeros_like(q_tile)
    low = lane < LANES // 2
    return jnp.concatenate([jnp.where(low, q_tile, zero), jnp.where(low, zero, q_tile)], axis=0)


def _count_rows(ref, s, pred):
    acc = jnp.zeros((COUNT_ROWS, LANES), I32)
    for c in range(s // COUNT_ROWS):
        r0 = c * COUNT_ROWS
        acc = acc + jnp.where(pred(ref[r0:r0 + COUNT_ROWS, :], r0), 1, 0)
    return acc.sum(axis=0, keepdims=True)


def _attn_prompt_kernel(qi_ref, wt_ref, qb_ref, kid_ref, kd_ref, vt_ref, o_ref, score_ref, bias_ref,
                        lg_ref, e_ref, ot_ref, rhsi_ref, rhsq_ref, *, s_variants, k_sel, n_bits):
    j = pl.program_id(1)
    t0 = j * Q_BLOCK
    gran = s_variants[0]
    lane_q = lax.broadcasted_iota(I32, (Q_BLOCK, LANES), 1)
    n_pairs = N_HEADS // 2
    pairs_per_kv = (N_HEADS // N_KV) // 2
    for p in range(n_pairs):
        rhsi_ref[p] = _split_heads_rhs(qi_ref[:, p * LANES:(p + 1) * LANES], lane_q)
        rhsq_ref[p] = _split_heads_rhs(qb_ref[:, p * LANES:(p + 1) * LANES], lane_q)

    def process(s):
        w = wt_ref[...]
        for c in range(s // SCORE_ROWS):
            r0 = c * SCORE_ROWS
            kid_c = kid_ref[r0:r0 + SCORE_ROWS, :]
            score = None
            for p in range(n_pairs):
                d = lax.dot_general(kid_c, rhsi_ref[p], _NT, preferred_element_type=F32)
                term = (jnp.maximum(d[:, 0:LANES], 0.0) * w[2 * p:2 * p + 1, :]
                        + jnp.maximum(d[:, LANES:], 0.0) * w[2 * p + 1:2 * p + 2, :])
                score = term if score is None else score + term
            if r0 + SCORE_ROWS > s - gran:
                row = r0 + lax.broadcasted_iota(I32, (SCORE_ROWS, LANES), 0)
                tq = t0 + lax.broadcasted_iota(I32, (SCORE_ROWS, LANES), 1)
                score = jnp.where(row <= tq, score, NEG_INF)
            score_ref[r0:r0 + SCORE_ROWS, :] = score

        n_chunks = s // ATTN_ROWS
        steps_per_pair = RADIX_BITS // n_pairs
        unroll = steps_per_pair if s <= RADIX_UNROLL_ALL_BELOW else RADIX_UNROLL
        trips = steps_per_pair // unroll
        jobs_per_trip = -(-n_chunks // trips)
        n_full, n_rem = divmod(n_chunks, jobs_per_trip)
        count_ge = lambda thr_c: _count_rows(score_ref, s, lambda x, r0: x >= thr_c)
        radix_step = _radix_step(count_ge, k_sel)
        carry = _radix_init((1, LANES), s)
        for p in range(n_pairs):
            g = p // pairs_per_kv

            def raw_logits(c, p=p, g=g):
                rows = pl.ds(pl.multiple_of(c * ATTN_ROWS, ATTN_ROWS), ATTN_ROWS)
                lg_ref[p, rows, :] = lax.dot_general(kd_ref[rows, g * LANES:(g + 1) * LANES],
                                                     rhsq_ref[p], _NT, preferred_element_type=F32)

            def trip(k, carry, n_jobs, p=p, raw_logits=raw_logits):
                for jj in range(n_jobs):
                    raw_logits(k * jobs_per_trip + jj)
                for uu in range(unroll):
                    carry = radix_step(p * steps_per_pair + k * unroll + uu, carry)
                return carry

            carry = lax.fori_loop(0, n_full, functools.partial(trip, n_jobs=jobs_per_trip), carry)
            done = n_full
            if n_rem:
                carry = lax.fori_loop(done, done + 1, functools.partial(trip, n_jobs=n_rem), carry)
                done += 1
            carry = lax.fori_loop(done, trips, functools.partial(trip, n_jobs=0), carry)
        thr, n_ge = _select_threshold(*carry)
        bias_ref[0:s, :] = jnp.where(score_ref[0:s, :] >= thr, 0.0, NEG_INF)

        @pl.when(jnp.max(n_ge) > k_sel)
        def _():
            need = k_sel - _count_rows(score_ref, s, lambda x, r0: x > thr)

            def count_eq_below(cand):
                def pred(x, r0):
                    row = r0 + lax.broadcasted_iota(I32, (COUNT_ROWS, LANES), 0)
                    return jnp.where(x == thr, row, cand) < cand
                return _count_rows(score_ref, s, pred)

            cut = _tie_cut(count_eq_below, need, n_bits, (1, LANES))
            score = score_ref[0:s, :]
            row = lax.broadcasted_iota(I32, (s, LANES), 0)
            bias_ref[0:s, :] = jnp.where(score == thr, jnp.where(row <= cut, 0.0, NEG_INF),
                                         jnp.where(score > thr, 0.0, NEG_INF))

        fold = lambda x, op: functools.reduce(
            op, [x[r:r + 4 * SUBLANES, :] for r in range(0, ATTN_ROWS, 4 * SUBLANES)])

        def logits_chunk(p, c, mx):
            sl = slice(c * ATTN_ROWS, (c + 1) * ATTN_ROWS)
            b = bias_ref[sl, :]
            lg = lg_ref[p, sl, :] + jnp.concatenate([b, b], axis=1)
            lg_ref[p, sl, :] = lg
            part = fold(lg, jnp.maximum)
            return part if mx is None else jnp.maximum(mx, part)

        def softmax_chunk(p, c, m, den):
            sl = slice(c * ATTN_ROWS, (c + 1) * ATTN_ROWS)
            e = jnp.exp(lg_ref[p, sl, :] - m)
            e_ref[p % 2, sl, :] = e.astype(BF16)
            part = fold(e, jnp.add)
            return part if den is None else den + part

        mx = None
        for c in range(n_chunks):
            mx = logits_chunk(0, c, mx)
        for p in range(n_pairs):
            g = p // pairs_per_kv
            m = jnp.max(mx, axis=0, keepdims=True)
            mx, den = None, None
            for c in range(n_chunks):
                if p + 1 < n_pairs:
                    mx = logits_chunk(p + 1, c, mx)
                den = softmax_chunk(p, c, m, den)
            pv = jnp.dot(vt_ref[g * HEAD_DIM:(g + 1) * HEAD_DIM, 0:s], e_ref[p % 2, 0:s, :],
                         preferred_element_type=F32)
            pv = pv / jnp.sum(den, axis=0, keepdims=True)
            ot_ref[(2 * p) * HEAD_DIM:(2 * p + 1) * HEAD_DIM, :] = pv[:, 0:LANES]
            ot_ref[(2 * p + 1) * HEAD_DIM:(2 * p + 2) * HEAD_DIM, :] = pv[:, LANES:]
        o_ref[...] = ot_ref[...].T.astype(BF16)

    blocks_per_variant = gran // Q_BLOCK
    for vi, s in enumerate(s_variants):
        if len(s_variants) == 1:
            process(s)
        else:
            pl.when(j // blocks_per_variant == vi)(functools.partial(process, s))


def _attn_prompt_call(qi, wt, qb, kid, kd, vt, batch, seq, k_sel):
    n_tok = batch * seq
    nq = seq // Q_BLOCK
    n_var = max(v for v in (1, 2, 4, 8) if nq % v == 0 and seq // v % ATTN_ROWS == 0)
    s_variants = tuple(seq // n_var * (v + 1) for v in range(n_var))
    qw = qb.shape[1]
    qblk = lambda w: pl.BlockSpec((Q_BLOCK, w), lambda b, j: (b * nq + j, 0))
    per_b = lambda w: pl.BlockSpec((seq, w), lambda b, j: (b, 0))
    return pl.pallas_call(
        functools.partial(_attn_prompt_kernel, s_variants=s_variants, k_sel=k_sel,
                          n_bits=(seq - 1).bit_length()),
        out_shape=jax.ShapeDtypeStruct((n_tok, qw), BF16),
        grid=(batch, nq),
        in_specs=[qblk(qi.shape[1]),
                  pl.BlockSpec((N_IDX_HEADS, Q_BLOCK), lambda b, j: (0, b * nq + j)),
                  qblk(qw), per_b(LANES), per_b(2 * LANES),
                  pl.BlockSpec((LANES, seq), lambda b, j: (0, b))],
        out_specs=qblk(qw),
        scratch_shapes=[pltpu.VMEM((seq, LANES), F32), pltpu.VMEM((seq, LANES), F32),
                        pltpu.VMEM((N_HEADS // 2, seq, 2 * Q_BLOCK), F32),
                        pltpu.VMEM((2, seq, 2 * Q_BLOCK), BF16), pltpu.VMEM((qw, Q_BLOCK), F32),
                        pltpu.VMEM((N_HEADS // 2, 2 * Q_BLOCK, LANES), BF16),
                        pltpu.VMEM((N_HEADS // 2, 2 * Q_BLOCK, LANES), BF16)],
        compiler_params=pltpu.CompilerParams(
            dimension_semantics=("arbitrary", "arbitrary"), vmem_limit_bytes=VMEM_LIMIT_BYTES),
        name="attn_prompt",
    )(qi, wt, qb, kid, kd, vt)


def _page_loop(n_pages, fn):
    def body(pg, carry):
        fn(pg)
        return carry
    lax.fori_loop(0, n_pages, body, 0, unroll=PAGE_LOOP_UNROLL)


def _page_cols(pg, page):
    return pl.ds(pl.multiple_of(pg * page, page), page)


def _sample_chunks(past, chunk):
    return [(c0, min(chunk, past - c0)) for c0 in range(0, past, chunk)] + [(past, LANES)]


def _idx_scores_kernel(pt_ref, qi_ref, wc_ref, kin_ref, cidx_ref, sc_ref, idx_buf, sem, *, n_pages, page,
                       dec_seq, chunk):
    b = pl.program_id(0)
    nb = pl.num_programs(0)
    slot = b % 2
    past = n_pages * page
    n_keys = past + LANES

    def idx_copy(bb, sl, pg):
        return pltpu.make_async_copy(cidx_ref.at[pt_ref[bb, pg]],
                                     idx_buf.at[sl, :, _page_cols(pg, page)], sem.at[sl])

    @pl.when(b == 0)
    def _():
        _page_loop(n_pages, lambda pg: idx_copy(0, 0, pg).start())

    @pl.when(b + 1 < nb)
    def _():
        _page_loop(n_pages, lambda pg: idx_copy(b + 1, 1 - slot, pg).start())

    idx_buf[slot, :, past:n_keys] = kin_ref[...]
    all_pages = idx_buf.at[slot, :, 0:past]
    pltpu.make_async_copy(all_pages, all_pages, sem.at[slot]).wait()

    qi = qi_ref[...]
    wc = wc_ref[...]
    for c0, n in _sample_chunks(past, chunk):
        d = jnp.dot(qi, idx_buf[slot, :, c0:c0 + n].astype(BF16), preferred_element_type=F32)
        sh = jnp.maximum(d, 0.0) * wc
        score = sh[0:SUBLANES, :]
        for hh in range(1, N_IDX_HEADS):
            score = score + sh[hh * SUBLANES:(hh + 1) * SUBLANES, :]
        tslot = lax.broadcasted_iota(I32, (SUBLANES, n), 0)
        col = c0 + lax.broadcasted_iota(I32, (SUBLANES, n), 1)
        score = jnp.where(col <= past + tslot, score, NEG_INF)
        sc_ref[:, c0:c0 + n] = score[0:dec_seq, :]


def _idx_scores_call(page_table, qi_rows, wcol, ki_new, cache_idx, dec_seq):
    dec_batch, n_pages = page_table.shape
    page = cache_idx.shape[2]
    past = n_pages * page
    n_keys = past + LANES
    rows = N_IDX_HEADS * SUBLANES
    per_b = lambda r, w: pl.BlockSpec((None, r, w), lambda b, pt: (b, 0, 0))
    grid_spec = pltpu.PrefetchScalarGridSpec(
        num_scalar_prefetch=1,
        grid=(dec_batch,),
        in_specs=[per_b(rows, D_IDX), per_b(rows, 1), per_b(D_IDX, LANES),
                  pl.BlockSpec(memory_space=pl.ANY)],
        out_specs=per_b(dec_seq, n_keys),
        scratch_shapes=[pltpu.VMEM((2, D_IDX, n_keys), F32), pltpu.SemaphoreType.DMA((2,))],
    )
    return pl.pallas_call(
        functools.partial(_idx_scores_kernel, n_pages=n_pages, page=page, dec_seq=dec_seq,
                          chunk=min(4096, past)),
        out_shape=jax.ShapeDtypeStruct((dec_batch, dec_seq, n_keys), F32),
        grid_spec=grid_spec,
        compiler_params=pltpu.CompilerParams(
            dimension_semantics=("arbitrary",), vmem_limit_bytes=VMEM_LIMIT_BYTES),
        name="idx_scores_sample",
    )(page_table, qi_rows, wcol, ki_new, cache_idx)


def _select_kernel(sc_ref, bias_ref, *, k_sel):
    nq, n_keys = sc_ref.shape
    n_tiles = n_keys // LANES

    def count(pred):
        acc = jnp.zeros((nq, LANES), I32)
        for c in range(n_tiles):
            acc = acc + jnp.where(pred(sc_ref[:, c * LANES:(c + 1) * LANES], c * LANES), 1, 0)
        return jnp.sum(acc, axis=1, keepdims=True)

    count_ge = lambda thr_c: count(lambda x, c0: x >= thr_c)
    thr, n_ge = _select_threshold(*_radix_select(count_ge, k_sel, (nq, 1), n_keys))
    for c in range(n_tiles):
        sl = slice(c * LANES, (c + 1) * LANES)
        bias_ref[:, sl] = jnp.where(sc_ref[:, sl] >= thr, 0.0, NEG_INF)

    @pl.when(jnp.max(n_ge) > k_sel)
    def _():
        need = k_sel - count(lambda x, c0: x > thr)

        def count_eq_below(cand):
            def pred(x, c0):
                col = c0 + lax.broadcasted_iota(I32, (nq, LANES), 1)
                return jnp.where(x == thr, col, cand) < cand
            return count(pred)

        cut = _tie_cut(count_eq_below, need, (n_keys - 1).bit_length(), (nq, 1))
        for c in range(n_tiles):
            sl = slice(c * LANES, (c + 1) * LANES)
            x = sc_ref[:, sl]
            col = c * LANES + lax.broadcasted_iota(I32, (nq, LANES), 1)
            bias_ref[:, sl] = jnp.where(x == thr, jnp.where(col <= cut, 0.0, NEG_INF),
                                        jnp.where(x > thr, 0.0, NEG_INF))


def _select_call(scores, k_sel):
    return pl.pallas_call(
        functools.partial(_select_kernel, k_sel=k_sel),
        out_shape=jax.ShapeDtypeStruct(scores.shape, F32),
        compiler_params=pltpu.CompilerParams(vmem_limit_bytes=VMEM_LIMIT_BYTES),
        name="select_sample",
    )(scores)


def _attn_sample_kernel(pt_ref, qp_ref, kn_ref, vn_ref, bias_ref, ck_ref, cv_ref, o_ref, k_buf, v_buf,
                        lg_ref, sem, *, n_pages, page, dec_seq, chunk):
    b = pl.program_id(0)
    nb = pl.num_programs(0)
    slot = b % 2
    past = n_pages * page
    n_keys = past + LANES
    rows = N_HEADS * SUBLANES

    def kv_copy(src, dst, bb, sl, s_id, pg):
        return pltpu.make_async_copy(src.at[pt_ref[bb, pg]], dst.at[sl, :, _page_cols(pg, page)],
                                     sem.at[s_id])

    def start_all(bb, sl):
        _page_loop(n_pages, lambda pg: kv_copy(ck_ref, k_buf, bb, sl, sl, pg).start())
        _page_loop(n_pages, lambda pg: kv_copy(cv_ref, v_buf, bb, sl, 2 + sl, pg).start())

    @pl.when(b == 0)
    def _():
        start_all(0, 0)

    @pl.when(b + 1 < nb)
    def _():
        start_all(b + 1, 1 - slot)

    k_buf[slot, :, past:n_keys] = kn_ref[...]
    v_buf[slot, :, past:n_keys] = vn_ref[...]
    for buf, s_id in ((k_buf, slot), (v_buf, 2 + slot)):
        all_pages = buf.at[slot, :, 0:past]
        pltpu.make_async_copy(all_pages, all_pages, sem.at[s_id]).wait()

    qp = qp_ref[...]
    chunks = _sample_chunks(past, chunk)
    for c0, n in chunks:
        bias = jnp.concatenate([bias_ref[:, c0:c0 + n], jnp.zeros((SUBLANES - dec_seq, n), F32)], axis=0)
        bias = jnp.concatenate([bias] * N_HEADS, axis=0)
        lg_ref[:, c0:c0 + n] = jnp.dot(qp, k_buf[slot, :, c0:c0 + n].astype(BF16),
                                       preferred_element_type=F32) + bias
    m = jnp.max(lg_ref[...], axis=1, keepdims=True)
    den = jnp.zeros((rows, 1), F32)
    acc = jnp.zeros((rows, LANES), F32)
    for c0, n in chunks:
        e = jnp.exp(lg_ref[:, c0:c0 + n] - m)
        den = den + jnp.sum(e, axis=1, keepdims=True)
        acc = acc + lax.dot_general(e.astype(BF16), v_buf[slot, :, c0:c0 + n].astype(BF16), _NT,
                                    preferred_element_type=F32)
    o_ref[...] = acc / den


def _attn_sample_call(page_table, qp_rows, k_new, v_new, bias, cache_k, cache_v, dec_seq):
    dec_batch, n_pages = page_table.shape
    page = cache_k.shape[2]
    past = n_pages * page
    n_keys = past + LANES
    rows = N_HEADS * SUBLANES
    per_b = lambda r, w: pl.BlockSpec((None, r, w), lambda b, pt: (b, 0, 0))
    any_spec = pl.BlockSpec(memory_space=pl.ANY)
    grid_spec = pltpu.PrefetchScalarGridSpec(
        num_scalar_prefetch=1,
        grid=(dec_batch,),
        in_specs=[per_b(rows, LANES), per_b(LANES, LANES), per_b(LANES, LANES), per_b(dec_seq, n_keys),
                  any_spec, any_spec],
        out_specs=per_b(rows, LANES),
        scratch_shapes=[pltpu.VMEM((2, LANES, n_keys), F32), pltpu.VMEM((2, LANES, n_keys), F32),
                        pltpu.VMEM((rows, n_keys), F32), pltpu.SemaphoreType.DMA((4,))],
    )
    return pl.pallas_call(
        functools.partial(_attn_sample_kernel, n_pages=n_pages, page=page, dec_seq=dec_seq,
                          chunk=min(4096, past)),
        out_shape=jax.ShapeDtypeStruct((dec_batch, rows, LANES), F32),
        grid_spec=grid_spec,
        compiler_params=pltpu.CompilerParams(
            dimension_semantics=("arbitrary",), vmem_limit_bytes=VMEM_LIMIT_BYTES),
        name="attn_sample",
    )(page_table, qp_rows, k_new, v_new, bias, cache_k, cache_v)


def _rope_tables(pos):
    half = ROT_DIM // 2
    inv = np.power(np.float64(ROPE_THETA), -np.arange(half, dtype=np.float64) * (2.0 / ROT_DIM))
    ang = np.asarray(pos, np.float64)[:, None] * inv[None, :]
    cos, sin = np.cos(ang), np.sin(ang)
    n = len(pos)
    rest = HEAD_DIM - ROT_DIM
    zh = np.zeros((n, half))
    c = np.concatenate([cos, cos, np.ones((n, rest))], axis=-1)
    sa = np.concatenate([-sin, zh, np.zeros((n, rest))], axis=-1)
    sb = np.concatenate([zh, sin, np.zeros((n, rest))], axis=-1)
    two = lambda a: jnp.asarray(np.concatenate([a, a], axis=-1), F32)
    return two(c), two(sa), two(sb)


def kernel(x_prompt, x_sample, cache_k, cache_v, cache_idx_k, cache_conv, page_table, ffn1_norm,
           ffn1_w_gate, ffn1_w_up, ffn1_w_down, mix_norm, w_in, conv_w, conv_b, q_norm, k_norm, w_out,
           ffn2_norm, ffn2_w_gate, ffn2_w_up, ffn2_w_down):
    assert IDX_ROT == ROT_DIM and D_IDX == HEAD_DIM and 2 * HEAD_DIM == LANES
    batch, seq, d = x_prompt.shape
    dec_batch, dec_seq, _ = x_sample.shape
    depth, n_pool, page = cache_k.shape[:3]
    n_pages = page_table.shape[1]
    past = n_pages * page
    d_conv = conv_w.shape[-1]
    assert seq & (seq - 1) == 0 and dec_seq & (dec_seq - 1) == 0 and seq % Q_BLOCK == 0
    assert dec_seq <= SUBLANES and dec_seq >= CONV_W - 1 and page == LANES
    n_s = dec_batch * dec_seq
    tm_p = min(512, seq)

    tabs_p = _rope_tables(np.arange(seq))
    tabs_s = _rope_tables(past + np.arange(n_s) % dec_seq)
    seg = jnp.asarray(np.kron(np.eye(LANES // HEAD_DIM), np.full((HEAD_DIM, HEAD_DIM), 1.0 / HEAD_DIM)), BF16)
    two = lambda a: jnp.concatenate([a, a], axis=-1)[None, :]

    xp = x_prompt.reshape(batch * seq, d)
    xs = x_sample.reshape(n_s, d)
    outs_p, outs_s = [], []
    for l in range(depth):
        wg1, wu1, wd1 = (w[l].astype(BF16) for w in (ffn1_w_gate, ffn1_w_up, ffn1_w_down))
        wg2, wu2, wd2 = (w[l].astype(BF16) for w in (ffn2_w_gate, ffn2_w_up, ffn2_w_down))
        n_in = w_in.shape[-1]
        n_proj = -(-(n_in - N_IDX_HEADS - D_IDX + LANES) // LANES) * LANES
        win = jnp.pad(w_in[l].astype(BF16), ((0, 0), (0, n_proj - n_in)))
        wo = w_out[l].astype(BF16)
        g1, gm, g2 = ffn1_norm[l][None, :], mix_norm[l][None, :], ffn2_norm[l][None, :]
        qn, kn = two(q_norm[l]), two(k_norm[l])
        cw, cb = conv_w[l], conv_b[l][None, :]

        x1 = _ffn_call(xp, g1, wg1, wu1, wd1, tm_p)
        (yc, qb, kd, vtb, kid, qi, wt, u_tail, k_t, v_t, ki_t) = _mix_call(
            x1, gm, win, cw, cb, qn, kn, seg, tabs_p, tm_p, seq)
        attn = _attn_prompt_call(qi, wt, qb, kid, kd, vtb, batch, seq, min(K_TOP, seq // 4))
        xp = _ffn_call(x1, g2, wg2, wu2, wd2, tm_p, mix=(yc, attn, wo))
        tiles_per_seq = seq // tm_p
        conv_p = u_tail.reshape(batch, tiles_per_seq, SUBLANES, d_conv)[:, -1, SUBLANES - (CONV_W - 1):]
        to_cache = lambda a: a.reshape(batch, N_KV, HEAD_DIM, seq).transpose(0, 3, 1, 2)
        outs_p.append((to_cache(k_t), to_cache(v_t), ki_t.transpose(0, 2, 1), conv_p))

        x1s = _ffn_call(xs, g1, wg1, wu1, wd1, n_s)
        prev = cache_conv[l]
        zrow = jnp.zeros((dec_batch, 1, d_conv), F32)
        pad = jnp.zeros((dec_batch, dec_seq - 2, d_conv), F32)
        e1 = jnp.concatenate([prev[:, 1:2], zrow, pad], axis=1).reshape(n_s, d_conv)
        e2 = jnp.concatenate([prev[:, 0:1], prev[:, 1:2], pad], axis=1).reshape(n_s, d_conv)
        (ycs, qbs, _, _, _, qis, wts, us, k_ts, v_ts, ki_ts) = _mix_call(
            x1s, gm, win, cw, cb, qn, kn, seg, tabs_s, n_s, dec_seq, prev=(e1, e2))

        def head_rows(a, width):
            a = a.reshape(dec_batch, dec_seq, N_HEADS, width).transpose(0, 2, 1, 3)
            a = jnp.pad(a, ((0, 0), (0, 0), (0, SUBLANES - dec_seq), (0, 0)))
            return a.reshape(dec_batch, N_HEADS * SUBLANES, width)

        def new_tile(a_t):
            r = a_t.shape[1]
            a = a_t[0].reshape(r, dec_batch, dec_seq).transpose(1, 0, 2)
            return jnp.pad(a, ((0, 0), (0, 0), (0, LANES - dec_seq)))

        qi_rows = head_rows(qis, D_IDX)
        q_rows = head_rows(qbs, HEAD_DIM)
        rep = N_HEADS // N_KV
        grp = (jnp.arange(N_HEADS * SUBLANES) // SUBLANES // rep)[None, :, None]
        zq = jnp.zeros_like(q_rows)
        qp_rows = jnp.concatenate([jnp.where(grp == 0, q_rows, zq), jnp.where(grp == 1, q_rows, zq)],
                                  axis=-1)
        wcol = head_rows(wts.T.reshape(n_s, N_IDX_HEADS), 1)
        ck = cache_k[l].transpose(0, 2, 3, 1).reshape(n_pool, LANES, page)
        cv = cache_v[l].transpose(0, 2, 3, 1).reshape(n_pool, LANES, page)
        ci = cache_idx_k[l].transpose(0, 2, 1)
        k_sel_s = min(K_TOP, (past + dec_seq) // 4)
        scores_s = _idx_scores_call(page_table, qi_rows, wcol, new_tile(ki_ts), ci, dec_seq)
        bias_s = _select_call(scores_s.reshape(n_s, past + LANES), k_sel_s)
        o_s = _attn_sample_call(page_table, qp_rows, new_tile(k_ts), new_tile(v_ts),
                                bias_s.reshape(dec_batch, dec_seq, past + LANES), ck, cv, dec_seq)
        o_s = o_s.reshape(dec_batch, N_HEADS, SUBLANES, N_KV, HEAD_DIM)[:, :, :dec_seq]
        hsel = (jnp.arange(N_HEADS) // rep)[None, :, None, None]
        attn_s = jnp.where(hsel == 0, o_s[:, :, :, 0], o_s[:, :, :, 1])
        attn_s = attn_s.transpose(0, 2, 1, 3).reshape(n_s, N_HEADS * HEAD_DIM).astype(BF16)
        xs = _ffn_call(x1s, g2, wg2, wu2, wd2, n_s, mix=(ycs, attn_s, wo))
        conv_s = us.reshape(dec_batch, dec_seq, d_conv)[:, dec_seq - (CONV_W - 1):]
        rows_s = lambda a_t: a_t[0].T.reshape(dec_batch, dec_seq, -1)
        outs_s.append((rows_s(k_ts).reshape(dec_batch, dec_seq, N_KV, HEAD_DIM),
                       rows_s(v_ts).reshape(dec_batch, dec_seq, N_KV, HEAD_DIM),
                       rows_s(ki_ts), conv_s))

    stack = lambda outs, n: jnp.stack([o[n] for o in outs])
    return (xp.reshape(batch, seq, d), xs.reshape(dec_batch, dec_seq, d),
            stack(outs_p, 0), stack(outs_p, 1), stack(outs_p, 2), stack(outs_p, 3),
            stack(outs_s, 0), stack(outs_s, 1), stack(outs_s, 2), stack(outs_s, 3))
```
